```python
import math
import jax, jax.numpy as jnp
from jax import lax
import numpy as np

D_MODEL = 1024
BATCH = 8
SEQ = 4096
DEPTH = 2
DEC_BATCH = 2
DEC_SEQ = 16384
PAST_LEN = 128

HEAD_DIM = 128
N_HEADS_A = 8
N_KV_A = 2
N_HEADS_B = 8
N_KV_B = 2
WINDOW = 128
BLOCK = 128
N_POOL_GROUPS = 4
POOL_WINDOWS = (2, 4, 8, 16)
POOL_GROUP_DIM = D_MODEL // N_POOL_GROUPS
N_BRANCHES = 3
N_META = 16
GRID_W = 64
ROPE_THETA = 10000.0
D_FF = -(-8 * D_MODEL // (3 * 256)) * 256
ALPHA = (2 * DEPTH) ** 0.25
BETA = (8 * DEPTH) ** -0.25
NEG = -1e30
IN_WIDTHS = (N_HEADS_A * HEAD_DIM, N_KV_A * HEAD_DIM, N_KV_A * HEAD_DIM,
             N_HEADS_B * HEAD_DIM, N_KV_B * HEAD_DIM, N_KV_B * HEAD_DIM,
             D_MODEL, N_BRANCHES * D_MODEL)
IN_WIDTH = sum(IN_WIDTHS)

kernel_name = "hybrid_gated_bidir_encoder"


def _layer_norm(x, g, b, eps=1e-5):
    xf = x.astype(jnp.float32)
    mu = jnp.mean(xf, -1, keepdims=True)
    var = jnp.mean(jnp.square(xf - mu), -1, keepdims=True)
    y = (xf - mu) * lax.rsqrt(var + eps) * g.astype(jnp.float32) + b.astype(jnp.float32)
    return y.astype(x.dtype)


def _rms_norm(x, g, eps=1e-6):
    xf = x.astype(jnp.float32)
    y = xf * lax.rsqrt(jnp.mean(xf * xf, -1, keepdims=True) + eps) * g.astype(jnp.float32)
    return y.astype(x.dtype)


def _rope(x, pos, dim):
    inv = ROPE_THETA ** (-jnp.arange(0, dim, 2, dtype=jnp.float32) / dim)
    ang = pos.astype(jnp.float32)[:, None] * inv[None, :]
    cos = jnp.cos(ang)[:, None, :]
    sin = jnp.sin(ang)[:, None, :]
    xf = x.astype(jnp.float32)
    x1, x2 = xf[..., : dim // 2], xf[..., dim // 2:]
    return jnp.concatenate([x1 * cos - x2 * sin, x1 * sin + x2 * cos], -1).astype(x.dtype)


def _axial_rope(x, row, col):
    half = HEAD_DIM // 2
    return jnp.concatenate([_rope(x[..., :half], row, half), _rope(x[..., half:], col, half)], -1)


def _sink_softmax(s, sink):
    m = jnp.maximum(jnp.max(s, -1, keepdims=True), sink)
    e = jnp.exp(s - m)
    return e / (jnp.sum(e, -1, keepdims=True) + jnp.exp(sink - m))


def _dense_attend(q, k, v):
    s = jnp.einsum('bqkgd,bskd->bkgqs', q, k).astype(jnp.float32) * HEAD_DIM ** -0.5
    p = jax.nn.softmax(s, axis=-1).astype(v.dtype)
    return jnp.einsum('bkgqs,bskd->bqkgd', p, v)


def _mixer_global(q, k, v, q_g, k_g):
    B, L = q.shape[:2]
    S = L - N_META
    nb = S // BLOCK
    ROWS = S // GRID_W
    row = jnp.concatenate([-jnp.ones((N_META,), jnp.int32), jnp.repeat(jnp.arange(ROWS, dtype=jnp.int32), GRID_W)])
    col = jnp.concatenate([jnp.arange(N_META, dtype=jnp.int32), jnp.tile(jnp.arange(GRID_W, dtype=jnp.int32), ROWS)])
    q = _axial_rope(_rms_norm(q, q_g), row, col)
    k = _axial_rope(_rms_norm(k, k_g), row, col)
    g = N_HEADS_A // N_KV_A
    q = q.reshape(B, L, N_KV_A, g, HEAD_DIM)
    o_meta = _dense_attend(q[:, :N_META], k, v)
    qb = jnp.moveaxis(q[:, N_META:].reshape(B, nb, BLOCK, N_KV_A, g, HEAD_DIM), 1, 0)
    o_real = lax.map(lambda qi: _dense_attend(qi, k, v), qb)
    o_real = jnp.moveaxis(o_real, 0, 1).reshape(B, S, N_KV_A, g, HEAD_DIM)
    return jnp.concatenate([o_meta, o_real], 1).reshape(B, L, N_HEADS_A * HEAD_DIM)


def _band(t, B, nb):
    tb = t[:, N_META:].reshape(B, nb, BLOCK, t.shape[2], t.shape[3])
    tp = jnp.pad(tb, ((0, 0), (1, 1), (0, 0), (0, 0), (0, 0)))
    return jnp.concatenate([tp[:, :-2], tp[:, 1:-1], tp[:, 2:]], axis=2)


def _mixer_window(q, k, v, sink):
    B, L = q.shape[:2]
    S = L - N_META
    nb = S // BLOCK
    pos = jnp.arange(L)
    q = _rope(q, pos, HEAD_DIM)
    k = _rope(k, pos, HEAD_DIM)
    g = N_HEADS_B // N_KV_B
    q = q.reshape(B, L, N_KV_B, g, HEAD_DIM)
    scale = HEAD_DIM ** -0.5
    sink_f = sink.astype(jnp.float32).reshape(N_KV_B, g)
    k_meta, v_meta = k[:, :N_META], v[:, :N_META]

    n_front = N_META + BLOCK
    s = jnp.einsum('bqkgd,bskd->bkgqs', q[:, :N_META], k[:, :n_front]).astype(jnp.float32) * scale
    qi = jnp.arange(N_META)[:, None]
    kj = jnp.arange(n_front)[None, :]
    s = jnp.where((kj < N_META) | (kj - qi <= WINDOW), s, NEG)
    p = _sink_softmax(s, sink_f[None, :, :, None, None]).astype(v.dtype)
    o_meta = jnp.einsum('bkgqs,bskd->bqkgd', p, v[:, :n_front])

    qr = q[:, N_META:].reshape(B, nb, BLOCK, N_KV_B, g, HEAD_DIM)
    kb = _band(k, B, nb)
    vb = _band(v, B, nb)
    s_m = jnp.einsum('bnqkgd,bskd->bkgnqs', qr, k_meta).astype(jnp.float32) * scale
    s_b = jnp.einsum('bnqkgd,bnskd->bkgnqs', qr, kb).astype(jnp.float32) * scale
    i = jnp.arange(BLOCK)[:, None]
    j = jnp.arange(3 * BLOCK)[None, :]
    blk = jnp.arange(nb)[:, None, None]
    rel = i + BLOCK - j
    in_win = jnp.abs(rel) <= WINDOW
    in_rng = ((j >= BLOCK) | (blk > 0)) & ((j < 2 * BLOCK) | (blk < nb - 1))
    s_b = jnp.where(in_win[None] & in_rng, s_b, NEG)
    p = _sink_softmax(jnp.concatenate([s_m, s_b], -1), sink_f[None, :, :, None, None, None]).astype(v.dtype)
    o_real = (jnp.einsum('bkgnqs,bskd->bnqkgd', p[..., :N_META], v_meta)
              + jnp.einsum('bkgnqs,bnskd->bnqkgd', p[..., N_META:], vb))
    o_real = o_real.reshape(B, S, N_KV_B, g, HEAD_DIM)
    return jnp.concatenate([o_meta, o_real], 1).reshape(B, L, N_HEADS_B * HEAD_DIM)


def _mixer_pool(u, w, scale):
    B, L, C = u.shape
    cs = jnp.pad(jnp.cumsum(u.astype(jnp.float32), axis=1), ((0, 0), (1, 0), (0, 0)))
    t = jnp.arange(L)
    outs = []
    for gi, win in enumerate(POOL_WINDOWS):
        lo = jnp.clip(t - win // 2, 0, L)
        hi = jnp.clip(t - win // 2 + win, 0, L)
        sl = slice(gi * POOL_GROUP_DIM, (gi + 1) * POOL_GROUP_DIM)
        c = cs[:, :, sl]
        mean = (c[:, hi] - c[:, lo]) / (hi - lo).astype(jnp.float32)[None, :, None]
        outs.append(mean - u[:, :, sl].astype(jnp.float32))
    d = jnp.stack(outs, 2).astype(u.dtype)
    y = jnp.einsum('blgc,gcd->blgd', d, w).reshape(B, L, C)
    return y * scale


def _split_in(u):
    parts = []
    start = 0
    for wdt in IN_WIDTHS:
        parts.append(u[..., start:start + wdt])
        start += wdt
    return parts


def _trunk(x, meta_tokens, w_in, q_norm_g, k_norm_g, sink_logit, pool_w, pool_scale,
           w_branch_a, w_branch_b, w_out, ln1_g, ln1_b, w_up, w_down, ln2_g, ln2_b):
    B = x.shape[0]
    D = D_MODEL
    meta = jnp.broadcast_to(meta_tokens.astype(x.dtype)[None], (B, N_META, D))
    h = jnp.concatenate([meta, x], axis=1)
    L = h.shape[1]
    for l in range(DEPTH):
        u = h @ w_in[l]
        qa, ka, va, qb, kb, vb, uc, ug = _split_in(u)
        oa = _mixer_global(qa.reshape(B, L, N_HEADS_A, HEAD_DIM), ka.reshape(B, L, N_KV_A, HEAD_DIM),
                           va.reshape(B, L, N_KV_A, HEAD_DIM), q_norm_g[l], k_norm_g[l])
        ob = _mixer_window(qb.reshape(B, L, N_HEADS_B, HEAD_DIM), kb.reshape(B, L, N_KV_B, HEAD_DIM),
                           vb.reshape(B, L, N_KV_B, HEAD_DIM), sink_logit[l])
        ya = oa @ w_branch_a[l]
        yb = ob @ w_branch_b[l]
        yc = _mixer_pool(uc, pool_w[l], pool_scale[l])
        gates = jax.nn.sigmoid(ug.astype(jnp.float32)).reshape(B, L, N_BRANCHES, D)
        merged = (gates[:, :, 0] * ya + gates[:, :, 1] * yb + gates[:, :, 2] * yc).astype(h.dtype)
        h = _layer_norm(ALPHA * h + merged @ w_out[l], ln1_g[l], ln1_b[l])
        gu = h @ w_up[l]
        f = (jax.nn.silu(gu[..., :D_FF]) * gu[..., D_FF:]) @ w_down[l]
        h = _layer_norm(ALPHA * h + f, ln2_g[l], ln2_b[l])
    return h[:, N_META:]


def setup_inputs(seed: int = 0) -> dict:
    key = jax.random.key(seed)
    ks = jax.random.split(key, 18)
    D = D_MODEL

    def nrm(k, shape, s):
        return jax.random.normal(k, shape, jnp.float32) * s

    return {
        "x_prompt": nrm(ks[0], (BATCH, SEQ, D), 1.0),
        "x_sample": nrm(ks[1], (DEC_BATCH, DEC_SEQ, D), 1.0),
        "meta_tokens": nrm(ks[2], (N_META, D), 1.0),
        "w_in": nrm(ks[3], (DEPTH, D, IN_WIDTH), D ** -0.5),
        "q_norm_g": 1.0 + nrm(ks[4], (DEPTH, HEAD_DIM), 0.1),
        "k_norm_g": 1.0 + nrm(ks[5], (DEPTH, HEAD_DIM), 0.1),
        "sink_logit": nrm(ks[6], (DEPTH, N_HEADS_B), 0.5),
        "pool_w": nrm(ks[7], (DEPTH, N_POOL_GROUPS, POOL_GROUP_DIM, POOL_GROUP_DIM), POOL_GROUP_DIM ** -0.5),
        "pool_scale": 1.0 + nrm(ks[8], (DEPTH, D), 0.1),
        "w_branch_a": nrm(ks[9], (DEPTH, N_HEADS_A * HEAD_DIM, D), (N_HEADS_A * HEAD_DIM) ** -0.5),
        "w_branch_b": nrm(ks[10], (DEPTH, N_HEADS_B * HEAD_DIM, D), (N_HEADS_B * HEAD_DIM) ** -0.5),
        "w_out": nrm(ks[11], (DEPTH, D, D), D ** -0.5 * BETA),
        "ln1_g": 1.0 + nrm(ks[12], (DEPTH, D), 0.1),
        "ln1_b": nrm(ks[13], (DEPTH, D), 0.02),
        "w_up": nrm(ks[14], (DEPTH, D, 2 * D_FF), D ** -0.5),
        "w_down": nrm(ks[15], (DEPTH, D_FF, D), D_FF ** -0.5 * BETA),
        "ln2_g": 1.0 + nrm(ks[16], (DEPTH, D), 0.1),
        "ln2_b": nrm(ks[17], (DEPTH, D), 0.02),
    }


def reference(x_prompt, x_sample, meta_tokens, w_in, q_norm_g, k_norm_g, sink_logit, pool_w, pool_scale,
              w_branch_a, w_branch_b, w_out, ln1_g, ln1_b, w_up, w_down, ln2_g, ln2_b):
    y_prompt = _trunk(x_prompt, meta_tokens, w_in, q_norm_g, k_norm_g, sink_logit, pool_w, pool_scale,
                      w_branch_a, w_branch_b, w_out, ln1_g, ln1_b, w_up, w_down, ln2_g, ln2_b)
    y_sample = _trunk(x_sample, meta_tokens, w_in, q_norm_g, k_norm_g, sink_logit, pool_w, pool_scale,
                      w_branch_a, w_branch_b, w_out, ln1_g, ln1_b, w_up, w_down, ln2_g, ln2_b)
    return (y_prompt, y_sample)
```

```python
import functools

import jax
import jax.numpy as jnp
from jax import lax
from jax.experimental import pallas as pl
from jax.experimental.pallas import tpu as pltpu

D_MODEL = 1024
HEAD_DIM = 128
N_HEADS = 8
N_KV = 2
GROUP = N_HEADS // N_KV
WINDOW = 128
N_META = 16
META_ROWS = 128
GRID_W = 64
ROPE_THETA = 10000.0
POOL_WINDOWS = (2, 4, 8, 16)
POOL_DIM = D_MODEL // len(POOL_WINDOWS)
POOL_HALO = 8
D_FF = 2816
DEPTH = 2
ALPHA = (2 * DEPTH) ** 0.25
NEG = -1e30
QKV_WIDTH = 4096

MM_DTYPE = jnp.bfloat16
F32 = jnp.float32

V7X_VMEM_BYTES = 64 * 1024 * 1024
VMEM_CAP = V7X_VMEM_BYTES - 6 * 1024 * 1024

ROW_TILE = 512
ATTN_TQ = 256
ATTN_TK = 256
WIN_TQ = 256
WIN_BLK = 128
FF_CHUNK = 256

_NT = (((1,), (1,)), ((), ()))


def _vmem_limit(block_bytes, scratch_bytes=0, temp_bytes=0):
    est = 2 * block_bytes + scratch_bytes + temp_bytes + (4 << 20)
    return int(min(max(est, 16 << 20), VMEM_CAP))


def _nbytes(shape, dtype):
    n = 1
    for s in shape:
        n *= s
    return n * jnp.dtype(dtype).itemsize


def _resident(shape):
    zeros = (0,) * len(shape)
    return pl.BlockSpec(shape, lambda *_: zeros, pipeline_mode=pl.Buffered(1))


def _layer_norm(x, g, b):
    mu = jnp.mean(x, axis=-1, keepdims=True)
    xc = x - mu
    var = jnp.mean(xc * xc, axis=-1, keepdims=True)
    return xc * lax.rsqrt(var + 1e-5) * g + b


def _proj_kernel(x_ref, w_ref, qg_ref, kg_ref, ca_ref, sa_ref, cb_ref, sb_ref,
                 qa_ref, ka_ref, va_ref, qb_ref, kb_ref, vb_ref, uc_ref, *, tm, cka, ckb):
    xb = x_ref[0].astype(MM_DTYPE)
    ca = ca_ref[...]
    sa = sa_ref[...]
    cb = cb_ref[...]
    sb = sb_ref[...]
    qg = qg_ref[...]
    kg = kg_ref[...]
    lane = lax.broadcasted_iota(jnp.int32, (tm, HEAD_DIM), 1)
    low_quarter = (lane % 64) < 32

    def rms(x, g):
        return x * lax.rsqrt(jnp.mean(x * x, axis=-1, keepdims=True) + 1e-6) * g

    def rope_a(x):
        partner = jnp.where(low_quarter, pltpu.roll(x, 96, 1), pltpu.roll(x, 32, 1))
        return x * ca + partner * sa

    def rope_b(x):
        return x * cb + pltpu.roll(x, 64, 1) * sb

    def cols(c0, n):
        return jnp.dot(xb, w_ref[:, c0:c0 + n], preferred_element_type=F32)

    def head(u, j):
        return u[:, j * HEAD_DIM:(j + 1) * HEAD_DIM]

    def store_vt(v_ref, v, ck):
        for j in range(tm // ck):
            v_ref[0, j] = v[j * ck:(j + 1) * ck, :].T.astype(v_ref.dtype)

    for blk in range(2):
        u = cols(blk * 512, 512)
        for j in range(4):
            h = blk * 4 + j
            qa_ref[0, :, h * HEAD_DIM:(h + 1) * HEAD_DIM] = rope_a(rms(head(u, j), qg)).astype(qa_ref.dtype)
    u = cols(1024, 512)
    for j in range(2):
        ka_ref[0, :, j * HEAD_DIM:(j + 1) * HEAD_DIM] = rope_a(rms(head(u, j), kg)).astype(ka_ref.dtype)
    store_vt(va_ref, u[:, 256:512], cka)
    for blk in range(2):
        u = cols(1536 + blk * 512, 512)
        for j in range(4):
            h = blk * 4 + j
            qb_ref[0, :, h * HEAD_DIM:(h + 1) * HEAD_DIM] = rope_b(head(u, j)).astype(qb_ref.dtype)
    u = cols(2560, 512)
    for j in range(2):
        kb_ref[0, :, j * HEAD_DIM:(j + 1) * HEAD_DIM] = rope_b(head(u, j)).astype(kb_ref.dtype)
    store_vt(vb_ref, u[:, 256:512], ckb)
    for blk in range(2):
        uc_ref[0, :, blk * 512:(blk + 1) * 512] = cols(3072 + blk * 512, 512)


def _proj(x, w_qkv, qg, kg, tabs, *, tm, cka, ckb):
    B, T, D = x.shape
    nt = T // tm
    ca, sa, cb, sb = tabs
    row = lambda b, i: (b, i, 0)
    tab = pl.BlockSpec((tm, HEAD_DIM), lambda b, i: (i, 0))
    small = pl.BlockSpec((1, HEAD_DIM), lambda b, i: (0, 0))
    out_shape = (
        jax.ShapeDtypeStruct((B, T, 1024), MM_DTYPE),
        jax.ShapeDtypeStruct((B, T, 256), MM_DTYPE),
        jax.ShapeDtypeStruct((B, T // cka, 256, cka), MM_DTYPE),
        jax.ShapeDtypeStruct((B, T, 1024), MM_DTYPE),
        jax.ShapeDtypeStruct((B, T, 256), MM_DTYPE),
        jax.ShapeDtypeStruct((B, T // ckb, 256, ckb), MM_DTYPE),
        jax.ShapeDtypeStruct((B, T, 1024), F32),
    )
    out_specs = (
        pl.BlockSpec((1, tm, 1024), row),
        pl.BlockSpec((1, tm, 256), row),
        pl.BlockSpec((1, tm // cka, 256, cka), lambda b, i: (b, i, 0, 0)),
        pl.BlockSpec((1, tm, 1024), row),
        pl.BlockSpec((1, tm, 256), row),
        pl.BlockSpec((1, tm // ckb, 256, ckb), lambda b, i: (b, i, 0, 0)),
        pl.BlockSpec((1, tm, 1024), row),
    )
    blocks = (_nbytes((tm, D), F32) + 4 * _nbytes((tm, HEAD_DIM), F32)
              + _nbytes((tm, 3072), MM_DTYPE) + _nbytes((tm, 1024), F32))
    return pl.pallas_call(
        functools.partial(_proj_kernel, tm=tm, cka=cka, ckb=ckb),
        grid=(B, nt),
        in_specs=[pl.BlockSpec((1, tm, D), row), _resident(w_qkv.shape), small, small, tab, tab, tab, tab],
        out_specs=out_specs,
        out_shape=out_shape,
        compiler_params=pltpu.CompilerParams(
            dimension_semantics=("parallel", "parallel"),
            vmem_limit_bytes=_vmem_limit(blocks, _nbytes(w_qkv.shape, MM_DTYPE), 8 * _nbytes((tm, 512), F32))),
        name="proj",
    )(x, w_qkv, qg, kg, ca, sa, cb, sb)


def _attn_a_kernel(q_ref, k_ref, vt_ref, km_ref, vmt_ref, o_ref, acc_ref, m_ref, l_ref, *, tq, tk, n_chunks):
    scale = HEAD_DIM ** -0.5
    key_row = lax.broadcasted_iota(jnp.int32, (META_ROWS, tq), 0)
    km = km_ref[0]
    vmt = vmt_ref[0, 0]
    for h in range(GROUP):
        qh = q_ref[0, :, h * HEAD_DIM:(h + 1) * HEAD_DIM]
        s = lax.dot_general(km, qh, _NT, preferred_element_type=F32) * scale
        s = jnp.where(key_row < N_META, s, NEG)
        m = jnp.max(s, axis=0, keepdims=True)
        p = jnp.exp(s - m)
        m_ref[h] = m
        l_ref[h] = jnp.sum(p, axis=0, keepdims=True)
        acc_ref[h] = jnp.dot(vmt, p.astype(MM_DTYPE), preferred_element_type=F32)

    def body(c, carry):
        kc = k_ref[0, pl.ds(pl.multiple_of(c * tk, tk), tk), :]
        vc = vt_ref[0, c]
        for h in range(GROUP):
            qh = q_ref[0, :, h * HEAD_DIM:(h + 1) * HEAD_DIM]
            s = lax.dot_general(kc, qh, _NT, preferred_element_type=F32) * scale
            m_old = m_ref[h]
            m_new = jnp.maximum(m_old, jnp.max(s, axis=0, keepdims=True))
            a = jnp.exp(m_old - m_new)
            p = jnp.exp(s - m_new)
            l_ref[h] = a * l_ref[h] + jnp.sum(p, axis=0, keepdims=True)
            acc_ref[h] = a * acc_ref[h] + jnp.dot(vc, p.astype(MM_DTYPE), preferred_element_type=F32)
            m_ref[h] = m_new
        return carry

    lax.fori_loop(0, n_chunks, body, 0)
    for h in range(GROUP):
        o = acc_ref[h] / l_ref[h]
        o_ref[0, :, h * HEAD_DIM:(h + 1) * HEAD_DIM] = o.T.astype(o_ref.dtype)


def _attn_a(q, k, vt, km, vmt, *, tq, tk):
    B, Tq, _ = q.shape
    S = k.shape[1]
    n_chunks = S // tk
    blocks = (_nbytes((tq, 512), MM_DTYPE) * 2 + 2 * _nbytes((S, HEAD_DIM), MM_DTYPE)
              + 2 * _nbytes((META_ROWS, HEAD_DIM), MM_DTYPE))
    scratch = GROUP * (_nbytes((HEAD_DIM, tq), F32) + 2 * _nbytes((8, tq), F32))
    return pl.pallas_call(
        functools.partial(_attn_a_kernel, tq=tq, tk=tk, n_chunks=n_chunks),
        grid=(B, N_KV, Tq // tq),
        in_specs=[
            pl.BlockSpec((1, tq, GROUP * HEAD_DIM), lambda b, g, i: (b, i, g)),
            pl.BlockSpec((1, S, HEAD_DIM), lambda b, g, i: (b, 0, g)),
            pl.BlockSpec((1, n_chunks, HEAD_DIM, tk), lambda b, g, i: (b, 0, g, 0)),
            pl.BlockSpec((1, META_ROWS, HEAD_DIM), lambda b, g, i: (b, 0, g)),
            pl.BlockSpec((1, 1, HEAD_DIM, META_ROWS), lambda b, g, i: (b, 0, g, 0)),
        ],
        out_specs=pl.BlockSpec((1, tq, GROUP * HEAD_DIM), lambda b, g, i: (b, i, g)),
        out_shape=jax.ShapeDtypeStruct((B, Tq, N_HEADS * HEAD_DIM), MM_DTYPE),
        scratch_shapes=[
            pltpu.VMEM((GROUP, HEAD_DIM, tq), F32),
            pltpu.VMEM((GROUP, 1, tq), F32),
            pltpu.VMEM((GROUP, 1, tq), F32),
        ],
        compiler_params=pltpu.CompilerParams(
            dimension_semantics=("parallel", "parallel", "parallel"),
            vmem_limit_bytes=_vmem_limit(blocks, scratch, 6 * _nbytes((tk, tq), F32))),
        name="attn_global",
    )(q, k, vt, km, vmt)


def _win_heads(sink_ref, q_ref, k_all, vt_all, mask, o_ref):
    scale = HEAD_DIM ** -0.5
    g = pl.program_id(1)
    for h in range(GROUP):
        qh = q_ref[0, :, h * HEAD_DIM:(h + 1) * HEAD_DIM]
        s = lax.dot_general(k_all, qh, _NT, preferred_element_type=F32) * scale
        s = jnp.where(mask, s, NEG)
        sink = sink_ref[g * GROUP + h]
        m = jnp.maximum(jnp.max(s, axis=0, keepdims=True), sink)
        p = jnp.exp(s - m)
        den = jnp.sum(p, axis=0, keepdims=True) + jnp.exp(sink - m)
        o = jnp.dot(vt_all, p.astype(MM_DTYPE), preferred_element_type=F32) / den
        o_ref[0, :, h * HEAD_DIM:(h + 1) * HEAD_DIM] = o.T.astype(o_ref.dtype)


def _attn_b_kernel(sink_ref, q_ref, kp_ref, kc_ref, kn_ref, vp_ref, vc_ref, vn_ref, km_ref, vmt_ref, o_ref,
                   *, tq, n_tiles):
    i = pl.program_id(2)
    nb = tq // WIN_BLK
    k_all = jnp.concatenate([km_ref[0], kp_ref[0], kc_ref[0], kn_ref[0]], axis=0)
    vt_all = jnp.concatenate([vmt_ref[0, 0], vp_ref[0, 0]] + [vc_ref[0, j] for j in range(nb)] + [vn_ref[0, 0]],
                             axis=1)
    n_keys = META_ROWS + 2 * WIN_BLK + tq
    r = lax.broadcasted_iota(jnp.int32, (n_keys, tq), 0)
    c = lax.broadcasted_iota(jnp.int32, (n_keys, tq), 1)
    first_real = META_ROWS + jnp.where(i == 0, WIN_BLK, 0)
    end_real = n_keys - jnp.where(i == n_tiles - 1, WIN_BLK, 0)
    rel = r - c - (META_ROWS + WIN_BLK)
    in_win = (rel >= -WINDOW) & (rel <= WINDOW) & (r >= first_real) & (r < end_real)
    mask = (r < N_META) | in_win
    _win_heads(sink_ref, q_ref, k_all, vt_all, mask, o_ref)


def _attn_b_meta_kernel(sink_ref, q_ref, k0_ref, v0_ref, km_ref, vmt_ref, o_ref):
    k_all = jnp.concatenate([km_ref[0], k0_ref[0]], axis=0)
    vt_all = jnp.concatenate([vmt_ref[0, 0], v0_ref[0, 0]], axis=1)
    r = lax.broadcasted_iota(jnp.int32, (META_ROWS + WIN_BLK, META_ROWS), 0)
    c = lax.broadcasted_iota(jnp.int32, (META_ROWS + WIN_BLK, META_ROWS), 1)
    mask = (r < N_META) | ((r >= META_ROWS) & (r - META_ROWS + N_META - c <= WINDOW))
    _win_heads(sink_ref, q_ref, k_all, vt_all, mask, o_ref)


def _attn_b(sink, q, k, vt, km, vmt, *, tq):
    B, S, _ = q.shape
    n_tiles = S // tq
    nb = tq // WIN_BLK
    last_blk = S // WIN_BLK - 1
    kblk = lambda shape, f: pl.BlockSpec(shape, f)
    n_keys = META_ROWS + 2 * WIN_BLK + tq
    blocks = 2 * _nbytes((tq, 512), MM_DTYPE) + 2 * _nbytes((n_keys, HEAD_DIM), MM_DTYPE)
    grid_spec = pltpu.PrefetchScalarGridSpec(
        num_scalar_prefetch=1,
        grid=(B, N_KV, n_tiles),
        in_specs=[
            pl.BlockSpec((1, tq, GROUP * HEAD_DIM), lambda b, g, i, s: (b, i, g)),
            kblk((1, WIN_BLK, HEAD_DIM), lambda b, g, i, s: (b, jnp.maximum(i * nb - 1, 0), g)),
            kblk((1, tq, HEAD_DIM), lambda b, g, i, s: (b, i, g)),
            kblk((1, WIN_BLK, HEAD_DIM), lambda b, g, i, s: (b, jnp.minimum((i + 1) * nb, last_blk), g)),
            kblk((1, 1, HEAD_DIM, WIN_BLK), lambda b, g, i, s: (b, jnp.maximum(i * nb - 1, 0), g, 0)),
            kblk((1, nb, HEAD_DIM, WIN_BLK), lambda b, g, i, s: (b, i, g, 0)),
            kblk((1, 1, HEAD_DIM, WIN_BLK), lambda b, g, i, s: (b, jnp.minimum((i + 1) * nb, last_blk), g, 0)),
            kblk((1, META_ROWS, HEAD_DIM), lambda b, g, i, s: (b, 0, g)),
            kblk((1, 1, HEAD_DIM, META_ROWS), lambda b, g, i, s: (b, 0, g, 0)),
        ],
        out_specs=pl.BlockSpec((1, tq, GROUP * HEAD_DIM), lambda b, g, i, s: (b, i, g)),
    )
    return pl.pallas_call(
        functools.partial(_attn_b_kernel, tq=tq, n_tiles=n_tiles),
        grid_spec=grid_spec,
        out_shape=jax.ShapeDtypeStruct((B, S, N_HEADS * HEAD_DIM), MM_DTYPE),
        compiler_params=pltpu.CompilerParams(
            dimension_semantics=("parallel", "parallel", "parallel"),
            vmem_limit_bytes=_vmem_limit(blocks, 0, 8 * _nbytes((n_keys, tq), F32))),
        name="attn_window",
    )(sink, q, k, k, k, vt, vt, vt, km, vmt)


def _attn_b_meta(sink, qm, k, vt, km, vmt):
    B = qm.shape[0]
    blocks = 2 * _nbytes((META_ROWS, 512), MM_DTYPE) + 4 * _nbytes((META_ROWS, HEAD_DIM), MM_DTYPE)
    grid_spec = pltpu.PrefetchScalarGridSpec(
        num_scalar_prefetch=1,
        grid=(B, N_KV),
        in_specs=[
            pl.BlockSpec((1, META_ROWS, GROUP * HEAD_DIM), lambda b, g, s: (b, 0, g)),
            pl.BlockSpec((1, WIN_BLK, HEAD_DIM), lambda b, g, s: (b, 0, g)),
            pl.BlockSpec((1, 1, HEAD_DIM, WIN_BLK), lambda b, g, s: (b, 0, g, 0)),
            pl.BlockSpec((1, META_ROWS, HEAD_DIM), lambda b, g, s: (b, 0, g)),
            pl.BlockSpec((1, 1, HEAD_DIM, META_ROWS), lambda b, g, s: (b, 0, g, 0)),
        ],
        out_specs=pl.BlockSpec((1, META_ROWS, GROUP * HEAD_DIM), lambda b, g, s: (b, 0, g)),
    )
    return pl.pallas_call(
        _attn_b_meta_kernel,
        grid_spec=grid_spec,
        out_shape=jax.ShapeDtypeStruct((B, META_ROWS, N_HEADS * HEAD_DIM), MM_DTYPE),
        compiler_params=pltpu.CompilerParams(
            dimension_semantics=("parallel", "parallel"),
            vmem_limit_bytes=_vmem_limit(blocks, 0, 8 * _nbytes((2 * META_ROWS, META_ROWS), F32))),
        name="attn_window_meta",
    )(sink, qm, k, vt, km, vmt)


def _pool_rows(ext_ref, centre_ref, d_ref, pos, seq_len, tm):
    for gi, win in enumerate(POOL_WINDOWS):
        c0 = gi * POOL_DIM
        start = POOL_HALO - win // 2
        acc = ext_ref[start:start + tm, c0:c0 + POOL_DIM]
        for j in range(1, win):
            acc = acc + ext_ref[start + j:start + j + tm, c0:c0 + POOL_DIM]
        lo = jnp.maximum(pos - win // 2, 0)
        hi = jnp.minimum(pos - win // 2 + win, seq_len)
        mean = acc / (hi - lo).astype(F32)
        d_ref[0, :, c0:c0 + POOL_DIM] = (mean - centre_ref[0, :, c0:c0 + POOL_DIM]).astype(d_ref.dtype)


def _pool_kernel(prev_ref, main_ref, next_ref, meta_ref, d_ref, ext_ref, *, tm, n_tiles, seq_len):
    i = pl.program_id(1)
    ext_ref[0:POOL_HALO, :] = jnp.where(i == 0, meta_ref[0], prev_ref[0])
    ext_ref[POOL_HALO:POOL_HALO + tm, :] = main_ref[0]
    ext_ref[POOL_HALO + tm:2 * POOL_HALO + tm, :] = jnp.where(i == n_tiles - 1, 0.0, next_ref[0])
    pos = N_META + i * tm + lax.broadcasted_iota(jnp.int32, (tm, 1), 0)
    _pool_rows(ext_ref, main_ref, d_ref, pos, seq_len, tm)


def _pool_meta_kernel(meta_ref, first_ref, d_ref, ext_ref, *, seq_len):
    ext_ref[...] = jnp.zeros(ext_ref.shape, F32)
    ext_ref[POOL_HALO:POOL_HALO + N_META, :] = meta_ref[0, 0:N_META, :]
    ext_ref[POOL_HALO + N_META:2 * POOL_HALO + N_META, :] = first_ref[0]
    pos = lax.broadcasted_iota(jnp.int32, (META_ROWS, 1), 0)
    _pool_rows(ext_ref, meta_ref, d_ref, pos, seq_len, META_ROWS)


def _pool(uc, ucm, *, tm):
    B, S, C = uc.shape
    n_tiles = S // tm
    per = tm // POOL_HALO
    last = S // POOL_HALO - 1
    halo = lambda f: pl.BlockSpec((1, POOL_HALO, C), f)
    blocks = 2 * _nbytes((tm, C), F32) + _nbytes((tm, C), MM_DTYPE)
    return pl.pallas_call(
        functools.partial(_pool_kernel, tm=tm, n_tiles=n_tiles, seq_len=N_META + S),
        grid=(B, n_tiles),
        in_specs=[
            halo(lambda b, i: (b, jnp.maximum(i * per - 1, 0), 0)),
            pl.BlockSpec((1, tm, C), lambda b, i: (b, i, 0)),
            halo(lambda b, i: (b, jnp.minimum((i + 1) * per, last), 0)),
            halo(lambda b, i: (b, N_META // POOL_HALO - 1, 0)),
        ],
        out_specs=pl.BlockSpec((1, tm, C), lambda b, i: (b, i, 0)),
        out_shape=jax.ShapeDtypeStruct((B, S, C), MM_DTYPE),
        scratch_shapes=[pltpu.VMEM((tm + 2 * POOL_HALO, C), F32)],
        compiler_params=pltpu.CompilerParams(
            dimension_semantics=("parallel", "parallel"),
            vmem_limit_bytes=_vmem_limit(blocks, _nbytes((tm + 16, C), F32), 4 * _nbytes((tm, POOL_DIM), F32))),
        name="pool",
    )(uc, uc, uc, ucm)


def _pool_meta(ucm, uc, *, seq_len):
    B, _, C = ucm.shape
    return pl.pallas_call(
        functools.partial(_pool_meta_kernel, seq_len=seq_len),
        grid=(B,),
        in_specs=[
            pl.BlockSpec((1, META_ROWS, C), lambda b: (b, 0, 0)),
            pl.BlockSpec((1, POOL_HALO, C), lambda b: (b, 0, 0)),
        ],
        out_specs=pl.BlockSpec((1, META_ROWS, C), lambda b: (b, 0, 0)),
        out_shape=jax.ShapeDtypeStruct((B, META_ROWS, C), MM_DTYPE),
        scratch_shapes=[pltpu.VMEM((META_ROWS + 2 * POOL_HALO, C), F32)],
        compiler_params=pltpu.CompilerParams(
            dimension_semantics=("parallel",),
            vmem_limit_bytes=_vmem_limit(3 * _nbytes((META_ROWS, C), F32))),
        name="pool_meta",
    )(ucm, uc)


def _merge_kernel(h_ref, oa_ref, ob_ref, d_ref, wg_ref, wa_ref, wb_ref, pw_ref, ps_ref, wo_ref, g_ref, b_ref,
                  out_ref):
    h = h_ref[0]
    hb = h.astype(MM_DTYPE)
    ya = jnp.dot(oa_ref[0], wa_ref[...], preferred_element_type=F32)
    yb = jnp.dot(ob_ref[0], wb_ref[...], preferred_element_type=F32)
    yc = jnp.concatenate(
        [jnp.dot(d_ref[0, :, gi * POOL_DIM:(gi + 1) * POOL_DIM], pw_ref[gi], preferred_element_type=F32)
         for gi in range(len(POOL_WINDOWS))], axis=1) * ps_ref[...]

    def gate(j):
        return jax.nn.sigmoid(jnp.dot(hb, wg_ref[:, j * D_MODEL:(j + 1) * D_MODEL], preferred_element_type=F32))

    merged = gate(0) * ya + gate(1) * yb + gate(2) * yc
    y = jnp.dot(merged.astype(MM_DTYPE), wo_ref[...], preferred_element_type=F32)
    out_ref[0] = _layer_norm(ALPHA * h + y, g_ref[...], b_ref[...])


def _merge(h, oa, ob, d, wg, wa, wb, pw, ps, wo, g, b, *, tm):
    B, T, D = h.shape
    row = lambda bb, i: (bb, i, 0)
    vec = pl.BlockSpec((1, D), lambda bb, i: (0, 0))
    wbytes = sum(_nbytes(w.shape, MM_DTYPE) for w in (wg, wa, wb, pw, wo))
    blocks = 2 * _nbytes((tm, D), F32) + 3 * _nbytes((tm, D), MM_DTYPE)
    return pl.pallas_call(
        _merge_kernel,
        grid=(B, T // tm),
        in_specs=[pl.BlockSpec((1, tm, D), row)] * 4
        + [_resident(wg.shape), _resident(wa.shape), _resident(wb.shape), _resident(pw.shape), vec,
           _resident(wo.shape), vec, vec],
        out_specs=pl.BlockSpec((1, tm, D), row),
        out_shape=jax.ShapeDtypeStruct((B, T, D), F32),
        compiler_params=pltpu.CompilerParams(
            dimension_semantics=("parallel", "parallel"),
            vmem_limit_bytes=_vmem_limit(blocks, wbytes, 10 * _nbytes((tm, D), F32))),
        name="merge",
    )(h, oa, ob, d, wg, wa, wb, pw, ps, wo, g, b)


def _ffn_kernel(h_ref, wu_ref, wd_ref, g_ref, b_ref, out_ref, act_ref):
    h = h_ref[0]
    hb = h.astype(MM_DTYPE)
    for c in range(D_FF // FF_CHUNK):
        c0 = c * FF_CHUNK
        gate = jnp.dot(hb, wu_ref[:, c0:c0 + FF_CHUNK], preferred_element_type=F32)
        up = jnp.dot(hb, wu_ref[:, D_FF + c0:D_FF + c0 + FF_CHUNK], preferred_element_type=F32)
        act_ref[:, c0:c0 + FF_CHUNK] = (jax.nn.silu(gate) * up).astype(act_ref.dtype)
    f = jnp.dot(act_ref[...], wd_ref[...], preferred_element_type=F32)
    out_ref[0] = _layer_norm(ALPHA * h + f, g_ref[...], b_ref[...])


def _ffn(h, wu, wd, g, b, *, tm):
    B, T, D = h.shape
    row = lambda bb, i: (bb, i, 0)
    vec = pl.BlockSpec((1, D), lambda bb, i: (0, 0))
    wbytes = _nbytes(wu.shape, MM_DTYPE) + _nbytes(wd.shape, MM_DTYPE)
    return pl.pallas_call(
        _ffn_kernel,
        grid=(B, T // tm),
        in_specs=[pl.BlockSpec((1, tm, D), row), _resident(wu.shape), _resident(wd.shape), vec, vec],
        out_specs=pl.BlockSpec((1, tm, D), row),
        out_shape=jax.ShapeDtypeStruct((B, T, D), F32),
        scratch_shapes=[pltpu.VMEM((tm, D_FF), MM_DTYPE)],
        compiler_params=pltpu.CompilerParams(
            dimension_semantics=("parallel", "parallel"),
            vmem_limit_bytes=_vmem_limit(2 * _nbytes((tm, D), F32), wbytes + _nbytes((tm, D_FF), MM_DTYPE),
                                         6 * _nbytes((tm, D), F32))),
        name="ffn",
    )(h, wu, wd, g, b)


def _rope_tables(row, col, pos):
    def cs(p, dim):
        inv = ROPE_THETA ** (-jnp.arange(0, dim, 2, dtype=F32) / dim)
        ang = p.astype(F32)[:, None] * inv[None, :]
        return jnp.cos(ang), jnp.sin(ang)

    cr, sr = cs(row, HEAD_DIM // 2)
    cc, sc = cs(col, HEAD_DIM // 2)
    cp, sp = cs(pos, HEAD_DIM)
    ca = jnp.concatenate([cr, cr, cc, cc], axis=1)
    sa = jnp.concatenate([-sr, sr, -sc, sc], axis=1)
    cb = jnp.concatenate([cp, cp], axis=1)
    sb = jnp.concatenate([-sp, sp], axis=1)
    return ca, sa, cb, sb


def _real_tables(S):
    t = jnp.arange(S, dtype=jnp.int32)
    return _rope_tables(t // GRID_W, t % GRID_W, t + N_META)


def _meta_tables():
    i = jnp.arange(META_ROWS, dtype=jnp.int32)
    return _rope_tables(-jnp.ones((META_ROWS,), jnp.int32), i, i)


def _layer(xr, xm, p, tabs_r, tabs_m):
    B, S, _ = xr.shape
    qa, ka, va, qb, kb, vb, uc = _proj(xr, p["w_qkv"], p["qg"], p["kg"], tabs_r,
                                       tm=ROW_TILE, cka=ATTN_TK, ckb=WIN_BLK)
    qam, kam, vam, qbm, kbm, vbm, ucm = _proj(xm, p["w_qkv"], p["qg"], p["kg"], tabs_m,
                                              tm=META_ROWS, cka=META_ROWS, ckb=META_ROWS)
    oa = _attn_a(qa, ka, va, kam, vam, tq=ATTN_TQ, tk=ATTN_TK)
    oam = _attn_a(qam, ka, va, kam, vam, tq=META_ROWS, tk=ATTN_TK)
    ob = _attn_b(p["sink"], qb, kb, vb, kbm, vbm, tq=WIN_TQ)
    obm = _attn_b_meta(p["sink"], qbm, kb, vb, kbm, vbm)
    d = _pool(uc, ucm, tm=ROW_TILE)
    dm = _pool_meta(ucm, uc, seq_len=N_META + S)
    merge_w = (p["wg"], p["wa"], p["wb"], p["pw"], p["ps"], p["wo"], p["ln1_g"], p["ln1_b"])
    hr = _merge(xr, oa, ob, d, *merge_w, tm=ROW_TILE)
    hm = _merge(xm, oam, obm, dm, *merge_w, tm=META_ROWS)
    ffn_w = (p["wu"], p["wd"], p["ln2_g"], p["ln2_b"])
    return _ffn(hr, *ffn_w, tm=ROW_TILE), _ffn(hm, *ffn_w, tm=META_ROWS)


def _trunk(x, meta_tokens, layers, tabs_r, tabs_m):
    B, S, D = x.shape
    xm = jnp.zeros((B, META_ROWS, D), x.dtype).at[:, :N_META].set(meta_tokens.astype(x.dtype)[None])
    xr = x
    for p in layers:
        xr, xm = _layer(xr, xm, p, tabs_r, tabs_m)
    return xr


def kernel(x_prompt, x_sample, meta_tokens, w_in, q_norm_g, k_norm_g, sink_logit, pool_w, pool_scale,
           w_branch_a, w_branch_b, w_out, ln1_g, ln1_b, w_up, w_down, ln2_g, ln2_b):
    mm = lambda w: w.astype(MM_DTYPE)
    vec = lambda v: v.astype(F32).reshape(1, -1)
    layers = []
    for l in range(w_in.shape[0]):
        layers.append(dict(
            w_qkv=mm(w_in[l, :, :QKV_WIDTH]), wg=mm(w_in[l, :, QKV_WIDTH:]),
            qg=vec(q_norm_g[l]), kg=vec(k_norm_g[l]), sink=sink_logit[l].astype(F32),
            pw=mm(pool_w[l]), ps=vec(pool_scale[l]),
            wa=mm(w_branch_a[l]), wb=mm(w_branch_b[l]), wo=mm(w_out[l]),
            ln1_g=vec(ln1_g[l]), ln1_b=vec(ln1_b[l]),
            wu=mm(w_up[l]), wd=mm(w_down[l]),
            ln2_g=vec(ln2_g[l]), ln2_b=vec(ln2_b[l]),
        ))
    tabs_m = _meta_tables()
    tabs_s = _real_tables(x_sample.shape[1])
    tabs_p = tuple(t[:x_prompt.shape[1]] for t in tabs_s)
    y_prompt = _trunk(x_prompt, meta_tokens, layers, tabs_p, tabs_m)
    y_sample = _trunk(x_sample, meta_tokens, layers, tabs_s, tabs_m)
    return (y_prompt, y_sample)
```

```python
import functools

import jax
import jax.numpy as jnp
from jax import lax
from jax.experimental import pallas as pl
from jax.experimental.pallas import tpu as pltpu

D_MODEL = 1024
HEAD_DIM = 128
N_HEADS = 8
N_KV = 2
GROUP = N_HEADS // N_KV
WINDOW = 128
N_META = 16
META_ROWS = 128
GRID_W = 64
ROPE_THETA = 10000.0
POOL_WINDOWS = (2, 4, 8, 16)
POOL_DIM = D_MODEL // len(POOL_WINDOWS)
POOL_HALO = 8
D_FF = 2816
DEPTH = 2
ALPHA = (2 * DEPTH) ** 0.25
NEG = -1e30
QKV_WIDTH = 4096
LOG2E = 1.4426950408889634
QK_SCALE = HEAD_DIM ** -0.5 * LOG2E

MM_DTYPE = jnp.bfloat16
F32 = jnp.float32

V7X_VMEM_BYTES = 64 * 1024 * 1024
VMEM_CAP = V7X_VMEM_BYTES - 6 * 1024 * 1024

ROW_TILE = 512
ATTN_TQ = 512
ATTN_TK = 512
WIN_TQ = 256
WIN_BLK = 128
FF_CHUNK = 256

_NT = (((1,), (1,)), ((), ()))


def _vmem_limit(block_bytes, scratch_bytes=0, temp_bytes=0):
    est = 2 * block_bytes + scratch_bytes + temp_bytes + (4 << 20)
    return int(min(max(est, 16 << 20), VMEM_CAP))


def _nbytes(shape, dtype):
    n = 1
    for s in shape:
        n *= s
    return n * jnp.dtype(dtype).itemsize


def _resident(shape):
    zeros = (0,) * len(shape)
    return pl.BlockSpec(shape, lambda *_: zeros, pipeline_mode=pl.Buffered(1))


def _layer_norm(x, g, b):
    mu = jnp.mean(x, axis=-1, keepdims=True)
    xc = x - mu
    var = jnp.mean(xc * xc, axis=-1, keepdims=True)
    return xc * lax.rsqrt(var + 1e-5) * g + b


def _proj_kernel(x_ref, w_ref, qg_ref, kg_ref, ca_ref, sa_ref, cb_ref, sb_ref,
                 qa_ref, ka_ref, va_ref, qb_ref, kb_ref, vb_ref, uc_ref, *, tm, cka, ckb):
    xb = x_ref[0].astype(MM_DTYPE)
    ca = ca_ref[...]
    sa = sa_ref[...]
    cb = cb_ref[...]
    sb = sb_ref[...]
    qg = qg_ref[...]
    kg = kg_ref[...]
    lane = lax.broadcasted_iota(jnp.int32, (tm, HEAD_DIM), 1)
    low_quarter = (lane % 64) < 32

    def rms(x, g):
        return x * lax.rsqrt(jnp.mean(x * x, axis=-1, keepdims=True) + 1e-6) * g

    def rope_a(x):
        partner = jnp.where(low_quarter, pltpu.roll(x, 96, 1), pltpu.roll(x, 32, 1))
        return x * ca + partner * sa

    def rope_b(x):
        return x * cb + pltpu.roll(x, 64, 1) * sb

    def cols(c0, n):
        return jnp.dot(xb, w_ref[:, c0:c0 + n], preferred_element_type=F32)

    def head(u, j):
        return u[:, j * HEAD_DIM:(j + 1) * HEAD_DIM]

    def store_vt(v_ref, v, ck):
        for j in range(tm // ck):
            v_ref[0, j] = v[j * ck:(j + 1) * ck, :].T.astype(v_ref.dtype)

    for blk in range(2):
        u = cols(blk * 512, 512)
        for j in range(4):
            h = blk * 4 + j
            qa_ref[0, :, h * HEAD_DIM:(h + 1) * HEAD_DIM] = rope_a(rms(head(u, j), qg)).astype(qa_ref.dtype)
    u = cols(1024, 512)
    for j in range(2):
        ka_ref[0, :, j * HEAD_DIM:(j + 1) * HEAD_DIM] = rope_a(rms(head(u, j), kg)).astype(ka_ref.dtype)
    store_vt(va_ref, u[:, 256:512], cka)
    for blk in range(2):
        u = cols(1536 + blk * 512, 512)
        for j in range(4):
            h = blk * 4 + j
            qb_ref[0, :, h * HEAD_DIM:(h + 1) * HEAD_DIM] = (rope_b(head(u, j)) * QK_SCALE).astype(qb_ref.dtype)
    u = cols(2560, 512)
    for j in range(2):
        kb_ref[0, :, j * HEAD_DIM:(j + 1) * HEAD_DIM] = rope_b(head(u, j)).astype(kb_ref.dtype)
    store_vt(vb_ref, u[:, 256:512], ckb)
    for blk in range(2):
        uc_ref[0, :, blk * 512:(blk + 1) * 512] = cols(3072 + blk * 512, 512)


def _proj(x, w_qkv, qg, kg, tabs, *, tm, cka, ckb):
    B, T, D = x.shape
    nt = T // tm
    ca, sa, cb, sb = tabs
    row = lambda b, i: (b, i, 0)
    tab = pl.BlockSpec((tm, HEAD_DIM), lambda b, i: (i, 0))
    small = pl.BlockSpec((1, HEAD_DIM), lambda b, i: (0, 0))
    out_shape = (
        jax.ShapeDtypeStruct((B, T, 1024), MM_DTYPE),
        jax.ShapeDtypeStruct((B, T, 256), MM_DTYPE),
        jax.ShapeDtypeStruct((B, T // cka, 256, cka), MM_DTYPE),
        jax.ShapeDtypeStruct((B, T, 1024), MM_DTYPE),
        jax.ShapeDtypeStruct((B, T, 256), MM_DTYPE),
        jax.ShapeDtypeStruct((B, T // ckb, 256, ckb), MM_DTYPE),
        jax.ShapeDtypeStruct((B, T, 1024), F32),
    )
    out_specs = (
        pl.BlockSpec((1, tm, 1024), row),
        pl.BlockSpec((1, tm, 256), row),
        pl.BlockSpec((1, tm // cka, 256, cka), lambda b, i: (b, i, 0, 0)),
        pl.BlockSpec((1, tm, 1024), row),
        pl.BlockSpec((1, tm, 256), row),
        pl.BlockSpec((1, tm // ckb, 256, ckb), lambda b, i: (b, i, 0, 0)),
        pl.BlockSpec((1, tm, 1024), row),
    )
    blocks = (_nbytes((tm, D), F32) + 4 * _nbytes((tm, HEAD_DIM), F32)
              + _nbytes((tm, 3072), MM_DTYPE) + _nbytes((tm, 1024), F32))
    return pl.pallas_call(
        functools.partial(_proj_kernel, tm=tm, cka=cka, ckb=ckb),
        grid=(B, nt),
        in_specs=[pl.BlockSpec((1, tm, D), row), _resident(w_qkv.shape), small, small, tab, tab, tab, tab],
        out_specs=out_specs,
        out_shape=out_shape,
        compiler_params=pltpu.CompilerParams(
            dimension_semantics=("parallel", "parallel"),
            vmem_limit_bytes=_vmem_limit(blocks, _nbytes(w_qkv.shape, MM_DTYPE), 8 * _nbytes((tm, 512), F32))),
        name="proj",
    )(x, w_qkv, qg, kg, ca, sa, cb, sb)


def _attn_a_kernel(q_ref, k_ref, vt_ref, km_ref, vmt_ref, o_ref,
                   acc_ref, m_ref, l_ref, s_ref, p_ref, ms_ref, as_ref, ac_ref, *, tq, tk, n_chunks):
    def q_head(h):
        return q_ref[0, :, h * HEAD_DIM:(h + 1) * HEAD_DIM]

    def scores(c, slot, h):
        kc = k_ref[0, pl.ds(pl.multiple_of(c * tk, tk), tk), :]
        s = lax.dot_general(kc, q_head(h), _NT, preferred_element_type=F32)
        s_ref[slot, h] = s
        m_old = m_ref[h]
        m_new = jnp.maximum(m_old, jnp.max(s, axis=0, keepdims=True))
        m_ref[h] = m_new
        ms_ref[slot, h] = m_new
        as_ref[slot, h] = jnp.exp2(m_old - m_new)

    def weights(slot, h):
        a = as_ref[slot, h]
        p = jnp.exp2(s_ref[slot, h] - ms_ref[slot, h])
        l_ref[h] = a * l_ref[h] + jnp.sum(p, axis=0, keepdims=True)
        p_ref[slot, h] = p.astype(p_ref.dtype)
        ac_ref[slot, h] = a

    def values(c, slot, h):
        pv = jnp.dot(vt_ref[0, c], p_ref[slot, h], preferred_element_type=F32)
        acc_ref[h] = ac_ref[slot, h] * acc_ref[h] + pv

    key_row = lax.broadcasted_iota(jnp.int32, (META_ROWS, tq), 0)
    for h in range(GROUP):
        s = lax.dot_general(km_ref[0], q_head(h), _NT, preferred_element_type=F32)
        s = jnp.where(key_row < N_META, s, NEG)
        m = jnp.max(s, axis=0, keepdims=True)
        p = jnp.exp2(s - m)
        m_ref[h] = m
        l_ref[h] = jnp.sum(p, axis=0, keepdims=True)
        acc_ref[h] = jnp.dot(vmt_ref[0, 0], p.astype(MM_DTYPE), preferred_element_type=F32)
        p_ref[1, h] = jnp.zeros((tk, tq), p_ref.dtype)
        ac_ref[1, h] = jnp.ones((1, tq), F32)
    for h in range(GROUP):
        scores(0, 0, h)

    def body(j, carry):
        c0 = 2 * j
        for h in range(GROUP):
            scores(c0 + 1, 1, h)
            values(jnp.maximum(c0 - 1, 0), 1, h)
            weights(0, h)
        for h in range(GROUP):
            scores(jnp.minimum(c0 + 2, n_chunks - 1), 0, h)
            values(c0, 0, h)
            weights(1, h)
        return carry

    lax.fori_loop(0, n_chunks // 2, body, 0)
    for h in range(GROUP):
        values(n_chunks - 1, 1, h)
        o = acc_ref[h] / l_ref[h]
        o_ref[0, :, h * HEAD_DIM:(h + 1) * HEAD_DIM] = o.T.astype(o_ref.dtype)


def _attn_a(q, k, vt, km, vmt, *, tq, tk):
    B, Tq, _ = q.shape
    S = k.shape[1]
    n_chunks = S // tk
    assert n_chunks % 2 == 0
    blocks = (_nbytes((tq, 512), MM_DTYPE) * 2 + 2 * _nbytes((S, HEAD_DIM), MM_DTYPE)
              + 2 * _nbytes((META_ROWS, HEAD_DIM), MM_DTYPE))
    stat = pltpu.VMEM((2, GROUP, 1, tq), F32)
    scratch = (GROUP * (_nbytes((HEAD_DIM, tq), F32) + 8 * _nbytes((8, tq), F32))
               + 2 * GROUP * (_nbytes((tk, tq), F32) + _nbytes((tk, tq), MM_DTYPE)))
    return pl.pallas_call(
        functools.partial(_attn_a_kernel, tq=tq, tk=tk, n_chunks=n_chunks),
        grid=(B, N_KV, Tq // tq),
        in_specs=[
            pl.BlockSpec((1, tq, GROUP * HEAD_DIM), lambda b, g, i: (b, i, g)),
            pl.BlockSpec((1, S, HEAD_DIM), lambda b, g, i: (b, 0, g)),
            pl.BlockSpec((1, n_chunks, HEAD_DIM, tk), lambda b, g, i: (b, 0, g, 0)),
            pl.BlockSpec((1, META_ROWS, HEAD_DIM), lambda b, g, i: (b, 0, g)),
            pl.BlockSpec((1, 1, HEAD_DIM, META_ROWS), lambda b, g, i: (b, 0, g, 0)),
        ],
        out_specs=pl.BlockSpec((1, tq, GROUP * HEAD_DIM), lambda b, g, i: (b, i, g)),
        out_shape=jax.ShapeDtypeStruct((B, Tq, N_HEADS * HEAD_DIM), MM_DTYPE),
        scratch_shapes=[
            pltpu.VMEM((GROUP, HEAD_DIM, tq), F32),
            pltpu.VMEM((GROUP, 1, tq), F32),
            pltpu.VMEM((GROUP, 1, tq), F32),
            pltpu.VMEM((2, GROUP, tk, tq), F32),
            pltpu.VMEM((2, GROUP, tk, tq), MM_DTYPE),
            stat, stat, stat,
        ],
        compiler_params=pltpu.CompilerParams(
            dimension_semantics=("parallel", "parallel", "parallel"),
            vmem_limit_bytes=_vmem_limit(blocks, scratch, 6 * _nbytes((tk, tq), F32))),
        name="attn_global",
    )(q, k, vt, km, vmt)


def _win_heads(sink_ref, q_ref, k_all, vt_all, mask, o_ref):
    g = pl.program_id(1)
    scores = [lax.dot_general(k_all, q_ref[0, :, h * HEAD_DIM:(h + 1) * HEAD_DIM], _NT,
                              preferred_element_type=F32) for h in range(GROUP)]
    for h in range(GROUP):
        s = jnp.where(mask, scores[h], NEG)
        sink = sink_ref[g * GROUP + h] * LOG2E
        m = jnp.maximum(jnp.max(s, axis=0, keepdims=True), sink)
        p = jnp.exp2(s - m)
        den = jnp.sum(p, axis=0, keepdims=True) + jnp.exp2(sink - m)
        o = jnp.dot(vt_all, p.astype(MM_DTYPE), preferred_element_type=F32) / den
        o_ref[0, :, h * HEAD_DIM:(h + 1) * HEAD_DIM] = o.T.astype(o_ref.dtype)


def _attn_b_kernel(sink_ref, q_ref, kp_ref, kc_ref, kn_ref, vp_ref, vc_ref, vn_ref, km_ref, vmt_ref, o_ref,
                   *, tq, n_tiles):
    i = pl.program_id(2)
    nb = tq // WIN_BLK
    k_all = jnp.concatenate([km_ref[0], kp_ref[0], kc_ref[0], kn_ref[0]], axis=0)
    vt_all = jnp.concatenate([vmt_ref[0, 0], vp_ref[0, 0]] + [vc_ref[0, j] for j in range(nb)] + [vn_ref[0, 0]],
                             axis=1)
    n_keys = META_ROWS + 2 * WIN_BLK + tq
    r = lax.broadcasted_iota(jnp.int32, (n_keys, tq), 0)
    c = lax.broadcasted_iota(jnp.int32, (n_keys, tq), 1)
    first_real = META_ROWS + jnp.where(i == 0, WIN_BLK, 0)
    end_real = n_keys - jnp.where(i == n_tiles - 1, WIN_BLK, 0)
    rel = r - c - (META_ROWS + WIN_BLK)
    in_win = (rel >= -WINDOW) & (rel <= WINDOW) & (r >= first_real) & (r < end_real)
    mask = (r < N_META) | in_win
    _win_heads(sink_ref, q_ref, k_all, vt_all, mask, o_ref)


def _attn_b_meta_kernel(sink_ref, q_ref, k0_ref, v0_ref, km_ref, vmt_ref, o_ref):
    k_all = jnp.concatenate([km_ref[0], k0_ref[0]], axis=0)
    vt_all = jnp.concatenate([vmt_ref[0, 0], v0_ref[0, 0]], axis=1)
    r = lax.broadcasted_iota(jnp.int32, (META_ROWS + WIN_BLK, META_ROWS), 0)
    c = lax.broadcasted_iota(jnp.int32, (META_ROWS + WIN_BLK, META_ROWS), 1)
    mask = (r < N_META) | ((r >= META_ROWS) & (r - META_ROWS + N_META - c <= WINDOW))
    _win_heads(sink_ref, q_ref, k_all, vt_all, mask, o_ref)


def _attn_b(sink, q, k, vt, km, vmt, *, tq):
    B, S, _ = q.shape
    n_tiles = S // tq
    nb = tq // WIN_BLK
    last_blk = S // WIN_BLK - 1
    kblk = lambda shape, f: pl.BlockSpec(shape, f)
    n_keys = META_ROWS + 2 * WIN_BLK + tq
    blocks = 2 * _nbytes((tq, 512), MM_DTYPE) + 2 * _nbytes((n_keys, HEAD_DIM), MM_DTYPE)
    grid_spec = pltpu.PrefetchScalarGridSpec(
        num_scalar_prefetch=1,
        grid=(B, N_KV, n_tiles),
        in_specs=[
            pl.BlockSpec((1, tq, GROUP * HEAD_DIM), lambda b, g, i, s: (b, i, g)),
            kblk((1, WIN_BLK, HEAD_DIM), lambda b, g, i, s: (b, jnp.maximum(i * nb - 1, 0), g)),
            kblk((1, tq, HEAD_DIM), lambda b, g, i, s: (b, i, g)),
            kblk((1, WIN_BLK, HEAD_DIM), lambda b, g, i, s: (b, jnp.minimum((i + 1) * nb, last_blk), g)),
            kblk((1, 1, HEAD_DIM, WIN_BLK), lambda b, g, i, s: (b, jnp.maximum(i * nb - 1, 0), g, 0)),
            kblk((1, nb, HEAD_DIM, WIN_BLK), lambda b, g, i, s: (b, i, g, 0)),
            kblk((1, 1, HEAD_DIM, WIN_BLK), lambda b, g, i, s: (b, jnp.minimum((i + 1) * nb, last_blk), g, 0)),
            kblk((1, META_ROWS, HEAD_DIM), lambda b, g, i, s: (b, 0, g)),
            kblk((1, 1, HEAD_DIM, META_ROWS), lambda b, g, i, s: (b, 0, g, 0)),
        ],
        out_specs=pl.BlockSpec((1, tq, GROUP * HEAD_DIM), lambda b, g, i, s: (b, i, g)),
    )
    return pl.pallas_call(
        functools.partial(_attn_b_kernel, tq=tq, n_tiles=n_tiles),
        grid_spec=grid_spec,
        out_shape=jax.ShapeDtypeStruct((B, S, N_HEADS * HEAD_DIM), MM_DTYPE),
        compiler_params=pltpu.CompilerParams(
            dimension_semantics=("parallel", "parallel", "parallel"),
            vmem_limit_bytes=_vmem_limit(blocks, 0, 8 * _nbytes((n_keys, tq), F32))),
        name="attn_window",
    )(sink, q, k, k, k, vt, vt, vt, km, vmt)


def _attn_b_meta(sink, qm, k, vt, km, vmt):
    B = qm.shape[0]
    blocks = 2 * _nbytes((META_ROWS, 512), MM_DTYPE) + 4 * _nbytes((META_ROWS, HEAD_DIM), MM_DTYPE)
    grid_spec = pltpu.PrefetchScalarGridSpec(
        num_scalar_prefetch=1,
        grid=(B, N_KV),
        in_specs=[
            pl.BlockSpec((1, META_ROWS, GROUP * HEAD_DIM), lambda b, g, s: (b, 0, g)),
            pl.BlockSpec((1, WIN_BLK, HEAD_DIM), lambda b, g, s: (b, 0, g)),
            pl.BlockSpec((1, 1, HEAD_DIM, WIN_BLK), lambda b, g, s: (b, 0, g, 0)),
            pl.BlockSpec((1, META_ROWS, HEAD_DIM), lambda b, g, s: (b, 0, g)),
            pl.BlockSpec((1, 1, HEAD_DIM, META_ROWS), lambda b, g, s: (b, 0, g, 0)),
        ],
        out_specs=pl.BlockSpec((1, META_ROWS, GROUP * HEAD_DIM), lambda b, g, s: (b, 0, g)),
    )
    return pl.pallas_call(
        _attn_b_meta_kernel,
        grid_spec=grid_spec,
        out_shape=jax.ShapeDtypeStruct((B, META_ROWS, N_HEADS * HEAD_DIM), MM_DTYPE),
        compiler_params=pltpu.CompilerParams(
            dimension_semantics=("parallel", "parallel"),
            vmem_limit_bytes=_vmem_limit(blocks, 0, 8 * _nbytes((2 * META_ROWS, META_ROWS), F32))),
        name="attn_window_meta",
    )(sink, qm, k, vt, km, vmt)


def _pool_rows(ext_ref, centre_ref, d_ref, pos, seq_len, tm):
    for gi, win in enumerate(POOL_WINDOWS):
        c0 = gi * POOL_DIM
        start = POOL_HALO - win // 2
        acc = ext_ref[start:start + tm, c0:c0 + POOL_DIM]
        for j in range(1, win):
            acc = acc + ext_ref[start + j:start + j + tm, c0:c0 + POOL_DIM]
        lo = jnp.maximum(pos - win // 2, 0)
        hi = jnp.minimum(pos - win // 2 + win, seq_len)
        mean = acc / (hi - lo).astype(F32)
        d_ref[0, :, c0:c0 + POOL_DIM] = (mean - centre_ref[0, :, c0:c0 + POOL_DIM]).astype(d_ref.dtype)


def _pool_kernel(prev_ref, main_ref, next_ref, meta_ref, d_ref, ext_ref, *, tm, n_tiles, seq_len):
    i = pl.program_id(1)
    ext_ref[0:POOL_HALO, :] = jnp.where(i == 0, meta_ref[0], prev_ref[0])
    ext_ref[POOL_HALO:POOL_HALO + tm, :] = main_ref[0]
    ext_ref[POOL_HALO + tm:2 * POOL_HALO + tm, :] = jnp.where(i == n_tiles - 1, 0.0, next_ref[0])
    pos = N_META + i * tm + lax.broadcasted_iota(jnp.int32, (tm, 1), 0)
    _pool_rows(ext_ref, main_ref, d_ref, pos, seq_len, tm)


def _pool_meta_kernel(meta_ref, first_ref, d_ref, ext_ref, *, seq_len):
    ext_ref[...] = jnp.zeros(ext_ref.shape, F32)
    ext_ref[POOL_HALO:POOL_HALO + N_META, :] = meta_ref[0, 0:N_META, :]
    ext_ref[POOL_HALO + N_META:2 * POOL_HALO + N_META, :] = first_ref[0]
    pos = lax.broadcasted_iota(jnp.int32, (META_ROWS, 1), 0)
    _pool_rows(ext_ref, meta_ref, d_ref, pos, seq_len, META_ROWS)


def _pool(uc, ucm, *, tm):
    B, S, C = uc.shape
    n_tiles = S // tm
    per = tm // POOL_HALO
    last = S // POOL_HALO - 1
    halo = lambda f: pl.BlockSpec((1, POOL_HALO, C), f)
    blocks = 2 * _nbytes((tm, C), F32) + _nbytes((tm, C), MM_DTYPE)
    return pl.pallas_call(
        functools.partial(_pool_kernel, tm=tm, n_tiles=n_tiles, seq_len=N_META + S),
        grid=(B, n_tiles),
        in_specs=[
            halo(lambda b, i: (b, jnp.maximum(i * per - 1, 0), 0)),
            pl.BlockSpec((1, tm, C), lambda b, i: (b, i, 0)),
            halo(lambda b, i: (b, jnp.minimum((i + 1) * per, last), 0)),
            halo(lambda b, i: (b, N_META // POOL_HALO - 1, 0)),
        ],
        out_specs=pl.BlockSpec((1, tm, C), lambda b, i: (b, i, 0)),
        out_shape=jax.ShapeDtypeStruct((B, S, C), MM_DTYPE),
        scratch_shapes=[pltpu.VMEM((tm + 2 * POOL_HALO, C), F32)],
        compiler_params=pltpu.CompilerParams(
            dimension_semantics=("parallel", "parallel"),
            vmem_limit_bytes=_vmem_limit(blocks, _nbytes((tm + 16, C), F32), 4 * _nbytes((tm, POOL_DIM), F32))),
        name="pool",
    )(uc, uc, uc, ucm)


def _pool_meta(ucm, uc, *, seq_len):
    B, _, C = ucm.shape
    return pl.pallas_call(
        functools.partial(_pool_meta_kernel, seq_len=seq_len),
        grid=(B,),
        in_specs=[
            pl.BlockSpec((1, META_ROWS, C), lambda b: (b, 0, 0)),
            pl.BlockSpec((1, POOL_HALO, C), lambda b: (b, 0, 0)),
        ],
        out_specs=pl.BlockSpec((1, META_ROWS, C), lambda b: (b, 0, 0)),
        out_shape=jax.ShapeDtypeStruct((B, META_ROWS, C), MM_DTYPE),
        scratch_shapes=[pltpu.VMEM((META_ROWS + 2 * POOL_HALO, C), F32)],
        compiler_params=pltpu.CompilerParams(
            dimension_semantics=("parallel",),
            vmem_limit_bytes=_vmem_limit(3 * _nbytes((META_ROWS, C), F32))),
        name="pool_meta",
    )(ucm, uc)


def _merge_kernel(h_ref, oa_ref, ob_ref, d_ref, wg_ref, wa_ref, wb_ref, pw_ref, ps_ref, wo_ref, g_ref, b_ref,
                  out_ref):
    h = h_ref[0]
    hb = h.astype(MM_DTYPE)
    ya = jnp.dot(oa_ref[0], wa_ref[...], preferred_element_type=F32)
    yb = jnp.dot(ob_ref[0], wb_ref[...], preferred_element_type=F32)
    yc = jnp.concatenate(
        [jnp.dot(d_ref[0, :, gi * POOL_DIM:(gi + 1) * POOL_DIM], pw_ref[gi], preferred_element_type=F32)
         for gi in range(len(POOL_WINDOWS))], axis=1) * ps_ref[...]

    def gate(j):
        return jax.nn.sigmoid(jnp.dot(hb, wg_ref[:, j * D_MODEL:(j + 1) * D_MODEL], preferred_element_type=F32))

    merged = gate(0) * ya + gate(1) * yb + gate(2) * yc
    y = jnp.dot(merged.astype(MM_DTYPE), wo_ref[...], preferred_element_type=F32)
    out_ref[0] = _layer_norm(ALPHA * h + y, g_ref[...], b_ref[...])


def _merge(h, oa, ob, d, wg, wa, wb, pw, ps, wo, g, b, *, tm):
    B, T, D = h.shape
    row = lambda bb, i: (bb, i, 0)
    vec = pl.BlockSpec((1, D), lambda bb, i: (0, 0))
    wbytes = sum(_nbytes(w.shape, MM_DTYPE) for w in (wg, wa, wb, pw, wo))
    blocks = 2 * _nbytes((tm, D), F32) + 3 * _nbytes((tm, D), MM_DTYPE)
    return pl.pallas_call(
        _merge_kernel,
        grid=(B, T // tm),
        in_specs=[pl.BlockSpec((1, tm, D), row)] * 4
        + [_resident(wg.shape), _resident(wa.shape), _resident(wb.shape), _resident(pw.shape), vec,
           _resident(wo.shape), vec, vec],
        out_specs=pl.BlockSpec((1, tm, D), row),
        out_shape=jax.ShapeDtypeStruct((B, T, D), F32),
        compiler_params=pltpu.CompilerParams(
            dimension_semantics=("parallel", "parallel"),
            vmem_limit_bytes=_vmem_limit(blocks, wbytes, 10 * _nbytes((tm, D), F32))),
        name="merge",
    )(h, oa, ob, d, wg, wa, wb, pw, ps, wo, g, b)


def _ffn_kernel(h_ref, wu_ref, wd_ref, g_ref, b_ref, out_ref, act_ref):
    h = h_ref[0]
    hb = h.astype(MM_DTYPE)
    for c in range(D_FF // FF_CHUNK):
        c0 = c * FF_CHUNK
        gate = jnp.dot(hb, wu_ref[:, c0:c0 + FF_CHUNK], preferred_element_type=F32)
        up = jnp.dot(hb, wu_ref[:, D_FF + c0:D_FF + c0 + FF_CHUNK], preferred_element_type=F32)
        act_ref[:, c0:c0 + FF_CHUNK] = (jax.nn.silu(gate) * up).astype(act_ref.dtype)
    f = jnp.dot(act_ref[...], wd_ref[...], preferred_element_type=F32)
    out_ref[0] = _layer_norm(ALPHA * h + f, g_ref[...], b_ref[...])


def _ffn(h, wu, wd, g, b, *, tm):
    B, T, D = h.shape
    row = lambda bb, i: (bb, i, 0)
    vec = pl.BlockSpec((1, D), lambda bb, i: (0, 0))
    wbytes = _nbytes(wu.shape, MM_DTYPE) + _nbytes(wd.shape, MM_DTYPE)
    return pl.pallas_call(
        _ffn_kernel,
        grid=(B, T // tm),
        in_specs=[pl.BlockSpec((1, tm, D), row), _resident(wu.shape), _resident(wd.shape), vec, vec],
        out_specs=pl.BlockSpec((1, tm, D), row),
        out_shape=jax.ShapeDtypeStruct((B, T, D), F32),
        scratch_shapes=[pltpu.VMEM((tm, D_FF), MM_DTYPE)],
        compiler_params=pltpu.CompilerParams(
            dimension_semantics=("parallel", "parallel"),
            vmem_limit_bytes=_vmem_limit(2 * _nbytes((tm, D), F32), wbytes + _nbytes((tm, D_FF), MM_DTYPE),
                                         6 * _nbytes((tm, D), F32))),
        name="ffn",
    )(h, wu, wd, g, b)


def _rope_tables(row, col, pos):
    def cs(p, dim):
        inv = ROPE_THETA ** (-jnp.arange(0, dim, 2, dtype=F32) / dim)
        ang = p.astype(F32)[:, None] * inv[None, :]
        return jnp.cos(ang), jnp.sin(ang)

    cr, sr = cs(row, HEAD_DIM // 2)
    cc, sc = cs(col, HEAD_DIM // 2)
    cp, sp = cs(pos, HEAD_DIM)
    ca = jnp.concatenate([cr, cr, cc, cc], axis=1)
    sa = jnp.concatenate([-sr, sr, -sc, sc], axis=1)
    cb = jnp.concatenate([cp, cp], axis=1)
    sb = jnp.concatenate([-sp, sp], axis=1)
    return ca, sa, cb, sb


def _real_tables(S):
    t = jnp.arange(S, dtype=jnp.int32)
    return _rope_tables(t // GRID_W, t % GRID_W, t + N_META)


def _meta_tables():
    i = jnp.arange(META_ROWS, dtype=jnp.int32)
    return _rope_tables(-jnp.ones((META_ROWS,), jnp.int32), i, i)


def _layer(xr, xm, p, tabs_r, tabs_m):
    B, S, _ = xr.shape
    qa, ka, va, qb, kb, vb, uc = _proj(xr, p["w_qkv"], p["qg"], p["kg"], tabs_r,
                                       tm=ROW_TILE, cka=ATTN_TK, ckb=WIN_BLK)
    qam, kam, vam, qbm, kbm, vbm, ucm = _proj(xm, p["w_qkv"], p["qg"], p["kg"], tabs_m,
                                              tm=META_ROWS, cka=META_ROWS, ckb=META_ROWS)
    oa = _attn_a(qa, ka, va, kam, vam, tq=ATTN_TQ, tk=ATTN_TK)
    oam = _attn_a(qam, ka, va, kam, vam, tq=META_ROWS, tk=ATTN_TK)
    ob = _attn_b(p["sink"], qb, kb, vb, kbm, vbm, tq=WIN_TQ)
    obm = _attn_b_meta(p["sink"], qbm, kb, vb, kbm, vbm)
    d = _pool(uc, ucm, tm=ROW_TILE)
    dm = _pool_meta(ucm, uc, seq_len=N_META + S)
    merge_w = (p["wg"], p["wa"], p["wb"], p["pw"], p["ps"], p["wo"], p["ln1_g"], p["ln1_b"])
    hr = _merge(xr, oa, ob, d, *merge_w, tm=ROW_TILE)
    hm = _merge(xm, oam, obm, dm, *merge_w, tm=META_ROWS)
    ffn_w = (p["wu"], p["wd"], p["ln2_g"], p["ln2_b"])
    return _ffn(hr, *ffn_w, tm=ROW_TILE), _ffn(hm, *ffn_w, tm=META_ROWS)


def _trunk(x, meta_tokens, layers, tabs_r, tabs_m):
    B, S, D = x.shape
    xm = jnp.zeros((B, META_ROWS, D), x.dtype).at[:, :N_META].set(meta_tokens.astype(x.dtype)[None])
    xr = x
    for p in layers:
        xr, xm = _layer(xr, xm, p, tabs_r, tabs_m)
    return xr


def _layer_params(w_in, q_norm_g, k_norm_g, sink_logit, pool_w, pool_scale,
                  w_branch_a, w_branch_b, w_out, ln1_g, ln1_b, w_up, w_down, ln2_g, ln2_b):
    mm = lambda w: w.astype(MM_DTYPE)
    vec = lambda v: v.astype(F32).reshape(1, -1)
    layers = []
    for l in range(w_in.shape[0]):
        layers.append(dict(
            w_qkv=mm(w_in[l, :, :QKV_WIDTH]), wg=mm(w_in[l, :, QKV_WIDTH:]),
            qg=vec(q_norm_g[l]) * QK_SCALE, kg=vec(k_norm_g[l]), sink=sink_logit[l].astype(F32),
            pw=mm(pool_w[l]), ps=vec(pool_scale[l]),
            wa=mm(w_branch_a[l]), wb=mm(w_branch_b[l]), wo=mm(w_out[l]),
            ln1_g=vec(ln1_g[l]), ln1_b=vec(ln1_b[l]),
            wu=mm(w_up[l]), wd=mm(w_down[l]),
            ln2_g=vec(ln2_g[l]), ln2_b=vec(ln2_b[l]),
        ))
    return layers


def kernel(x_prompt, x_sample, meta_tokens, w_in, q_norm_g, k_norm_g, sink_logit, pool_w, pool_scale,
           w_branch_a, w_branch_b, w_out, ln1_g, ln1_b, w_up, w_down, ln2_g, ln2_b):
    layers = _layer_params(w_in, q_norm_g, k_norm_g, sink_logit, pool_w, pool_scale,
                           w_branch_a, w_branch_b, w_out, ln1_g, ln1_b, w_up, w_down, ln2_g, ln2_b)
    tabs_m = _meta_tables()
    tabs_s = _real_tables(x_sample.shape[1])
    tabs_p = tuple(t[:x_prompt.shape[1]] for t in tabs_s)
    y_prompt = _trunk(x_prompt, meta_tokens, layers, tabs_p, tabs_m)
    y_sample = _trunk(x_sample, meta_tokens, layers, tabs_s, tabs_m)
    return (y_prompt, y_sample)
```

```python
import functools

import jax
import jax.numpy as jnp
from jax import lax
from jax.experimental import pallas as pl
from jax.experimental.pallas import tpu as pltpu

D_MODEL = 1024
HEAD_DIM = 128
N_HEADS = 8
N_KV = 2
GROUP = N_HEADS // N_KV
WINDOW = 128
N_META = 16
META_ROWS = 128
GRID_W = 64
ROPE_THETA = 10000.0
POOL_WINDOWS = (2, 4, 8, 16)
POOL_DIM = D_MODEL // len(POOL_WINDOWS)
POOL_HALO = 8
D_FF = 2816
DEPTH = 2
ALPHA = (2 * DEPTH) ** 0.25
NEG = -1e30
QKV_WIDTH = 4096
LOG2E = 1.4426950408889634
QK_SCALE = HEAD_DIM ** -0.5 * LOG2E

MM_DTYPE = jnp.bfloat16
F32 = jnp.float32

V7X_VMEM_BYTES = 64 * 1024 * 1024
VMEM_CAP = V7X_VMEM_BYTES - 6 * 1024 * 1024

ROW_TILE = 512
DENSE_TILE = 1024
ATTN_TQ = 512
ATTN_TK = 512
WIN_TQ = 256
WIN_BLK = 128
FF_CHUNK = 256
SUB_ROWS = 256
WEIGHT_ROWS = 16
SUM_CHAINS = 1

_NT = (((1,), (1,)), ((), ()))


def _vmem_limit(block_bytes, scratch_bytes=0, temp_bytes=0):
    est = 2 * block_bytes + scratch_bytes + temp_bytes + (8 << 20)
    return int(min(max(est, 16 << 20), VMEM_CAP))


def _nbytes(shape, dtype):
    n = 1
    for s in shape:
        n *= s
    return n * jnp.dtype(dtype).itemsize


def _resident(shape):
    zeros = (0,) * len(shape)
    return pl.BlockSpec(shape, lambda *_: zeros, pipeline_mode=pl.Buffered(1))


def _layer_norm(x, g, b):
    mu = jnp.mean(x, axis=-1, keepdims=True)
    xc = x - mu
    var = jnp.mean(xc * xc, axis=-1, keepdims=True)
    return xc * lax.rsqrt(var + 1e-5) * g + b


def _proj_kernel(x_ref, w_ref, qg_ref, kg_ref, ca_ref, sa_ref, cb_ref, sb_ref,
                 qa_ref, ka_ref, va_ref, qb_ref, kb_ref, vb_ref, uc_ref, *, tm, cka, ckb):
    sub = min(tm, SUB_ROWS)
    qg = qg_ref[...]
    kg = kg_ref[...]
    lane = lax.broadcasted_iota(jnp.int32, (sub, HEAD_DIM), 1)
    low_quarter = (lane % 64) < 32

    def rms(x, g):
        return x * lax.rsqrt(jnp.mean(x * x, axis=-1, keepdims=True) + 1e-6) * g

    def head(u, j):
        return u[:, j * HEAD_DIM:(j + 1) * HEAD_DIM]

    for r in range(0, tm, sub):
        rows = slice(r, r + sub)
        xb = x_ref[0, rows, :].astype(MM_DTYPE)
        ca = ca_ref[rows, :]
        sa = sa_ref[rows, :]
        cb = cb_ref[rows, :]
        sb = sb_ref[rows, :]

        def rope_a(x):
            partner = jnp.where(low_quarter, pltpu.roll(x, 96, 1), pltpu.roll(x, 32, 1))
            return x * ca + partner * sa

        def rope_b(x):
            return x * cb + pltpu.roll(x, 64, 1) * sb

        def cols(c0, n):
            return jnp.dot(xb, w_ref[:, c0:c0 + n], preferred_element_type=F32)

        def store_vt(v_ref, v, ck):
            step = min(sub, ck)
            for o in range(0, sub, step):
                j, lane0 = (r + o) // ck, (r + o) % ck
                v_ref[0, j, :, lane0:lane0 + step] = v[o:o + step, :].T.astype(v_ref.dtype)

        for blk in range(2):
            u = cols(blk * 512, 512)
            for j in range(4):
                h = blk * 4 + j
                qa_ref[0, rows, h * HEAD_DIM:(h + 1) * HEAD_DIM] = rope_a(rms(head(u, j), qg)).astype(qa_ref.dtype)
        u = cols(1024, 512)
        for j in range(2):
            ka_ref[0, rows, j * HEAD_DIM:(j + 1) * HEAD_DIM] = rope_a(rms(head(u, j), kg)).astype(ka_ref.dtype)
        store_vt(va_ref, u[:, 256:512], cka)
        for blk in range(2):
            u = cols(1536 + blk * 512, 512)
            for j in range(4):
                h = blk * 4 + j
                qb_ref[0, rows, h * HEAD_DIM:(h + 1) * HEAD_DIM] = (
                    rope_b(head(u, j)) * QK_SCALE).astype(qb_ref.dtype)
        u = cols(2560, 512)
        for j in range(2):
            kb_ref[0, rows, j * HEAD_DIM:(j + 1) * HEAD_DIM] = rope_b(head(u, j)).astype(kb_ref.dtype)
        store_vt(vb_ref, u[:, 256:512], ckb)
        for blk in range(2):
            uc_ref[0, rows, blk * 512:(blk + 1) * 512] = cols(3072 + blk * 512, 512)


def _proj(x, w_qkv, qg, kg, tabs, *, tm, cka, ckb):
    B, T, D = x.shape
    nt = T // tm
    ca, sa, cb, sb = tabs
    row = lambda b, i: (b, i, 0)
    tab = pl.BlockSpec((tm, HEAD_DIM), lambda b, i: (i, 0))
    small = pl.BlockSpec((1, HEAD_DIM), lambda b, i: (0, 0))
    out_shape = (
        jax.ShapeDtypeStruct((B, T, 1024), MM_DTYPE),
        jax.ShapeDtypeStruct((B, T, 256), MM_DTYPE),
        jax.ShapeDtypeStruct((B, T // cka, 256, cka), MM_DTYPE),
        jax.ShapeDtypeStruct((B, T, 1024), MM_DTYPE),
        jax.ShapeDtypeStruct((B, T, 256), MM_DTYPE),
        jax.ShapeDtypeStruct((B, T // ckb, 256, ckb), MM_DTYPE),
        jax.ShapeDtypeStruct((B, T, 1024), F32),
    )
    out_specs = (
        pl.BlockSpec((1, tm, 1024), row),
        pl.BlockSpec((1, tm, 256), row),
        pl.BlockSpec((1, tm // cka, 256, cka), lambda b, i: (b, i, 0, 0)),
        pl.BlockSpec((1, tm, 1024), row),
        pl.BlockSpec((1, tm, 256), row),
        pl.BlockSpec((1, tm // ckb, 256, ckb), lambda b, i: (b, i, 0, 0)),
        pl.BlockSpec((1, tm, 1024), row),
    )
    blocks = (_nbytes((tm, D), F32) + 4 * _nbytes((tm, HEAD_DIM), F32)
              + _nbytes((tm, 3072), MM_DTYPE) + _nbytes((tm, 1024), F32))
    return pl.pallas_call(
        functools.partial(_proj_kernel, tm=tm, cka=cka, ckb=ckb),
        grid=(B, nt),
        in_specs=[pl.BlockSpec((1, tm, D), row), _resident(w_qkv.shape), small, small, tab, tab, tab, tab],
        out_specs=out_specs,
        out_shape=out_shape,
        compiler_params=pltpu.CompilerParams(
            dimension_semantics=("parallel", "parallel"),
            vmem_limit_bytes=_vmem_limit(blocks, _nbytes(w_qkv.shape, MM_DTYPE), 8 * _nbytes((tm, 512), F32))),
        name="proj",
    )(x, w_qkv, qg, kg, ca, sa, cb, sb)


def _attn_a_kernel(q_ref, k_ref, vt_ref, km_ref, vmt_ref, o_ref,
                   acc_ref, m_ref, l_ref, s_ref, p_ref, ms_ref, as_ref, ac_ref, *, tq, tk, n_chunks):
    def q_head(h):
        return q_ref[0, :, h * HEAD_DIM:(h + 1) * HEAD_DIM]

    def scores(c, slot, h):
        kc = k_ref[0, pl.ds(pl.multiple_of(c * tk, tk), tk), :]
        s = lax.dot_general(kc, q_head(h), _NT, preferred_element_type=F32)
        s_ref[slot, h] = s
        m_old = m_ref[h]
        m_new = jnp.maximum(m_old, jnp.max(s, axis=0, keepdims=True))
        m_ref[h] = m_new
        ms_ref[slot, h] = m_new
        as_ref[slot, h] = jnp.exp2(m_old - m_new)

    def weights(slot, h):
        a = as_ref[slot, h]
        m = ms_ref[slot, h]
        parts = [jnp.zeros((8, tq), F32) for _ in range(SUM_CHAINS)]
        for b, r in enumerate(range(0, tk, WEIGHT_ROWS)):
            p = jnp.exp2(s_ref[slot, h, r:r + WEIGHT_ROWS, :] - m)
            p_ref[slot, h, r:r + WEIGHT_ROWS, :] = p.astype(p_ref.dtype)
            for i in range(WEIGHT_ROWS // 8):
                parts[b % SUM_CHAINS] = parts[b % SUM_CHAINS] + p[i * 8:(i + 1) * 8, :]
        part = functools.reduce(lambda x, y: x + y, parts)
        l_ref[h] = a * l_ref[h] + jnp.sum(part, axis=0, keepdims=True)
        ac_ref[slot, h] = a

    def values(c, slot, h):
        pv = jnp.dot(vt_ref[0, c], p_ref[slot, h], preferred_element_type=F32)
        acc_ref[h] = ac_ref[slot, h] * acc_ref[h] + pv

    key_row = lax.broadcasted_iota(jnp.int32, (META_ROWS, tq), 0)
    s_meta = [lax.dot_general(km_ref[0], q_head(h), _NT, preferred_element_type=F32) for h in range(GROUP)]
    p_meta = []
    for h in range(GROUP):
        s = jnp.where(key_row < N_META, s_meta[h], NEG)
        m = jnp.max(s, axis=0, keepdims=True)
        p = jnp.exp2(s - m)
        m_ref[h] = m
        l_ref[h] = jnp.sum(p, axis=0, keepdims=True)
        p_meta.append(p.astype(MM_DTYPE))
    for h in range(GROUP):
        acc_ref[h] = jnp.dot(vmt_ref[0, 0], p_meta[h], preferred_element_type=F32)
    for h in range(GROUP):
        scores(0, 0, h)
    for h in range(GROUP):
        scores(1, 1, h)
        weights(0, h)

    def body(j, carry):
        c1 = 2 * j + 1
        for h in range(GROUP):
            scores(c1 + 1, 0, h)
            values(c1 - 1, 0, h)
            weights(1, h)
        for h in range(GROUP):
            scores(c1 + 2, 1, h)
            values(c1, 1, h)
            weights(0, h)
        return carry

    lax.fori_loop(0, n_chunks // 2 - 1, body, 0)
    for h in range(GROUP):
        values(n_chunks - 2, 0, h)
        weights(1, h)
    for h in range(GROUP):
        values(n_chunks - 1, 1, h)
    for h in range(GROUP):
        o = acc_ref[h] / l_ref[h]
        o_ref[0, :, h * HEAD_DIM:(h + 1) * HEAD_DIM] = o.T.astype(o_ref.dtype)


def _attn_a(q, k, vt, km, vmt, *, tq, tk):
    B, Tq, _ = q.shape
    S = k.shape[1]
    n_chunks = S // tk
    assert n_chunks % 2 == 0
    blocks = (_nbytes((tq, 512), MM_DTYPE) * 2 + 2 * _nbytes((S, HEAD_DIM), MM_DTYPE)
              + 2 * _nbytes((META_ROWS, HEAD_DIM), MM_DTYPE))
    stat = pltpu.VMEM((2, GROUP, 1, tq), F32)
    scratch = (GROUP * (_nbytes((HEAD_DIM, tq), F32) + 8 * _nbytes((8, tq), F32))
               + 2 * GROUP * (_nbytes((tk, tq), F32) + _nbytes((tk, tq), MM_DTYPE)))
    return pl.pallas_call(
        functools.partial(_attn_a_kernel, tq=tq, tk=tk, n_chunks=n_chunks),
        grid=(B, N_KV, Tq // tq),
        in_specs=[
            pl.BlockSpec((1, tq, GROUP * HEAD_DIM), lambda b, g, i: (b, i, g)),
            pl.BlockSpec((1, S, HEAD_DIM), lambda b, g, i: (b, 0, g)),
            pl.BlockSpec((1, n_chunks, HEAD_DIM, tk), lambda b, g, i: (b, 0, g, 0)),
            pl.BlockSpec((1, META_ROWS, HEAD_DIM), lambda b, g, i: (b, 0, g)),
            pl.BlockSpec((1, 1, HEAD_DIM, META_ROWS), lambda b, g, i: (b, 0, g, 0)),
        ],
        out_specs=pl.BlockSpec((1, tq, GROUP * HEAD_DIM), lambda b, g, i: (b, i, g)),
        out_shape=jax.ShapeDtypeStruct((B, Tq, N_HEADS * HEAD_DIM), MM_DTYPE),
        scratch_shapes=[
            pltpu.VMEM((GROUP, HEAD_DIM, tq), F32),
            pltpu.VMEM((GROUP, 1, tq), F32),
            pltpu.VMEM((GROUP, 1, tq), F32),
            pltpu.VMEM((2, GROUP, tk, tq), F32),
            pltpu.VMEM((2, GROUP, tk, tq), MM_DTYPE),
            stat, stat, stat,
        ],
        compiler_params=pltpu.CompilerParams(
            dimension_semantics=("parallel", "parallel", "parallel"),
            vmem_limit_bytes=_vmem_limit(blocks, scratch, 6 * _nbytes((tk, tq), F32))),
        name="attn_global",
    )(q, k, vt, km, vmt)


def _win_heads(sink_ref, q_ref, k_all, vt_all, mask, o_ref):
    g = pl.program_id(1)
    scores = [lax.dot_general(k_all, q_ref[0, :, h * HEAD_DIM:(h + 1) * HEAD_DIM], _NT,
                              preferred_element_type=F32) for h in range(GROUP)]
    for h in range(GROUP):
        s = jnp.where(mask, scores[h], NEG)
        sink = sink_ref[g * GROUP + h] * LOG2E
        m = jnp.maximum(jnp.max(s, axis=0, keepdims=True), sink)
        p = jnp.exp2(s - m)
        den = jnp.sum(p, axis=0, keepdims=True) + jnp.exp2(sink - m)
        o = jnp.dot(vt_all, p.astype(MM_DTYPE), preferred_element_type=F32) / den
        o_ref[0, :, h * HEAD_DIM:(h + 1) * HEAD_DIM] = o.T.astype(o_ref.dtype)


def _attn_b_kernel(sink_ref, q_ref, kp_ref, kc_ref, kn_ref, vp_ref, vc_ref, vn_ref, km_ref, vmt_ref, o_ref,
                   *, tq, n_tiles):
    i = pl.program_id(2)
    nb = tq // WIN_BLK
    k_all = jnp.concatenate([km_ref[0], kp_ref[0], kc_ref[0], kn_ref[0]], axis=0)
    vt_all = jnp.concatenate([vmt_ref[0, 0], vp_ref[0, 0]] + [vc_ref[0, j] for j in range(nb)] + [vn_ref[0, 0]],
                             axis=1)
    n_keys = META_ROWS + 2 * WIN_BLK + tq
    r = lax.broadcasted_iota(jnp.int32, (n_keys, tq), 0)
    c = lax.broadcasted_iota(jnp.int32, (n_keys, tq), 1)
    first_real = META_ROWS + jnp.where(i == 0, WIN_BLK, 0)
    end_real = n_keys - jnp.where(i == n_tiles - 1, WIN_BLK, 0)
    rel = r - c - (META_ROWS + WIN_BLK)
    in_win = (rel >= -WINDOW) & (rel <= WINDOW) & (r >= first_real) & (r < end_real)
    mask = (r < N_META) | in_win
    _win_heads(sink_ref, q_ref, k_all, vt_all, mask, o_ref)


def _attn_b_meta_kernel(sink_ref, q_ref, k0_ref, v0_ref, km_ref, vmt_ref, o_ref):
    k_all = jnp.concatenate([km_ref[0], k0_ref[0]], axis=0)
    vt_all = jnp.concatenate([vmt_ref[0, 0], v0_ref[0, 0]], axis=1)
    r = lax.broadcasted_iota(jnp.int32, (META_ROWS + WIN_BLK, META_ROWS), 0)
    c = lax.broadcasted_iota(jnp.int32, (META_ROWS + WIN_BLK, META_ROWS), 1)
    mask = (r < N_META) | ((r >= META_ROWS) & (r - META_ROWS + N_META - c <= WINDOW))
    _win_heads(sink_ref, q_ref, k_all, vt_all, mask, o_ref)


def _attn_b(sink, q, k, vt, km, vmt, *, tq):
    B, S, _ = q.shape
    n_tiles = S // tq
    nb = tq // WIN_BLK
    last_blk = S // WIN_BLK - 1
    kblk = lambda shape, f: pl.BlockSpec(shape, f)
    n_keys = META_ROWS + 2 * WIN_BLK + tq
    blocks = 2 * _nbytes((tq, 512), MM_DTYPE) + 2 * _nbytes((n_keys, HEAD_DIM), MM_DTYPE)
    grid_spec = pltpu.PrefetchScalarGridSpec(
        num_scalar_prefetch=1,
        grid=(B, N_KV, n_tiles),
        in_specs=[
            pl.BlockSpec((1, tq, GROUP * HEAD_DIM), lambda b, g, i, s: (b, i, g)),
            kblk((1, WIN_BLK, HEAD_DIM), lambda b, g, i, s: (b, jnp.maximum(i * nb - 1, 0), g)),
            kblk((1, tq, HEAD_DIM), lambda b, g, i, s: (b, i, g)),
            kblk((1, WIN_BLK, HEAD_DIM), lambda b, g, i, s: (b, jnp.minimum((i + 1) * nb, last_blk), g)),
            kblk((1, 1, HEAD_DIM, WIN_BLK), lambda b, g, i, s: (b, jnp.maximum(i * nb - 1, 0), g, 0)),
            kblk((1, nb, HEAD_DIM, WIN_BLK), lambda b, g, i, s: (b, i, g, 0)),
            kblk((1, 1, HEAD_DIM, WIN_BLK), lambda b, g, i, s: (b, jnp.minimum((i + 1) * nb, last_blk), g, 0)),
            kblk((1, META_ROWS, HEAD_DIM), lambda b, g, i, s: (b, 0, g)),
            kblk((1, 1, HEAD_DIM, META_ROWS), lambda b, g, i, s: (b, 0, g, 0)),
        ],
        out_specs=pl.BlockSpec((1, tq, GROUP * HEAD_DIM), lambda b, g, i, s: (b, i, g)),
    )
    return pl.pallas_call(
        functools.partial(_attn_b_kernel, tq=tq, n_tiles=n_tiles),
        grid_spec=grid_spec,
        out_shape=jax.ShapeDtypeStruct((B, S, N_HEADS * HEAD_DIM), MM_DTYPE),
        compiler_params=pltpu.CompilerParams(
            dimension_semantics=("parallel", "parallel", "parallel"),
            vmem_limit_bytes=_vmem_limit(blocks, 0, 8 * _nbytes((n_keys, tq), F32))),
        name="attn_window",
    )(sink, q, k, k, k, vt, vt, vt, km, vmt)


def _attn_b_meta(sink, qm, k, vt, km, vmt):
    B = qm.shape[0]
    blocks = 2 * _nbytes((META_ROWS, 512), MM_DTYPE) + 4 * _nbytes((META_ROWS, HEAD_DIM), MM_DTYPE)
    grid_spec = pltpu.PrefetchScalarGridSpec(
        num_scalar_prefetch=1,
        grid=(B, N_KV),
        in_specs=[
            pl.BlockSpec((1, META_ROWS, GROUP * HEAD_DIM), lambda b, g, s: (b, 0, g)),
            pl.BlockSpec((1, WIN_BLK, HEAD_DIM), lambda b, g, s: (b, 0, g)),
            pl.BlockSpec((1, 1, HEAD_DIM, WIN_BLK), lambda b, g, s: (b, 0, g, 0)),
            pl.BlockSpec((1, META_ROWS, HEAD_DIM), lambda b, g, s: (b, 0, g)),
            pl.BlockSpec((1, 1, HEAD_DIM, META_ROWS), lambda b, g, s: (b, 0, g, 0)),
        ],
        out_specs=pl.BlockSpec((1, META_ROWS, GROUP * HEAD_DIM), lambda b, g, s: (b, 0, g)),
    )
    return pl.pallas_call(
        _attn_b_meta_kernel,
        grid_spec=grid_spec,
        out_shape=jax.ShapeDtypeStruct((B, META_ROWS, N_HEADS * HEAD_DIM), MM_DTYPE),
        compiler_params=pltpu.CompilerParams(
            dimension_semantics=("parallel", "parallel"),
            vmem_limit_bytes=_vmem_limit(blocks, 0, 8 * _nbytes((2 * META_ROWS, META_ROWS), F32))),
        name="attn_window_meta",
    )(sink, qm, k, vt, km, vmt)


def _pool_rows(ext_ref, centre_ref, d_ref, pos, seq_len, tm):
    n = tm + 2 * POOL_HALO

    def ahead(x, k):
        return pltpu.roll(x, n - k, 0) if k % n else x

    for gi, win in enumerate(POOL_WINDOWS):
        c0 = gi * POOL_DIM
        s = ext_ref[:, c0:c0 + POOL_DIM]
        span = 1
        while span < win:
            s = s + ahead(s, span)
            span *= 2
        acc = ahead(s, POOL_HALO - win // 2)[0:tm]
        lo = jnp.maximum(pos - win // 2, 0)
        hi = jnp.minimum(pos - win // 2 + win, seq_len)
        mean = acc / (hi - lo).astype(F32)
        d_ref[0, :, c0:c0 + POOL_DIM] = (mean - centre_ref[0, :, c0:c0 + POOL_DIM]).astype(d_ref.dtype)


def _pool_kernel(prev_ref, main_ref, next_ref, meta_ref, d_ref, ext_ref, *, tm, n_tiles, seq_len):
    i = pl.program_id(1)
    ext_ref[0:POOL_HALO, :] = jnp.where(i == 0, meta_ref[0], prev_ref[0])
    ext_ref[POOL_HALO:POOL_HALO + tm, :] = main_ref[0]
    ext_ref[POOL_HALO + tm:2 * POOL_HALO + tm, :] = jnp.where(i == n_tiles - 1, 0.0, next_ref[0])
    pos = N_META + i * tm + lax.broadcasted_iota(jnp.int32, (tm, 1), 0)
    _pool_rows(ext_ref, main_ref, d_ref, pos, seq_len, tm)


def _pool_meta_kernel(meta_ref, first_ref, d_ref, ext_ref, *, seq_len):
    ext_ref[...] = jnp.zeros(ext_ref.shape, F32)
    ext_ref[POOL_HALO:POOL_HALO + N_META, :] = meta_ref[0, 0:N_META, :]
    ext_ref[POOL_HALO + N_META:2 * POOL_HALO + N_META, :] = first_ref[0]
    pos = lax.broadcasted_iota(jnp.int32, (META_ROWS, 1), 0)
    _pool_rows(ext_ref, meta_ref, d_ref, pos, seq_len, META_ROWS)


def _pool(uc, ucm, *, tm):
    B, S, C = uc.shape
    n_tiles = S // tm
    per = tm // POOL_HALO
    last = S // POOL_HALO - 1
    halo = lambda f: pl.BlockSpec((1, POOL_HALO, C), f)
    blocks = 2 * _nbytes((tm, C), F32) + _nbytes((tm, C), MM_DTYPE)
    return pl.pallas_call(
        functools.partial(_pool_kernel, tm=tm, n_tiles=n_tiles, seq_len=N_META + S),
        grid=(B, n_tiles),
        in_specs=[
            halo(lambda b, i: (b, jnp.maximum(i * per - 1, 0), 0)),
            pl.BlockSpec((1, tm, C), lambda b, i: (b, i, 0)),
            halo(lambda b, i: (b, jnp.minimum((i + 1) * per, last), 0)),
            halo(lambda b, i: (b, N_META // POOL_HALO - 1, 0)),
        ],
        out_specs=pl.BlockSpec((1, tm, C), lambda b, i: (b, i, 0)),
        out_shape=jax.ShapeDtypeStruct((B, S, C), MM_DTYPE),
        scratch_shapes=[pltpu.VMEM((tm + 2 * POOL_HALO, C), F32)],
        compiler_params=pltpu.CompilerParams(
            dimension_semantics=("parallel", "parallel"),
            vmem_limit_bytes=_vmem_limit(blocks, _nbytes((tm + 16, C), F32), 4 * _nbytes((tm, POOL_DIM), F32))),
        name="pool",
    )(uc, uc, uc, ucm)


def _pool_meta(ucm, uc, *, seq_len):
    B, _, C = ucm.shape
    return pl.pallas_call(
        functools.partial(_pool_meta_kernel, seq_len=seq_len),
        grid=(B,),
        in_specs=[
            pl.BlockSpec((1, META_ROWS, C), lambda b: (b, 0, 0)),
            pl.BlockSpec((1, POOL_HALO, C), lambda b: (b, 0, 0)),
        ],
        out_specs=pl.BlockSpec((1, META_ROWS, C), lambda b: (b, 0, 0)),
        out_shape=jax.ShapeDtypeStruct((B, META_ROWS, C), MM_DTYPE),
        scratch_shapes=[pltpu.VMEM((META_ROWS + 2 * POOL_HALO, C), F32)],
        compiler_params=pltpu.CompilerParams(
            dimension_semantics=("parallel",),
            vmem_limit_bytes=_vmem_limit(3 * _nbytes((META_ROWS, C), F32))),
        name="pool_meta",
    )(ucm, uc)


def _merge_kernel(h_ref, oa_ref, ob_ref, d_ref, wg_ref, wa_ref, wb_ref, pw_ref, ps_ref, wo_ref, g_ref, b_ref,
                  out_ref, *, tm):
    def gate(hb, j):
        return jax.nn.sigmoid(jnp.dot(hb, wg_ref[:, j * D_MODEL:(j + 1) * D_MODEL], preferred_element_type=F32))

    sub = min(tm, SUB_ROWS)
    for r in range(0, tm, sub):
        rows = slice(r, r + sub)
        h = h_ref[0, rows, :]
        hb = h.astype(MM_DTYPE)
        ya = jnp.dot(oa_ref[0, rows, :], wa_ref[...], preferred_element_type=F32)
        yb = jnp.dot(ob_ref[0, rows, :], wb_ref[...], preferred_element_type=F32)
        yc = jnp.concatenate(
            [jnp.dot(d_ref[0, rows, gi * POOL_DIM:(gi + 1) * POOL_DIM], pw_ref[gi], preferred_element_type=F32)
             for gi in range(len(POOL_WINDOWS))], axis=1) * ps_ref[...]
        merged = gate(hb, 0) * ya + gate(hb, 1) * yb + gate(hb, 2) * yc
        y = jnp.dot(merged.astype(MM_DTYPE), wo_ref[...], preferred_element_type=F32)
        out_ref[0, rows, :] = _layer_norm(ALPHA * h + y, g_ref[...], b_ref[...])


def _merge(h, oa, ob, d, wg, wa, wb, pw, ps, wo, g, b, *, tm):
    B, T, D = h.shape
    row = lambda bb, i: (bb, i, 0)
    vec = pl.BlockSpec((1, D), lambda bb, i: (0, 0))
    wbytes = sum(_nbytes(w.shape, MM_DTYPE) for w in (wg, wa, wb, pw, wo))
    blocks = 2 * _nbytes((tm, D), F32) + 3 * _nbytes((tm, D), MM_DTYPE)
    return pl.pallas_call(
        functools.partial(_merge_kernel, tm=tm),
        grid=(B, T // tm),
        in_specs=[pl.BlockSpec((1, tm, D), row)] * 4
        + [_resident(wg.shape), _resident(wa.shape), _resident(wb.shape), _resident(pw.shape), vec,
           _resident(wo.shape), vec, vec],
        out_specs=pl.BlockSpec((1, tm, D), row),
        out_shape=jax.ShapeDtypeStruct((B, T, D), F32),
        compiler_params=pltpu.CompilerParams(
            dimension_semantics=("parallel", "parallel"),
            vmem_limit_bytes=_vmem_limit(blocks, wbytes, 12 * _nbytes((min(tm, SUB_ROWS), D), F32))),
        name="merge",
    )(h, oa, ob, d, wg, wa, wb, pw, ps, wo, g, b)


def _ffn_kernel(h_ref, wu_ref, wd_ref, g_ref, b_ref, out_ref, act_ref, *, tm):
    sub = min(tm, SUB_ROWS)
    for r in range(0, tm, sub):
        h = h_ref[0, r:r + sub, :]
        hb = h.astype(MM_DTYPE)
        for c in range(D_FF // FF_CHUNK):
            c0 = c * FF_CHUNK
            gate = jnp.dot(hb, wu_ref[:, c0:c0 + FF_CHUNK], preferred_element_type=F32)
            up = jnp.dot(hb, wu_ref[:, D_FF + c0:D_FF + c0 + FF_CHUNK], preferred_element_type=F32)
            act_ref[r:r + sub, c0:c0 + FF_CHUNK] = (jax.nn.silu(gate) * up).astype(act_ref.dtype)
        f = jnp.dot(act_ref[r:r + sub, :], wd_ref[...], preferred_element_type=F32)
        out_ref[0, r:r + sub, :] = _layer_norm(ALPHA * h + f, g_ref[...], b_ref[...])


def _ffn(h, wu, wd, g, b, *, tm):
    B, T, D = h.shape
    row = lambda bb, i: (bb, i, 0)
    vec = pl.BlockSpec((1, D), lambda bb, i: (0, 0))
    wbytes = _nbytes(wu.shape, MM_DTYPE) + _nbytes(wd.shape, MM_DTYPE)
    return pl.pallas_call(
        functools.partial(_ffn_kernel, tm=tm),
        grid=(B, T // tm),
        in_specs=[pl.BlockSpec((1, tm, D), row), _resident(wu.shape), _resident(wd.shape), vec, vec],
        out_specs=pl.BlockSpec((1, tm, D), row),
        out_shape=jax.ShapeDtypeStruct((B, T, D), F32),
        scratch_shapes=[pltpu.VMEM((tm, D_FF), MM_DTYPE)],
        compiler_params=pltpu.CompilerParams(
            dimension_semantics=("parallel", "parallel"),
            vmem_limit_bytes=_vmem_limit(2 * _nbytes((tm, D), F32), wbytes + _nbytes((tm, D_FF), MM_DTYPE),
                                         6 * _nbytes((tm, D), F32))),
        name="ffn",
    )(h, wu, wd, g, b)


def _rope_tables(row, col, pos):
    def cs(p, dim):
        inv = ROPE_THETA ** (-jnp.arange(0, dim, 2, dtype=F32) / dim)
        ang = p.astype(F32)[:, None] * inv[None, :]
        return jnp.cos(ang), jnp.sin(ang)

    cr, sr = cs(row, HEAD_DIM // 2)
    cc, sc = cs(col, HEAD_DIM // 2)
    cp, sp = cs(pos, HEAD_DIM)
    ca = jnp.concatenate([cr, cr, cc, cc], axis=1)
    sa = jnp.concatenate([-sr, sr, -sc, sc], axis=1)
    cb = jnp.concatenate([cp, cp], axis=1)
    sb = jnp.concatenate([-sp, sp], axis=1)
    return ca, sa, cb, sb


def _real_tables(S):
    t = jnp.arange(S, dtype=jnp.int32)
    return _rope_tables(t // GRID_W, t % GRID_W, t + N_META)


def _meta_tables():
    i = jnp.arange(META_ROWS, dtype=jnp.int32)
    return _rope_tables(-jnp.ones((META_ROWS,), jnp.int32), i, i)


def _layer(xr, xm, p, tabs_r, tabs_m):
    B, S, _ = xr.shape
    qa, ka, va, qb, kb, vb, uc = _proj(xr, p["w_qkv"], p["qg"], p["kg"], tabs_r,
                                       tm=ROW_TILE, cka=ATTN_TK, ckb=WIN_BLK)
    qam, kam, vam, qbm, kbm, vbm, ucm = _proj(xm, p["w_qkv"], p["qg"], p["kg"], tabs_m,
                                              tm=META_ROWS, cka=META_ROWS, ckb=META_ROWS)
    oa = _attn_a(qa, ka, va, kam, vam, tq=ATTN_TQ, tk=ATTN_TK)
    oam = _attn_a(qam, ka, va, kam, vam, tq=META_ROWS, tk=ATTN_TK)
    ob = _attn_b(p["sink"], qb, kb, vb, kbm, vbm, tq=WIN_TQ)
    obm = _attn_b_meta(p["sink"], qbm, kb, vb, kbm, vbm)
    d = _pool(uc, ucm, tm=ROW_TILE)
    dm = _pool_meta(ucm, uc, seq_len=N_META + S)
    merge_w = (p["wg"], p["wa"], p["wb"], p["pw"], p["ps"], p["wo"], p["ln1_g"], p["ln1_b"])
    hr = _merge(xr, oa, ob, d, *merge_w, tm=DENSE_TILE)
    hm = _merge(xm, oam, obm, dm, *merge_w, tm=META_ROWS)
    ffn_w = (p["wu"], p["wd"], p["ln2_g"], p["ln2_b"])
    return _ffn(hr, *ffn_w, tm=DENSE_TILE), _ffn(hm, *ffn_w, tm=META_ROWS)


def _trunk(x, meta_tokens, layers, tabs_r, tabs_m):
    B, S, D = x.shape
    xm = jnp.zeros((B, META_ROWS, D), x.dtype).at[:, :N_META].set(meta_tokens.astype(x.dtype)[None])
    xr = x
    for p in layers:
        xr, xm = _layer(xr, xm, p, tabs_r, tabs_m)
    return xr


def _layer_params(w_in, q_norm_g, k_norm_g, sink_logit, pool_w, pool_scale,
                  w_branch_a, w_branch_b, w_out, ln1_g, ln1_b, w_up, w_down, ln2_g, ln2_b):
    mm = lambda w: w.astype(MM_DTYPE)
    vec = lambda v: v.astype(F32).reshape(1, -1)
    layers = []
    for l in range(w_in.shape[0]):
        layers.append(dict(
            w_qkv=mm(w_in[l, :, :QKV_WIDTH]), wg=mm(w_in[l, :, QKV_WIDTH:]),
            qg=vec(q_norm_g[l]) * QK_SCALE, kg=vec(k_norm_g[l]), sink=sink_logit[l].astype(F32),
            pw=mm(pool_w[l]), ps=vec(pool_scale[l]),
            wa=mm(w_branch_a[l]), wb=mm(w_branch_b[l]), wo=mm(w_out[l]),
            ln1_g=vec(ln1_g[l]), ln1_b=vec(ln1_b[l]),
            wu=mm(w_up[l]), wd=mm(w_down[l]),
            ln2_g=vec(ln2_g[l]), ln2_b=vec(ln2_b[l]),
        ))
    return layers


def kernel(x_prompt, x_sample, meta_tokens, w_in, q_norm_g, k_norm_g, sink_logit, pool_w, pool_scale,
           w_branch_a, w_branch_b, w_out, ln1_g, ln1_b, w_up, w_down, ln2_g, ln2_b):
    layers = _layer_params(w_in, q_norm_g, k_norm_g, sink_logit, pool_w, pool_scale,
                           w_branch_a, w_branch_b, w_out, ln1_g, ln1_b, w_up, w_down, ln2_g, ln2_b)
    tabs_m = _meta_tables()
    tabs_s = _real_tables(x_sample.shape[1])
    tabs_p = tuple(t[:x_prompt.shape[1]] for t in tabs_s)
    y_prompt = _trunk(x_prompt, meta_tokens, layers, tabs_p, tabs_m)
    y_sample = _trunk(x_sample, meta_tokens, layers, tabs_s, tabs_m)
    return (y_prompt, y_sample)
```

```python
import functools

import jax
import jax.numpy as jnp
from jax import lax
from jax.experimental import pallas as pl
from jax.experimental.pallas import tpu as pltpu

D_MODEL = 1024
HEAD_DIM = 128
N_HEADS = 8
N_KV = 2
GROUP = N_HEADS // N_KV
WINDOW = 128
N_META = 16
META_ROWS = 128
GRID_W = 64
ROPE_THETA = 10000.0
POOL_WINDOWS = (2, 4, 8, 16)
POOL_DIM = D_MODEL // len(POOL_WINDOWS)
POOL_HALO = 8
D_FF = 2816
DEPTH = 2
ALPHA = (2 * DEPTH) ** 0.25
NEG = -1e30
QKV_WIDTH = 4096
LOG2E = 1.4426950408889634
QK_SCALE = HEAD_DIM ** -0.5 * LOG2E

MM_DTYPE = jnp.bfloat16
F32 = jnp.float32

V7X_VMEM_BYTES = 64 * 1024 * 1024
VMEM_CAP = V7X_VMEM_BYTES - 6 * 1024 * 1024

ROW_TILE = 512
DENSE_TILE = 1024
ATTN_TQ = 512
ATTN_TK = 512
WIN_TQ = 1024
WIN_SUB = 256
WIN_BLK = 128
FF_CHUNK = 256
SUB_ROWS = 256
WEIGHT_ROWS = 32
DEN_ROWS_BY_LAYER = (0, 16)

_NT = (((1,), (1,)), ((), ()))


def _vmem_limit(block_bytes, scratch_bytes=0, temp_bytes=0):
    est = 2 * block_bytes + scratch_bytes + temp_bytes + (8 << 20)
    return int(min(max(est, 16 << 20), VMEM_CAP))


def _nbytes(shape, dtype):
    n = 1
    for s in shape:
        n *= s
    return n * jnp.dtype(dtype).itemsize


def _resident(shape):
    zeros = (0,) * len(shape)
    return pl.BlockSpec(shape, lambda *_: zeros, pipeline_mode=pl.Buffered(1))


def _layer_norm(x, g, b):
    mu = jnp.mean(x, axis=-1, keepdims=True)
    xc = x - mu
    var = jnp.mean(xc * xc, axis=-1, keepdims=True)
    return xc * lax.rsqrt(var + 1e-5) * g + b


def _proj_kernel(x_ref, w_ref, qg_ref, kg_ref, ca_ref, sa_ref, cb_ref, sb_ref,
                 qa_ref, ka_ref, va_ref, qb_ref, kb_ref, vb_ref, uc_ref, *, tm, cka, ckb):
    sub = min(tm, SUB_ROWS)
    qg = qg_ref[...]
    kg = kg_ref[...]
    lane = lax.broadcasted_iota(jnp.int32, (sub, HEAD_DIM), 1)
    low_quarter = (lane % 64) < 32

    def rms(x, g):
        return x * lax.rsqrt(jnp.mean(x * x, axis=-1, keepdims=True) + 1e-6) * g

    def head(u, j):
        return u[:, j * HEAD_DIM:(j + 1) * HEAD_DIM]

    for r in range(0, tm, sub):
        rows = slice(r, r + sub)
        xb = x_ref[0, rows, :].astype(MM_DTYPE)
        ca = ca_ref[rows, :]
        sa = sa_ref[rows, :]
        cb = cb_ref[rows, :]
        sb = sb_ref[rows, :]

        def rope_a(x):
            partner = jnp.where(low_quarter, pltpu.roll(x, 96, 1), pltpu.roll(x, 32, 1))
            return x * ca + partner * sa

        def rope_b(x):
            return x * cb + pltpu.roll(x, 64, 1) * sb

        def cols(c0, n):
            return jnp.dot(xb, w_ref[:, c0:c0 + n], preferred_element_type=F32)

        def store_vt(v_ref, v, ck):
            step = min(sub, ck)
            for o in range(0, sub, step):
                j, lane0 = (r + o) // ck, (r + o) % ck
                v_ref[0, j, :, lane0:lane0 + step] = v[o:o + step, :].T.astype(v_ref.dtype)

        for blk in range(2):
            u = cols(blk * 512, 512)
            for j in range(4):
                h = blk * 4 + j
                qa_ref[0, rows, h * HEAD_DIM:(h + 1) * HEAD_DIM] = rope_a(rms(head(u, j), qg)).astype(qa_ref.dtype)
        u = cols(1024, 512)
        for j in range(2):
            ka_ref[0, rows, j * HEAD_DIM:(j + 1) * HEAD_DIM] = rope_a(rms(head(u, j), kg)).astype(ka_ref.dtype)
        store_vt(va_ref, u[:, 256:512], cka)
        for blk in range(2):
            u = cols(1536 + blk * 512, 512)
            for j in range(4):
                h = blk * 4 + j
                qb_ref[0, rows, h * HEAD_DIM:(h + 1) * HEAD_DIM] = (
                    rope_b(head(u, j)) * QK_SCALE).astype(qb_ref.dtype)
        u = cols(2560, 512)
        for j in range(2):
            kb_ref[0, rows, j * HEAD_DIM:(j + 1) * HEAD_DIM] = rope_b(head(u, j)).astype(kb_ref.dtype)
        store_vt(vb_ref, u[:, 256:512], ckb)
        for blk in range(2):
            uc_ref[0, rows, blk * 512:(blk + 1) * 512] = cols(3072 + blk * 512, 512)


def _proj(x, w_qkv, qg, kg, tabs, *, tm, cka, ckb):
    B, T, D = x.shape
    nt = T // tm
    ca, sa, cb, sb = tabs
    row = lambda b, i: (b, i, 0)
    tab = pl.BlockSpec((tm, HEAD_DIM), lambda b, i: (i, 0))
    small = pl.BlockSpec((1, HEAD_DIM), lambda b, i: (0, 0))
    out_shape = (
        jax.ShapeDtypeStruct((B, T, 1024), MM_DTYPE),
        jax.ShapeDtypeStruct((B, T, 256), MM_DTYPE),
        jax.ShapeDtypeStruct((B, T // cka, 256, cka), MM_DTYPE),
        jax.ShapeDtypeStruct((B, T, 1024), MM_DTYPE),
        jax.ShapeDtypeStruct((B, T, 256), MM_DTYPE),
        jax.ShapeDtypeStruct((B, T // ckb, 256, ckb), MM_DTYPE),
        jax.ShapeDtypeStruct((B, T, 1024), F32),
    )
    out_specs = (
        pl.BlockSpec((1, tm, 1024), row),
        pl.BlockSpec((1, tm, 256), row),
        pl.BlockSpec((1, tm // cka, 256, cka), lambda b, i: (b, i, 0, 0)),
        pl.BlockSpec((1, tm, 1024), row),
        pl.BlockSpec((1, tm, 256), row),
        pl.BlockSpec((1, tm // ckb, 256, ckb), lambda b, i: (b, i, 0, 0)),
        pl.BlockSpec((1, tm, 1024), row),
    )
    blocks = (_nbytes((tm, D), F32) + 4 * _nbytes((tm, HEAD_DIM), F32)
              + _nbytes((tm, 3072), MM_DTYPE) + _nbytes((tm, 1024), F32))
    return pl.pallas_call(
        functools.partial(_proj_kernel, tm=tm, cka=cka, ckb=ckb),
        grid=(B, nt),
        in_specs=[pl.BlockSpec((1, tm, D), row), _resident(w_qkv.shape), small, small, tab, tab, tab, tab],
        out_specs=out_specs,
        out_shape=out_shape,
        compiler_params=pltpu.CompilerParams(
            dimension_semantics=("parallel", "parallel"),
            vmem_limit_bytes=_vmem_limit(blocks, _nbytes(w_qkv.shape, MM_DTYPE), 8 * _nbytes((tm, 512), F32))),
        name="proj",
    )(x, w_qkv, qg, kg, ca, sa, cb, sb)


def _attn_a_kernel(q_ref, k_ref, vt_ref, km_ref, vmt_ref, o_ref,
                   acc_ref, m_ref, l_ref, s_ref, p_ref, ms_ref, as_ref, ac_ref, *, tq, tk, n_chunks, den_rows):
    def q_head(h):
        return q_ref[0, :, h * HEAD_DIM:(h + 1) * HEAD_DIM]

    def with_ones(vt):
        if not den_rows:
            return vt
        return jnp.concatenate([vt, jnp.ones((den_rows, vt.shape[1]), vt.dtype)], axis=0)

    def scores(c, slot, h):
        kc = k_ref[0, pl.ds(pl.multiple_of(c * tk, tk), tk), :]
        s = lax.dot_general(kc, q_head(h), _NT, preferred_element_type=F32)
        s_ref[slot, h] = s
        m_old = m_ref[h]
        m_new = jnp.maximum(m_old, jnp.max(s, axis=0, keepdims=True))
        m_ref[h] = m_new
        ms_ref[slot, h] = m_new
        as_ref[slot, h] = jnp.exp2(m_old - m_new)

    def weights(slot, h):
        a = as_ref[slot, h]
        m = ms_ref[slot, h]
        ac_ref[slot, h] = a
        if den_rows:
            p_ref[slot, h] = jnp.exp2(s_ref[slot, h] - m).astype(p_ref.dtype)
            return
        part = jnp.zeros((8, tq), F32)
        for r in range(0, tk, WEIGHT_ROWS):
            p = jnp.exp2(s_ref[slot, h, r:r + WEIGHT_ROWS, :] - m)
            p_ref[slot, h, r:r + WEIGHT_ROWS, :] = p.astype(p_ref.dtype)
            slabs = [p[i * 8:(i + 1) * 8, :] for i in range(WEIGHT_ROWS // 8)]
            while len(slabs) > 1:
                slabs = [slabs[i] + slabs[i + 1] for i in range(0, len(slabs), 2)]
            part = part + slabs[0]
        l_ref[h] = a * l_ref[h] + jnp.sum(part, axis=0, keepdims=True)

    def values(c, slot, h):
        pv = jnp.dot(with_ones(vt_ref[0, c]), p_ref[slot, h], preferred_element_type=F32)
        acc_ref[h] = ac_ref[slot, h] * acc_ref[h] + pv

    key_row = lax.broadcasted_iota(jnp.int32, (META_ROWS, tq), 0)
    s_meta = [lax.dot_general(km_ref[0], q_head(h), _NT, preferred_element_type=F32) for h in range(GROUP)]
    p_meta = []
    for h in range(GROUP):
        s = jnp.where(key_row < N_META, s_meta[h], NEG)
        m = jnp.max(s, axis=0, keepdims=True)
        p = jnp.exp2(s - m)
        m_ref[h] = m
        if not den_rows:
            l_ref[h] = jnp.sum(p, axis=0, keepdims=True)
        p_meta.append(p.astype(MM_DTYPE))
    for h in range(GROUP):
        acc_ref[h] = jnp.dot(with_ones(vmt_ref[0, 0]), p_meta[h], preferred_element_type=F32)
    for h in range(GROUP):
        scores(0, 0, h)
    for h in range(GROUP):
        scores(1, 1, h)
        weights(0, h)

    def body(j, carry):
        c1 = 2 * j + 1
        for h in range(GROUP):
            scores(c1 + 1, 0, h)
            values(c1 - 1, 0, h)
            weights(1, h)
        for h in range(GROUP):
            scores(c1 + 2, 1, h)
            values(c1, 1, h)
            weights(0, h)
        return carry

    lax.fori_loop(0, n_chunks // 2 - 1, body, 0)
    for h in range(GROUP):
        values(n_chunks - 2, 0, h)
        weights(1, h)
    for h in range(GROUP):
        values(n_chunks - 1, 1, h)
    for h in range(GROUP):
        den = acc_ref[h, HEAD_DIM:HEAD_DIM + 1, :] if den_rows else l_ref[h]
        o = acc_ref[h, 0:HEAD_DIM, :] / den
        o_ref[0, :, h * HEAD_DIM:(h + 1) * HEAD_DIM] = o.T.astype(o_ref.dtype)


def _attn_a(q, k, vt, km, vmt, *, tq, tk, den_rows):
    B, Tq, _ = q.shape
    S = k.shape[1]
    n_chunks = S // tk
    assert n_chunks % 2 == 0
    blocks = (_nbytes((tq, 512), MM_DTYPE) * 2 + 2 * _nbytes((S, HEAD_DIM), MM_DTYPE)
              + 2 * _nbytes((META_ROWS, HEAD_DIM), MM_DTYPE))
    stat = pltpu.VMEM((2, GROUP, 1, tq), F32)
    scratch = (GROUP * (_nbytes((HEAD_DIM + den_rows, tq), F32) + 8 * _nbytes((8, tq), F32))
               + 2 * GROUP * (_nbytes((tk, tq), F32) + _nbytes((tk, tq), MM_DTYPE)))
    return pl.pallas_call(
        functools.partial(_attn_a_kernel, tq=tq, tk=tk, n_chunks=n_chunks, den_rows=den_rows),
        grid=(B, N_KV, Tq // tq),
        in_specs=[
            pl.BlockSpec((1, tq, GROUP * HEAD_DIM), lambda b, g, i: (b, i, g)),
            pl.BlockSpec((1, S, HEAD_DIM), lambda b, g, i: (b, 0, g)),
            pl.BlockSpec((1, n_chunks, HEAD_DIM, tk), lambda b, g, i: (b, 0, g, 0)),
            pl.BlockSpec((1, META_ROWS, HEAD_DIM), lambda b, g, i: (b, 0, g)),
            pl.BlockSpec((1, 1, HEAD_DIM, META_ROWS), lambda b, g, i: (b, 0, g, 0)),
        ],
        out_specs=pl.BlockSpec((1, tq, GROUP * HEAD_DIM), lambda b, g, i: (b, i, g)),
        out_shape=jax.ShapeDtypeStruct((B, Tq, N_HEADS * HEAD_DIM), MM_DTYPE),
        scratch_shapes=[
            pltpu.VMEM((GROUP, HEAD_DIM + den_rows, tq), F32),
            pltpu.VMEM((GROUP, 1, tq), F32),
            pltpu.VMEM((GROUP, 1, tq), F32),
            pltpu.VMEM((2, GROUP, tk, tq), F32),
            pltpu.VMEM((2, GROUP, tk, tq), MM_DTYPE),
            stat, stat, stat,
        ],
        compiler_params=pltpu.CompilerParams(
            dimension_semantics=("parallel", "parallel", "parallel"),
            vmem_limit_bytes=_vmem_limit(blocks, scratch, 6 * _nbytes((tk, tq), F32))),
        name="attn_global",
    )(q, k, vt, km, vmt)


def _win_attend(sink_ref, q_ref, o_ref, problems):
    g = pl.program_id(1)
    scores = [[lax.dot_general(k_all, q_ref[0, col0:col0 + width, h * HEAD_DIM:(h + 1) * HEAD_DIM], _NT,
                               preferred_element_type=F32) for h in range(GROUP)]
              for col0, width, k_all, _, _ in problems]
    for (col0, width, _, vt_all, mask), s_heads in zip(problems, scores):
        for h in range(GROUP):
            s = jnp.where(mask, s_heads[h], NEG)
            sink = sink_ref[g * GROUP + h] * LOG2E
            m = jnp.maximum(jnp.max(s, axis=0, keepdims=True), sink)
            p = jnp.exp2(s - m)
            den = jnp.sum(p, axis=0, keepdims=True) + jnp.exp2(sink - m)
            o = jnp.dot(vt_all, p.astype(MM_DTYPE), preferred_element_type=F32) / den
            o_ref[0, col0:col0 + width, h * HEAD_DIM:(h + 1) * HEAD_DIM] = o.T.astype(o_ref.dtype)


def _attn_b_kernel(sink_ref, q_ref, kp_ref, kc_ref, kn_ref, vp_ref, vc_ref, vn_ref, km_ref, vmt_ref, o_ref,
                   *, tq, n_tiles):
    i = pl.program_id(2)
    nb = tq // WIN_BLK
    n_sub = tq // WIN_SUB
    k_real = jnp.concatenate([kp_ref[0], kc_ref[0], kn_ref[0]], axis=0)
    vt_real = jnp.concatenate([vp_ref[0, 0]] + [vc_ref[0, j] for j in range(nb)] + [vn_ref[0, 0]], axis=1)
    n_keys = META_ROWS + 2 * WIN_BLK + WIN_SUB
    r = lax.broadcasted_iota(jnp.int32, (n_keys, WIN_SUB), 0)
    c = lax.broadcasted_iota(jnp.int32, (n_keys, WIN_SUB), 1)
    rel = r - c - (META_ROWS + WIN_BLK)
    in_win = (rel >= -WINDOW) & (rel <= WINDOW) & (r >= META_ROWS)
    problems = []
    for j in range(n_sub):
        span = slice(j * WIN_SUB, j * WIN_SUB + WIN_SUB + 2 * WIN_BLK)
        k_all = jnp.concatenate([km_ref[0], k_real[span, :]], axis=0)
        vt_all = jnp.concatenate([vmt_ref[0, 0], vt_real[:, span]], axis=1)
        valid = in_win
        if j == 0:
            valid = valid & (r >= META_ROWS + jnp.where(i == 0, WIN_BLK, 0))
        if j == n_sub - 1:
            valid = valid & (r < n_keys - jnp.where(i == n_tiles - 1, WIN_BLK, 0))
        problems.append((j * WIN_SUB, WIN_SUB, k_all, vt_all, (r < N_META) | valid))
    _win_attend(sink_ref, q_ref, o_ref, problems)


def _attn_b_meta_kernel(sink_ref, q_ref, k0_ref, v0_ref, km_ref, vmt_ref, o_ref):
    k_all = jnp.concatenate([km_ref[0], k0_ref[0]], axis=0)
    vt_all = jnp.concatenate([vmt_ref[0, 0], v0_ref[0, 0]], axis=1)
    r = lax.broadcasted_iota(jnp.int32, (META_ROWS + WIN_BLK, META_ROWS), 0)
    c = lax.broadcasted_iota(jnp.int32, (META_ROWS + WIN_BLK, META_ROWS), 1)
    mask = (r < N_META) | ((r >= META_ROWS) & (r - META_ROWS + N_META - c <= WINDOW))
    _win_attend(sink_ref, q_ref, o_ref, [(0, META_ROWS, k_all, vt_all, mask)])


def _attn_b(sink, q, k, vt, km, vmt, *, tq):
    B, S, _ = q.shape
    n_tiles = S // tq
    nb = tq // WIN_BLK
    last_blk = S // WIN_BLK - 1
    kblk = lambda shape, f: pl.BlockSpec(shape, f)
    n_keys = META_ROWS + 2 * WIN_BLK + tq
    blocks = 2 * _nbytes((tq, 512), MM_DTYPE) + 2 * _nbytes((n_keys, HEAD_DIM), MM_DTYPE)
    grid_spec = pltpu.PrefetchScalarGridSpec(
        num_scalar_prefetch=1,
        grid=(B, N_KV, n_tiles),
        in_specs=[
            pl.BlockSpec((1, tq, GROUP * HEAD_DIM), lambda b, g, i, s: (b, i, g)),
            kblk((1, WIN_BLK, HEAD_DIM), lambda b, g, i, s: (b, jnp.maximum(i * nb - 1, 0), g)),
            kblk((1, tq, HEAD_DIM), lambda b, g, i, s: (b, i, g)),
            kblk((1, WIN_BLK, HEAD_DIM), lambda b, g, i, s: (b, jnp.minimum((i + 1) * nb, last_blk), g)),
            kblk((1, 1, HEAD_DIM, WIN_BLK), lambda b, g, i, s: (b, jnp.maximum(i * nb - 1, 0), g, 0)),
            kblk((1, nb, HEAD_DIM, WIN_BLK), lambda b, g, i, s: (b, i, g, 0)),
            kblk((1, 1, HEAD_DIM, WIN_BLK), lambda b, g, i, s: (b, jnp.minimum((i + 1) * nb, last_blk), g, 0)),
            kblk((1, META_ROWS, HEAD_DIM), lambda b, g, i, s: (b, 0, g)),
            kblk((1, 1, HEAD_DIM, META_ROWS), lambda b, g, i, s: (b, 0, g, 0)),
        ],
        out_specs=pl.BlockSpec((1, tq, GROUP * HEAD_DIM), lambda b, g, i, s: (b, i, g)),
    )
    return pl.pallas_call(
        functools.partial(_attn_b_kernel, tq=tq, n_tiles=n_tiles),
        grid_spec=grid_spec,
        out_shape=jax.ShapeDtypeStruct((B, S, N_HEADS * HEAD_DIM), MM_DTYPE),
        compiler_params=pltpu.CompilerParams(
            dimension_semantics=("parallel", "parallel", "parallel"),
            vmem_limit_bytes=_vmem_limit(blocks, 0, 8 * _nbytes((n_keys, tq), F32))),
        name="attn_window",
    )(sink, q, k, k, k, vt, vt, vt, km, vmt)


def _attn_b_meta(sink, qm, k, vt, km, vmt):
    B = qm.shape[0]
    blocks = 2 * _nbytes((META_ROWS, 512), MM_DTYPE) + 4 * _nbytes((META_ROWS, HEAD_DIM), MM_DTYPE)
    grid_spec = pltpu.PrefetchScalarGridSpec(
        num_scalar_prefetch=1,
        grid=(B, N_KV),
        in_specs=[
            pl.BlockSpec((1, META_ROWS, GROUP * HEAD_DIM), lambda b, g, s: (b, 0, g)),
            pl.BlockSpec((1, WIN_BLK, HEAD_DIM), lambda b, g, s: (b, 0, g)),
            pl.BlockSpec((1, 1, HEAD_DIM, WIN_BLK), lambda b, g, s: (b, 0, g, 0)),
            pl.BlockSpec((1, META_ROWS, HEAD_DIM), lambda b, g, s: (b, 0, g)),
            pl.BlockSpec((1, 1, HEAD_DIM, META_ROWS), lambda b, g, s: (b, 0, g, 0)),
        ],
        out_specs=pl.BlockSpec((1, META_ROWS, GROUP * HEAD_DIM), lambda b, g, s: (b, 0, g)),
    )
    return pl.pallas_call(
        _attn_b_meta_kernel,
        grid_spec=grid_spec,
        out_shape=jax.ShapeDtypeStruct((B, META_ROWS, N_HEADS * HEAD_DIM), MM_DTYPE),
        compiler_params=pltpu.CompilerParams(
            dimension_semantics=("parallel", "parallel"),
            vmem_limit_bytes=_vmem_limit(blocks, 0, 8 * _nbytes((2 * META_ROWS, META_ROWS), F32))),
        name="attn_window_meta",
    )(sink, qm, k, vt, km, vmt)


def _pool_rows(ext_ref, centre_ref, d_ref, pos, seq_len, tm):
    n = tm + 2 * POOL_HALO

    def ahead(x, k):
        return pltpu.roll(x, n - k, 0) if k % n else x

    for gi, win in enumerate(POOL_WINDOWS):
        c0 = gi * POOL_DIM
        s = ext_ref[:, c0:c0 + POOL_DIM]
        span = 1
        while span < win:
            s = s + ahead(s, span)
            span *= 2
        acc = ahead(s, POOL_HALO - win // 2)[0:tm]
        lo = jnp.maximum(pos - win // 2, 0)
        hi = jnp.minimum(pos - win // 2 + win, seq_len)
        mean = acc / (hi - lo).astype(F32)
        d_ref[0, :, c0:c0 + POOL_DIM] = (mean - centre_ref[0, :, c0:c0 + POOL_DIM]).astype(d_ref.dtype)


def _pool_kernel(prev_ref, main_ref, next_ref, meta_ref, d_ref, ext_ref, *, tm, n_tiles, seq_len):
    i = pl.program_id(1)
    ext_ref[0:POOL_HALO, :] = jnp.where(i == 0, meta_ref[0], prev_ref[0])
    ext_ref[POOL_HALO:POOL_HALO + tm, :] = main_ref[0]
    ext_ref[POOL_HALO + tm:2 * POOL_HALO + tm, :] = jnp.where(i == n_tiles - 1, 0.0, next_ref[0])
    pos = N_META + i * tm + lax.broadcasted_iota(jnp.int32, (tm, 1), 0)
    _pool_rows(ext_ref, main_ref, d_ref, pos, seq_len, tm)


def _pool_meta_kernel(meta_ref, first_ref, d_ref, ext_ref, *, seq_len):
    ext_ref[...] = jnp.zeros(ext_ref.shape, F32)
    ext_ref[POOL_HALO:POOL_HALO + N_META, :] = meta_ref[0, 0:N_META, :]
    ext_ref[POOL_HALO + N_META:2 * POOL_HALO + N_META, :] = first_ref[0]
    pos = lax.broadcasted_iota(jnp.int32, (META_ROWS, 1), 0)
    _pool_rows(ext_ref, meta_ref, d_ref, pos, seq_len, META_ROWS)


def _pool(uc, ucm, *, tm):
    B, S, C = uc.shape
    n_tiles = S // tm
    per = tm // POOL_HALO
    last = S // POOL_HALO - 1
    halo = lambda f: pl.BlockSpec((1, POOL_HALO, C), f)
    blocks = 2 * _nbytes((tm, C), F32) + _nbytes((tm, C), MM_DTYPE)
    return pl.pallas_call(
        functools.partial(_pool_kernel, tm=tm, n_tiles=n_tiles, seq_len=N_META + S),
        grid=(B, n_tiles),
        in_specs=[
            halo(lambda b, i: (b, jnp.maximum(i * per - 1, 0), 0)),
            pl.BlockSpec((1, tm, C), lambda b, i: (b, i, 0)),
            halo(lambda b, i: (b, jnp.minimum((i + 1) * per, last), 0)),
            halo(lambda b, i: (b, N_META // POOL_HALO - 1, 0)),
        ],
        out_specs=pl.BlockSpec((1, tm, C), lambda b, i: (b, i, 0)),
        out_shape=jax.ShapeDtypeStruct((B, S, C), MM_DTYPE),
        scratch_shapes=[pltpu.VMEM((tm + 2 * POOL_HALO, C), F32)],
        compiler_params=pltpu.CompilerParams(
            dimension_semantics=("parallel", "parallel"),
            vmem_limit_bytes=_vmem_limit(blocks, _nbytes((tm + 16, C), F32), 4 * _nbytes((tm, POOL_DIM), F32))),
        name="pool",
    )(uc, uc, uc, ucm)


def _pool_meta(ucm, uc, *, seq_len):
    B, _, C = ucm.shape
    return pl.pallas_call(
        functools.partial(_pool_meta_kernel, seq_len=seq_len),
        grid=(B,),
        in_specs=[
            pl.BlockSpec((1, META_ROWS, C), lambda b: (b, 0, 0)),
            pl.BlockSpec((1, POOL_HALO, C), lambda b: (b, 0, 0)),
        ],
        out_specs=pl.BlockSpec((1, META_ROWS, C), lambda b: (b, 0, 0)),
        out_shape=jax.ShapeDtypeStruct((B, META_ROWS, C), MM_DTYPE),
        scratch_shapes=[pltpu.VMEM((META_ROWS + 2 * POOL_HALO, C), F32)],
        compiler_params=pltpu.CompilerParams(
            dimension_semantics=("parallel",),
            vmem_limit_bytes=_vmem_limit(3 * _nbytes((META_ROWS, C), F32))),
        name="pool_meta",
    )(ucm, uc)


def _merge_kernel(h_ref, oa_ref, ob_ref, d_ref, wg_ref, wa_ref, wb_ref, pw_ref, ps_ref, wo_ref, g_ref, b_ref,
                  out_ref, *, tm):
    def gate(hb, j):
        return jax.nn.sigmoid(jnp.dot(hb, wg_ref[:, j * D_MODEL:(j + 1) * D_MODEL], preferred_element_type=F32))

    sub = min(tm, SUB_ROWS)
    for r in range(0, tm, sub):
        rows = slice(r, r + sub)
        h = h_ref[0, rows, :]
        hb = h.astype(MM_DTYPE)
        ya = jnp.dot(oa_ref[0, rows, :], wa_ref[...], preferred_element_type=F32)
        yb = jnp.dot(ob_ref[0, rows, :], wb_ref[...], preferred_element_type=F32)
        yc = jnp.concatenate(
            [jnp.dot(d_ref[0, rows, gi * POOL_DIM:(gi + 1) * POOL_DIM], pw_ref[gi], preferred_element_type=F32)
             for gi in range(len(POOL_WINDOWS))], axis=1) * ps_ref[...]
        merged = gate(hb, 0) * ya + gate(hb, 1) * yb + gate(hb, 2) * yc
        y = jnp.dot(merged.astype(MM_DTYPE), wo_ref[...], preferred_element_type=F32)
        out_ref[0, rows, :] = _layer_norm(ALPHA * h + y, g_ref[...], b_ref[...])


def _merge(h, oa, ob, d, wg, wa, wb, pw, ps, wo, g, b, *, tm):
    B, T, D = h.shape
    row = lambda bb, i: (bb, i, 0)
    vec = pl.BlockSpec((1, D), lambda bb, i: (0, 0))
    wbytes = sum(_nbytes(w.shape, MM_DTYPE) for w in (wg, wa, wb, pw, wo))
    blocks = 2 * _nbytes((tm, D), F32) + 3 * _nbytes((tm, D), MM_DTYPE)
    return pl.pallas_call(
        functools.partial(_merge_kernel, tm=tm),
        grid=(B, T // tm),
        in_specs=[pl.BlockSpec((1, tm, D), row)] * 4
        + [_resident(wg.shape), _resident(wa.shape), _resident(wb.shape), _resident(pw.shape), vec,
           _resident(wo.shape), vec, vec],
        out_specs=pl.BlockSpec((1, tm, D), row),
        out_shape=jax.ShapeDtypeStruct((B, T, D), F32),
        compiler_params=pltpu.CompilerParams(
            dimension_semantics=("parallel", "parallel"),
            vmem_limit_bytes=_vmem_limit(blocks, wbytes, 12 * _nbytes((min(tm, SUB_ROWS), D), F32))),
        name="merge",
    )(h, oa, ob, d, wg, wa, wb, pw, ps, wo, g, b)


def _ffn_kernel(h_ref, wu_ref, wd_ref, g_ref, b_ref, out_ref, act_ref, *, tm):
    sub = min(tm, SUB_ROWS)
    for r in range(0, tm, sub):
        h = h_ref[0, r:r + sub, :]
        hb = h.astype(MM_DTYPE)
        for c in range(D_FF // FF_CHUNK):
            c0 = c * FF_CHUNK
            gate = jnp.dot(hb, wu_ref[:, c0:c0 + FF_CHUNK], preferred_element_type=F32)
            up = jnp.dot(hb, wu_ref[:, D_FF + c0:D_FF + c0 + FF_CHUNK], preferred_element_type=F32)
            act_ref[r:r + sub, c0:c0 + FF_CHUNK] = (jax.nn.silu(gate) * up).astype(act_ref.dtype)
        f = jnp.dot(act_ref[r:r + sub, :], wd_ref[...], preferred_element_type=F32)
        out_ref[0, r:r + sub, :] = _layer_norm(ALPHA * h + f, g_ref[...], b_ref[...])


def _ffn(h, wu, wd, g, b, *, tm):
    B, T, D = h.shape
    row = lambda bb, i: (bb, i, 0)
    vec = pl.BlockSpec((1, D), lambda bb, i: (0, 0))
    wbytes = _nbytes(wu.shape, MM_DTYPE) + _nbytes(wd.shape, MM_DTYPE)
    return pl.pallas_call(
        functools.partial(_ffn_kernel, tm=tm),
        grid=(B, T // tm),
        in_specs=[pl.BlockSpec((1, tm, D), row), _resident(wu.shape), _resident(wd.shape), vec, vec],
        out_specs=pl.BlockSpec((1, tm, D), row),
        out_shape=jax.ShapeDtypeStruct((B, T, D), F32),
        scratch_shapes=[pltpu.VMEM((tm, D_FF), MM_DTYPE)],
        compiler_params=pltpu.CompilerParams(
            dimension_semantics=("parallel", "parallel"),
            vmem_limit_bytes=_vmem_limit(2 * _nbytes((tm, D), F32), wbytes + _nbytes((tm, D_FF), MM_DTYPE),
                                         6 * _nbytes((tm, D), F32))),
        name="ffn",
    )(h, wu, wd, g, b)


def _rope_tables(row, col, pos):
    def cs(p, dim):
        inv = ROPE_THETA ** (-jnp.arange(0, dim, 2, dtype=F32) / dim)
        ang = p.astype(F32)[:, None] * inv[None, :]
        return jnp.cos(ang), jnp.sin(ang)

    cr, sr = cs(row, HEAD_DIM // 2)
    cc, sc = cs(col, HEAD_DIM // 2)
    cp, sp = cs(pos, HEAD_DIM)
    ca = jnp.concatenate([cr, cr, cc, cc], axis=1)
    sa = jnp.concatenate([-sr, sr, -sc, sc], axis=1)
    cb = jnp.concatenate([cp, cp], axis=1)
    sb = jnp.concatenate([-sp, sp], axis=1)
    return ca, sa, cb, sb


def _real_tables(S):
    t = jnp.arange(S, dtype=jnp.int32)
    return _rope_tables(t // GRID_W, t % GRID_W, t + N_META)


def _meta_tables():
    i = jnp.arange(META_ROWS, dtype=jnp.int32)
    return _rope_tables(-jnp.ones((META_ROWS,), jnp.int32), i, i)


def _layer(xr, xm, p, tabs_r, tabs_m, den_rows):
    B, S, _ = xr.shape
    qa, ka, va, qb, kb, vb, uc = _proj(xr, p["w_qkv"], p["qg"], p["kg"], tabs_r,
                                       tm=ROW_TILE, cka=ATTN_TK, ckb=WIN_BLK)
    qam, kam, vam, qbm, kbm, vbm, ucm = _proj(xm, p["w_qkv"], p["qg"], p["kg"], tabs_m,
                                              tm=META_ROWS, cka=META_ROWS, ckb=META_ROWS)
    oa = _attn_a(qa, ka, va, kam, vam, tq=ATTN_TQ, tk=ATTN_TK, den_rows=den_rows)
    oam = _attn_a(qam, ka, va, kam, vam, tq=META_ROWS, tk=ATTN_TK, den_rows=den_rows)
    ob = _attn_b(p["sink"], qb, kb, vb, kbm, vbm, tq=WIN_TQ)
    obm = _attn_b_meta(p["sink"], qbm, kb, vb, kbm, vbm)
    d = _pool(uc, ucm, tm=ROW_TILE)
    dm = _pool_meta(ucm, uc, seq_len=N_META + S)
    merge_w = (p["wg"], p["wa"], p["wb"], p["pw"], p["ps"], p["wo"], p["ln1_g"], p["ln1_b"])
    hr = _merge(xr, oa, ob, d, *merge_w, tm=DENSE_TILE)
    hm = _merge(xm, oam, obm, dm, *merge_w, tm=META_ROWS)
    ffn_w = (p["wu"], p["wd"], p["ln2_g"], p["ln2_b"])
    return _ffn(hr, *ffn_w, tm=DENSE_TILE), _ffn(hm, *ffn_w, tm=META_ROWS)


def _trunk(x, meta_tokens, layers, tabs_r, tabs_m):
    B, S, D = x.shape
    xm = jnp.zeros((B, META_ROWS, D), x.dtype).at[:, :N_META].set(meta_tokens.astype(x.dtype)[None])
    xr = x
    for li, p in enumerate(layers):
        xr, xm = _layer(xr, xm, p, tabs_r, tabs_m, den_rows=DEN_ROWS_BY_LAYER[li])
    return xr


def _layer_params(w_in, q_norm_g, k_norm_g, sink_logit, pool_w, pool_scale,
                  w_branch_a, w_branch_b, w_out, ln1_g, ln1_b, w_up, w_down, ln2_g, ln2_b):
    mm = lambda w: w.astype(MM_DTYPE)
    vec = lambda v: v.astype(F32).reshape(1, -1)
    layers = []
    for l in range(w_in.shape[0]):
        layers.append(dict(
            w_qkv=mm(w_in[l, :, :QKV_WIDTH]), wg=mm(w_in[l, :, QKV_WIDTH:]),
            qg=vec(q_norm_g[l]) * QK_SCALE, kg=vec(k_norm_g[l]), sink=sink_logit[l].astype(F32),
            pw=mm(pool_w[l]), ps=vec(pool_scale[l]),
            wa=mm(w_branch_a[l]), wb=mm(w_branch_b[l]), wo=mm(w_out[l]),
            ln1_g=vec(ln1_g[l]), ln1_b=vec(ln1_b[l]),
            wu=mm(w_up[l]), wd=mm(w_down[l]),
            ln2_g=vec(ln2_g[l]), ln2_b=vec(ln2_b[l]),
        ))
    return layers


def kernel(x_prompt, x_sample, meta_tokens, w_in, q_norm_g, k_norm_g, sink_logit, pool_w, pool_scale,
           w_branch_a, w_branch_b, w_out, ln1_g, ln1_b, w_up, w_down, ln2_g, ln2_b):
    layers = _layer_params(w_in, q_norm_g, k_norm_g, sink_logit, pool_w, pool_scale,
                           w_branch_a, w_branch_b, w_out, ln1_g, ln1_b, w_up, w_down, ln2_g, ln2_b)
    tabs_m = _meta_tables()
    tabs_s = _real_tables(x_sample.shape[1])
    tabs_p = tuple(t[:x_prompt.shape[1]] for t in tabs_s)
    y_prompt = _trunk(x_prompt, meta_tokens, layers, tabs_p, tabs_m)
    y_sample = _trunk(x_sample, meta_tokens, layers, tabs_s, tabs_m)
    return (y_prompt, y_sample)
```

```python
import functools

import jax
import jax.numpy as jnp
from jax import lax
from jax.experimental import pallas as pl
from jax.experimental.pallas import tpu as pltpu

D_MODEL = 1024
HEAD_DIM = 128
N_HEADS = 8
N_KV = 2
GROUP = N_HEADS // N_KV
WINDOW = 128
N_META = 16
META_ROWS = 128
GRID_W = 64
ROPE_THETA = 10000.0
POOL_WINDOWS = (2, 4, 8, 16)
POOL_DIM = D_MODEL // len(POOL_WINDOWS)
POOL_HALO = 8
D_FF = 2816
DEPTH = 2
ALPHA = (2 * DEPTH) ** 0.25
NEG = -1e30
QKV_WIDTH = 4096
LOG2E = 1.4426950408889634
QK_SCALE = HEAD_DIM ** -0.5 * LOG2E

MM_DTYPE = jnp.bfloat16
F32 = jnp.float32

V7X_VMEM_BYTES = 64 * 1024 * 1024
VMEM_CAP = V7X_VMEM_BYTES - 6 * 1024 * 1024

ROW_TILE = 512
DENSE_TILE = 1024
ATTN_TQ = 512
ATTN_TK = 512
WIN_TQ = 1024
WIN_SUB = 256
WIN_BLK = 128
FF_CHUNK = 256
SUB_ROWS = 256
ATTN_OPTS_BY_LAYER = (dict(weight_rows=16, lane_pad=128), dict(weight_rows=0, lane_pad=0))

_NT = (((1,), (1,)), ((), ()))


def _vmem_limit(block_bytes, scratch_bytes=0, temp_bytes=0):
    est = 2 * block_bytes + scratch_bytes + temp_bytes + (8 << 20)
    return int(min(max(est, 16 << 20), VMEM_CAP))


def _nbytes(shape, dtype):
    n = 1
    for s in shape:
        n *= s
    return n * jnp.dtype(dtype).itemsize


def _resident(shape):
    zeros = (0,) * len(shape)
    return pl.BlockSpec(shape, lambda *_: zeros, pipeline_mode=pl.Buffered(1))


def _layer_norm(x, g, b):
    mu = jnp.mean(x, axis=-1, keepdims=True)
    xc = x - mu
    var = jnp.mean(xc * xc, axis=-1, keepdims=True)
    return xc * lax.rsqrt(var + 1e-5) * g + b


def _proj_kernel(x_ref, w_ref, qg_ref, kg_ref, ca_ref, sa_ref, cb_ref, sb_ref,
                 qa_ref, ka_ref, va_ref, qb_ref, kb_ref, vb_ref, uc_ref, *, tm, cka, ckb):
    sub = min(tm, SUB_ROWS)
    qg = qg_ref[...]
    kg = kg_ref[...]
    lane = lax.broadcasted_iota(jnp.int32, (sub, HEAD_DIM), 1)
    low_quarter = (lane % 64) < 32

    def rms(x, g):
        return x * lax.rsqrt(jnp.mean(x * x, axis=-1, keepdims=True) + 1e-6) * g

    def head(u, j):
        return u[:, j * HEAD_DIM:(j + 1) * HEAD_DIM]

    for r in range(0, tm, sub):
        rows = slice(r, r + sub)
        xb = x_ref[0, rows, :].astype(MM_DTYPE)
        ca = ca_ref[rows, :]
        sa = sa_ref[rows, :]
        cb = cb_ref[rows, :]
        sb = sb_ref[rows, :]

        def rope_a(x):
            partner = jnp.where(low_quarter, pltpu.roll(x, 96, 1), pltpu.roll(x, 32, 1))
            return x * ca + partner * sa

        def rope_b(x):
            return x * cb + pltpu.roll(x, 64, 1) * sb

        def cols(c0, n):
            return jnp.dot(xb, w_ref[:, c0:c0 + n], preferred_element_type=F32)

        def store_vt(v_ref, v, ck):
            step = min(sub, ck)
            for o in range(0, sub, step):
                j, lane0 = (r + o) // ck, (r + o) % ck
                v_ref[0, j, :, lane0:lane0 + step] = v[o:o + step, :].T.astype(v_ref.dtype)

        for blk in range(2):
            u = cols(blk * 512, 512)
            for j in range(4):
                h = blk * 4 + j
                qa_ref[0, rows, h * HEAD_DIM:(h + 1) * HEAD_DIM] = rope_a(rms(head(u, j), qg)).astype(qa_ref.dtype)
        u = cols(1024, 512)
        for j in range(2):
            ka_ref[0, rows, j * HEAD_DIM:(j + 1) * HEAD_DIM] = rope_a(rms(head(u, j), kg)).astype(ka_ref.dtype)
        store_vt(va_ref, u[:, 256:512], cka)
        for blk in range(2):
            u = cols(1536 + blk * 512, 512)
            for j in range(4):
                h = blk * 4 + j
                qb_ref[0, rows, h * HEAD_DIM:(h + 1) * HEAD_DIM] = (
                    rope_b(head(u, j)) * QK_SCALE).astype(qb_ref.dtype)
        u = cols(2560, 512)
        for j in range(2):
            kb_ref[0, rows, j * HEAD_DIM:(j + 1) * HEAD_DIM] = rope_b(head(u, j)).astype(kb_ref.dtype)
        store_vt(vb_ref, u[:, 256:512], ckb)
        for blk in range(2):
            uc_ref[0, rows, blk * 512:(blk + 1) * 512] = cols(3072 + blk * 512, 512)


def _proj(x, w_qkv, qg, kg, tabs, *, tm, cka, ckb):
    B, T, D = x.shape
    nt = T // tm
    ca, sa, cb, sb = tabs
    row = lambda b, i: (b, i, 0)
    tab = pl.BlockSpec((tm, HEAD_DIM), lambda b, i: (i, 0))
    small = pl.BlockSpec((1, HEAD_DIM), lambda b, i: (0, 0))
    out_shape = (
        jax.ShapeDtypeStruct((B, T, 1024), MM_DTYPE),
        jax.ShapeDtypeStruct((B, T, 256), MM_DTYPE),
        jax.ShapeDtypeStruct((B, T // cka, 256, cka), MM_DTYPE),
        jax.ShapeDtypeStruct((B, T, 1024), MM_DTYPE),
        jax.ShapeDtypeStruct((B, T, 256), MM_DTYPE),
        jax.ShapeDtypeStruct((B, T // ckb, 256, ckb), MM_DTYPE),
        jax.ShapeDtypeStruct((B, T, 1024), F32),
    )
    out_specs = (
        pl.BlockSpec((1, tm, 1024), row),
        pl.BlockSpec((1, tm, 256), row),
        pl.BlockSpec((1, tm // cka, 256, cka), lambda b, i: (b, i, 0, 0)),
        pl.BlockSpec((1, tm, 1024), row),
        pl.BlockSpec((1, tm, 256), row),
        pl.BlockSpec((1, tm // ckb, 256, ckb), lambda b, i: (b, i, 0, 0)),
        pl.BlockSpec((1, tm, 1024), row),
    )
    blocks = (_nbytes((tm, D), F32) + 4 * _nbytes((tm, HEAD_DIM), F32)
              + _nbytes((tm, 3072), MM_DTYPE) + _nbytes((tm, 1024), F32))
    return pl.pallas_call(
        functools.partial(_proj_kernel, tm=tm, cka=cka, ckb=ckb),
        grid=(B, nt),
        in_specs=[pl.BlockSpec((1, tm, D), row), _resident(w_qkv.shape), small, small, tab, tab, tab, tab],
        out_specs=out_specs,
        out_shape=out_shape,
        compiler_params=pltpu.CompilerParams(
            dimension_semantics=("parallel", "parallel"),
            vmem_limit_bytes=_vmem_limit(blocks, _nbytes(w_qkv.shape, MM_DTYPE), 8 * _nbytes((tm, 512), F32))),
        name="proj",
    )(x, w_qkv, qg, kg, ca, sa, cb, sb)


def _attn_a_kernel(q_ref, k_ref, vt_ref, km_ref, vmt_ref, o_ref,
                   acc_ref, m_ref, l_ref, s_ref, p_ref, ms_ref, as_ref, ac_ref, *, tq, tk, n_chunks, weight_rows):
    def q_head(h):
        return q_ref[0, :, h * HEAD_DIM:(h + 1) * HEAD_DIM]

    def scores(c, slot, h):
        kc = k_ref[0, pl.ds(pl.multiple_of(c * tk, tk), tk), :]
        s = lax.dot_general(kc, q_head(h), _NT, preferred_element_type=F32)
        s_ref[slot, h, :, 0:tq] = s
        m_old = m_ref[h]
        m_new = jnp.maximum(m_old, jnp.max(s, axis=0, keepdims=True))
        m_ref[h] = m_new
        ms_ref[slot, h] = m_new
        as_ref[slot, h] = jnp.exp2(m_old - m_new)

    def weights(slot, h):
        a = as_ref[slot, h]
        m = ms_ref[slot, h]
        ac_ref[slot, h] = a
        if not weight_rows:
            p = jnp.exp2(s_ref[slot, h, :, 0:tq] - m)
            p_ref[slot, h, :, 0:tq] = p.astype(p_ref.dtype)
            l_ref[h] = a * l_ref[h] + jnp.sum(p, axis=0, keepdims=True)
            return
        part = jnp.zeros((8, tq), F32)
        for r in range(0, tk, weight_rows):
            p = jnp.exp2(s_ref[slot, h, r:r + weight_rows, 0:tq] - m)
            p_ref[slot, h, r:r + weight_rows, 0:tq] = p.astype(p_ref.dtype)
            for i in range(weight_rows // 8):
                part = part + p[i * 8:(i + 1) * 8, :]
        l_ref[h] = a * l_ref[h] + jnp.sum(part, axis=0, keepdims=True)

    def values(c, slot, h):
        pv = jnp.dot(vt_ref[0, c], p_ref[slot, h, :, 0:tq], preferred_element_type=F32)
        acc_ref[h, :, 0:tq] = ac_ref[slot, h] * acc_ref[h, :, 0:tq] + pv

    key_row = lax.broadcasted_iota(jnp.int32, (META_ROWS, tq), 0)
    s_meta = [lax.dot_general(km_ref[0], q_head(h), _NT, preferred_element_type=F32) for h in range(GROUP)]
    p_meta = []
    for h in range(GROUP):
        s = jnp.where(key_row < N_META, s_meta[h], NEG)
        m = jnp.max(s, axis=0, keepdims=True)
        p = jnp.exp2(s - m)
        m_ref[h] = m
        l_ref[h] = jnp.sum(p, axis=0, keepdims=True)
        p_meta.append(p.astype(MM_DTYPE))
    for h in range(GROUP):
        acc_ref[h, :, 0:tq] = jnp.dot(vmt_ref[0, 0], p_meta[h], preferred_element_type=F32)
    for h in range(GROUP):
        scores(0, 0, h)
    for h in range(GROUP):
        scores(1, 1, h)
        weights(0, h)

    def body(j, carry):
        c1 = 2 * j + 1
        for h in range(GROUP):
            scores(c1 + 1, 0, h)
            values(c1 - 1, 0, h)
            weights(1, h)
        for h in range(GROUP):
            scores(c1 + 2, 1, h)
            values(c1, 1, h)
            weights(0, h)
        return carry

    lax.fori_loop(0, n_chunks // 2 - 1, body, 0)
    for h in range(GROUP):
        values(n_chunks - 2, 0, h)
        weights(1, h)
    for h in range(GROUP):
        values(n_chunks - 1, 1, h)
    for h in range(GROUP):
        o = acc_ref[h, :, 0:tq] / l_ref[h]
        o_ref[0, :, h * HEAD_DIM:(h + 1) * HEAD_DIM] = o.T.astype(o_ref.dtype)


def _attn_a(q, k, vt, km, vmt, *, tq, tk, weight_rows, lane_pad):
    B, Tq, _ = q.shape
    S = k.shape[1]
    n_chunks = S // tk
    assert n_chunks % 2 == 0
    blocks = (_nbytes((tq, 512), MM_DTYPE) * 2 + 2 * _nbytes((S, HEAD_DIM), MM_DTYPE)
              + 2 * _nbytes((META_ROWS, HEAD_DIM), MM_DTYPE))
    stat = pltpu.VMEM((2, GROUP, 1, tq), F32)
    wide = tq + lane_pad
    scratch = (GROUP * (_nbytes((HEAD_DIM, wide), F32) + 8 * _nbytes((8, tq), F32))
               + 2 * GROUP * (_nbytes((tk, wide), F32) + _nbytes((tk, wide), MM_DTYPE)))
    return pl.pallas_call(
        functools.partial(_attn_a_kernel, tq=tq, tk=tk, n_chunks=n_chunks, weight_rows=weight_rows),
        grid=(B, N_KV, Tq // tq),
        in_specs=[
            pl.BlockSpec((1, tq, GROUP * HEAD_DIM), lambda b, g, i: (b, i, g)),
            pl.BlockSpec((1, S, HEAD_DIM), lambda b, g, i: (b, 0, g)),
            pl.BlockSpec((1, n_chunks, HEAD_DIM, tk), lambda b, g, i: (b, 0, g, 0)),
            pl.BlockSpec((1, META_ROWS, HEAD_DIM), lambda b, g, i: (b, 0, g)),
            pl.BlockSpec((1, 1, HEAD_DIM, META_ROWS), lambda b, g, i: (b, 0, g, 0)),
        ],
        out_specs=pl.BlockSpec((1, tq, GROUP * HEAD_DIM), lambda b, g, i: (b, i, g)),
        out_shape=jax.ShapeDtypeStruct((B, Tq, N_HEADS * HEAD_DIM), MM_DTYPE),
        scratch_shapes=[
            pltpu.VMEM((GROUP, HEAD_DIM, wide), F32),
            pltpu.VMEM((GROUP, 1, tq), F32),
            pltpu.VMEM((GROUP, 1, tq), F32),
            pltpu.VMEM((2, GROUP, tk, wide), F32),
            pltpu.VMEM((2, GROUP, tk, wide), MM_DTYPE),
            stat, stat, stat,
        ],
        compiler_params=pltpu.CompilerParams(
            dimension_semantics=("parallel", "parallel", "parallel"),
            vmem_limit_bytes=_vmem_limit(blocks, scratch, 6 * _nbytes((tk, tq), F32))),
        name="attn_global",
    )(q, k, vt, km, vmt)


def _win_attend(sink_ref, q_ref, o_ref, problems):
    g = pl.program_id(1)
    scores = [[lax.dot_general(k_all, q_ref[0, col0:col0 + width, h * HEAD_DIM:(h + 1) * HEAD_DIM], _NT,
                               preferred_element_type=F32) for h in range(GROUP)]
              for col0, width, k_all, _, _ in problems]
    for (col0, width, _, vt_all, mask), s_heads in zip(problems, scores):
        for h in range(GROUP):
            s = jnp.where(mask, s_heads[h], NEG)
            sink = sink_ref[g * GROUP + h] * LOG2E
            m = jnp.maximum(jnp.max(s, axis=0, keepdims=True), sink)
            p = jnp.exp2(s - m)
            den = jnp.sum(p, axis=0, keepdims=True) + jnp.exp2(sink - m)
            o = jnp.dot(vt_all, p.astype(MM_DTYPE), preferred_element_type=F32) / den
            o_ref[0, col0:col0 + width, h * HEAD_DIM:(h + 1) * HEAD_DIM] = o.T.astype(o_ref.dtype)


def _attn_b_kernel(sink_ref, q_ref, kp_ref, kc_ref, kn_ref, vp_ref, vc_ref, vn_ref, km_ref, vmt_ref, o_ref,
                   *, tq, n_tiles):
    i = pl.program_id(2)
    nb = tq // WIN_BLK
    n_sub = tq // WIN_SUB
    k_real = jnp.concatenate([kp_ref[0], kc_ref[0], kn_ref[0]], axis=0)
    vt_real = jnp.concatenate([vp_ref[0, 0]] + [vc_ref[0, j] for j in range(nb)] + [vn_ref[0, 0]], axis=1)
    n_keys = META_ROWS + 2 * WIN_BLK + WIN_SUB
    r = lax.broadcasted_iota(jnp.int32, (n_keys, WIN_SUB), 0)
    c = lax.broadcasted_iota(jnp.int32, (n_keys, WIN_SUB), 1)
    rel = r - c - (META_ROWS + WIN_BLK)
    in_win = (rel >= -WINDOW) & (rel <= WINDOW) & (r >= META_ROWS)
    problems = []
    for j in range(n_sub):
        span = slice(j * WIN_SUB, j * WIN_SUB + WIN_SUB + 2 * WIN_BLK)
        k_all = jnp.concatenate([km_ref[0], k_real[span, :]], axis=0)
        vt_all = jnp.concatenate([vmt_ref[0, 0], vt_real[:, span]], axis=1)
        valid = in_win
        if j == 0:
            valid = valid & (r >= META_ROWS + jnp.where(i == 0, WIN_BLK, 0))
        if j == n_sub - 1:
            valid = valid & (r < n_keys - jnp.where(i == n_tiles - 1, WIN_BLK, 0))
        problems.append((j * WIN_SUB, WIN_SUB, k_all, vt_all, (r < N_META) | valid))
    _win_attend(sink_ref, q_ref, o_ref, problems)


def _attn_b_meta_kernel(sink_ref, q_ref, k0_ref, v0_ref, km_ref, vmt_ref, o_ref):
    k_all = jnp.concatenate([km_ref[0], k0_ref[0]], axis=0)
    vt_all = jnp.concatenate([vmt_ref[0, 0], v0_ref[0, 0]], axis=1)
    r = lax.broadcasted_iota(jnp.int32, (META_ROWS + WIN_BLK, META_ROWS), 0)
    c = lax.broadcasted_iota(jnp.int32, (META_ROWS + WIN_BLK, META_ROWS), 1)
    mask = (r < N_META) | ((r >= META_ROWS) & (r - META_ROWS + N_META - c <= WINDOW))
    _win_attend(sink_ref, q_ref, o_ref, [(0, META_ROWS, k_all, vt_all, mask)])


def _attn_b(sink, q, k, vt, km, vmt, *, tq):
    B, S, _ = q.shape
    n_tiles = S // tq
    nb = tq // WIN_BLK
    last_blk = S // WIN_BLK - 1
    kblk = lambda shape, f: pl.BlockSpec(shape, f)
    n_keys = META_ROWS + 2 * WIN_BLK + tq
    blocks = 2 * _nbytes((tq, 512), MM_DTYPE) + 2 * _nbytes((n_keys, HEAD_DIM), MM_DTYPE)
    grid_spec = pltpu.PrefetchScalarGridSpec(
        num_scalar_prefetch=1,
        grid=(B, N_KV, n_tiles),
        in_specs=[
            pl.BlockSpec((1, tq, GROUP * HEAD_DIM), lambda b, g, i, s: (b, i, g)),
            kblk((1, WIN_BLK, HEAD_DIM), lambda b, g, i, s: (b, jnp.maximum(i * nb - 1, 0), g)),
            kblk((1, tq, HEAD_DIM), lambda b, g, i, s: (b, i, g)),
            kblk((1, WIN_BLK, HEAD_DIM), lambda b, g, i, s: (b, jnp.minimum((i + 1) * nb, last_blk), g)),
            kblk((1, 1, HEAD_DIM, WIN_BLK), lambda b, g, i, s: (b, jnp.maximum(i * nb - 1, 0), g, 0)),
            kblk((1, nb, HEAD_DIM, WIN_BLK), lambda b, g, i, s: (b, i, g, 0)),
            kblk((1, 1, HEAD_DIM, WIN_BLK), lambda b, g, i, s: (b, jnp.minimum((i + 1) * nb, last_blk), g, 0)),
            kblk((1, META_ROWS, HEAD_DIM), lambda b, g, i, s: (b, 0, g)),
            kblk((1, 1, HEAD_DIM, META_ROWS), lambda b, g, i, s: (b, 0, g, 0)),
        ],
        out_specs=pl.BlockSpec((1, tq, GROUP * HEAD_DIM), lambda b, g, i, s: (b, i, g)),
    )
    return pl.pallas_call(
        functools.partial(_attn_b_kernel, tq=tq, n_tiles=n_tiles),
        grid_spec=grid_spec,
        out_shape=jax.ShapeDtypeStruct((B, S, N_HEADS * HEAD_DIM), MM_DTYPE),
        compiler_params=pltpu.CompilerParams(
            dimension_semantics=("parallel", "parallel", "parallel"),
            vmem_limit_bytes=_vmem_limit(blocks, 0, 8 * _nbytes((n_keys, tq), F32))),
        name="attn_window",
    )(sink, q, k, k, k, vt, vt, vt, km, vmt)


def _attn_b_meta(sink, qm, k, vt, km, vmt):
    B = qm.shape[0]
    blocks = 2 * _nbytes((META_ROWS, 512), MM_DTYPE) + 4 * _nbytes((META_ROWS, HEAD_DIM), MM_DTYPE)
    grid_spec = pltpu.PrefetchScalarGridSpec(
        num_scalar_prefetch=1,
        grid=(B, N_KV),
        in_specs=[
            pl.BlockSpec((1, META_ROWS, GROUP * HEAD_DIM), lambda b, g, s: (b, 0, g)),
            pl.BlockSpec((1, WIN_BLK, HEAD_DIM), lambda b, g, s: (b, 0, g)),
            pl.BlockSpec((1, 1, HEAD_DIM, WIN_BLK), lambda b, g, s: (b, 0, g, 0)),
            pl.BlockSpec((1, META_ROWS, HEAD_DIM), lambda b, g, s: (b, 0, g)),
            pl.BlockSpec((1, 1, HEAD_DIM, META_ROWS), lambda b, g, s: (b, 0, g, 0)),
        ],
        out_specs=pl.BlockSpec((1, META_ROWS, GROUP * HEAD_DIM), lambda b, g, s: (b, 0, g)),
    )
    return pl.pallas_call(
        _attn_b_meta_kernel,
        grid_spec=grid_spec,
        out_shape=jax.ShapeDtypeStruct((B, META_ROWS, N_HEADS * HEAD_DIM), MM_DTYPE),
        compiler_params=pltpu.CompilerParams(
            dimension_semantics=("parallel", "parallel"),
            vmem_limit_bytes=_vmem_limit(blocks, 0, 8 * _nbytes((2 * META_ROWS, META_ROWS), F32))),
        name="attn_window_meta",
    )(sink, qm, k, vt, km, vmt)


def _pool_rows(ext_ref, centre_ref, d_ref, pos, seq_len, tm):
    n = tm + 2 * POOL_HALO

    def ahead(x, k):
        return pltpu.roll(x, n - k, 0) if k % n else x

    for gi, win in enumerate(POOL_WINDOWS):
        c0 = gi * POOL_DIM
        s = ext_ref[:, c0:c0 + POOL_DIM]
        span = 1
        while span < win:
            s = s + ahead(s, span)
            span *= 2
        acc = ahead(s, POOL_HALO - win // 2)[0:tm]
        lo = jnp.maximum(pos - win // 2, 0)
        hi = jnp.minimum(pos - win // 2 + win, seq_len)
        mean = acc / (hi - lo).astype(F32)
        d_ref[0, :, c0:c0 + POOL_DIM] = (mean - centre_ref[0, :, c0:c0 + POOL_DIM]).astype(d_ref.dtype)


def _pool_kernel(prev_ref, main_ref, next_ref, meta_ref, d_ref, ext_ref, *, tm, n_tiles, seq_len):
    i = pl.program_id(1)
    ext_ref[0:POOL_HALO, :] = jnp.where(i == 0, meta_ref[0], prev_ref[0])
    ext_ref[POOL_HALO:POOL_HALO + tm, :] = main_ref[0]
    ext_ref[POOL_HALO + tm:2 * POOL_HALO + tm, :] = jnp.where(i == n_tiles - 1, 0.0, next_ref[0])
    pos = N_META + i * tm + lax.broadcasted_iota(jnp.int32, (tm, 1), 0)
    _pool_rows(ext_ref, main_ref, d_ref, pos, seq_len, tm)


def _pool_meta_kernel(meta_ref, first_ref, d_ref, ext_ref, *, seq_len):
    ext_ref[...] = jnp.zeros(ext_ref.shape, F32)
    ext_ref[POOL_HALO:POOL_HALO + N_META, :] = meta_ref[0, 0:N_META, :]
    ext_ref[POOL_HALO + N_META:2 * POOL_HALO + N_META, :] = first_ref[0]
    pos = lax.broadcasted_iota(jnp.int32, (META_ROWS, 1), 0)
    _pool_rows(ext_ref, meta_ref, d_ref, pos, seq_len, META_ROWS)


def _pool(uc, ucm, *, tm):
    B, S, C = uc.shape
    n_tiles = S // tm
    per = tm // POOL_HALO
    last = S // POOL_HALO - 1
    halo = lambda f: pl.BlockSpec((1, POOL_HALO, C), f)
    blocks = 2 * _nbytes((tm, C), F32) + _nbytes((tm, C), MM_DTYPE)
    return pl.pallas_call(
        functools.partial(_pool_kernel, tm=tm, n_tiles=n_tiles, seq_len=N_META + S),
        grid=(B, n_tiles),
        in_specs=[
            halo(lambda b, i: (b, jnp.maximum(i * per - 1, 0), 0)),
            pl.BlockSpec((1, tm, C), lambda b, i: (b, i, 0)),
            halo(lambda b, i: (b, jnp.minimum((i + 1) * per, last), 0)),
            halo(lambda b, i: (b, N_META // POOL_HALO - 1, 0)),
        ],
        out_specs=pl.BlockSpec((1, tm, C), lambda b, i: (b, i, 0)),
        out_shape=jax.ShapeDtypeStruct((B, S, C), MM_DTYPE),
        scratch_shapes=[pltpu.VMEM((tm + 2 * POOL_HALO, C), F32)],
        compiler_params=pltpu.CompilerParams(
            dimension_semantics=("parallel", "parallel"),
            vmem_limit_bytes=_vmem_limit(blocks, _nbytes((tm + 16, C), F32), 4 * _nbytes((tm, POOL_DIM), F32))),
        name="pool",
    )(uc, uc, uc, ucm)


def _pool_meta(ucm, uc, *, seq_len):
    B, _, C = ucm.shape
    return pl.pallas_call(
        functools.partial(_pool_meta_kernel, seq_len=seq_len),
        grid=(B,),
        in_specs=[
            pl.BlockSpec((1, META_ROWS, C), lambda b: (b, 0, 0)),
            pl.BlockSpec((1, POOL_HALO, C), lambda b: (b, 0, 0)),
        ],
        out_specs=pl.BlockSpec((1, META_ROWS, C), lambda b: (b, 0, 0)),
        out_shape=jax.ShapeDtypeStruct((B, META_ROWS, C), MM_DTYPE),
        scratch_shapes=[pltpu.VMEM((META_ROWS + 2 * POOL_HALO, C), F32)],
        compiler_params=pltpu.CompilerParams(
            dimension_semantics=("parallel",),
            vmem_limit_bytes=_vmem_limit(3 * _nbytes((META_ROWS, C), F32))),
        name="pool_meta",
    )(ucm, uc)


def _merge_kernel(h_ref, oa_ref, ob_ref, d_ref, wg_ref, wa_ref, wb_ref, pw_ref, ps_ref, wo_ref, g_ref, b_ref,
                  out_ref, *, tm):
    def gate(hb, j):
        return jax.nn.sigmoid(jnp.dot(hb, wg_ref[:, j * D_MODEL:(j + 1) * D_MODEL], preferred_element_type=F32))

    sub = min(tm, SUB_ROWS)
    for r in range(0, tm, sub):
        rows = slice(r, r + sub)
        h = h_ref[0, rows, :]
        hb = h.astype(MM_DTYPE)
        ya = jnp.dot(oa_ref[0, rows, :], wa_ref[...], preferred_element_type=F32)
        yb = jnp.dot(ob_ref[0, rows, :], wb_ref[...], preferred_element_type=F32)
        yc = jnp.concatenate(
            [jnp.dot(d_ref[0, rows, gi * POOL_DIM:(gi + 1) * POOL_DIM], pw_ref[gi], preferred_element_type=F32)
             for gi in range(len(POOL_WINDOWS))], axis=1) * ps_ref[...]
        merged = gate(hb, 0) * ya + gate(hb, 1) * yb + gate(hb, 2) * yc
        y = jnp.dot(merged.astype(MM_DTYPE), wo_ref[...], preferred_element_type=F32)
        out_ref[0, rows, :] = _layer_norm(ALPHA * h + y, g_ref[...], b_ref[...])


def _merge(h, oa, ob, d, wg, wa, wb, pw, ps, wo, g, b, *, tm):
    B, T, D = h.shape
    row = lambda bb, i: (bb, i, 0)
    vec = pl.BlockSpec((1, D), lambda bb, i: (0, 0))
    wbytes = sum(_nbytes(w.shape, MM_DTYPE) for w in (wg, wa, wb, pw, wo))
    blocks = 2 * _nbytes((tm, D), F32) + 3 * _nbytes((tm, D), MM_DTYPE)
    return pl.pallas_call(
        functools.partial(_merge_kernel, tm=tm),
        grid=(B, T // tm),
        in_specs=[pl.BlockSpec((1, tm, D), row)] * 4
        + [_resident(wg.shape), _resident(wa.shape), _resident(wb.shape), _resident(pw.shape), vec,
           _resident(wo.shape), vec, vec],
        out_specs=pl.BlockSpec((1, tm, D), row),
        out_shape=jax.ShapeDtypeStruct((B, T, D), F32),
        compiler_params=pltpu.CompilerParams(
            dimension_semantics=("parallel", "parallel"),
            vmem_limit_bytes=_vmem_limit(blocks, wbytes, 12 * _nbytes((min(tm, SUB_ROWS), D), F32))),
        name="merge",
    )(h, oa, ob, d, wg, wa, wb, pw, ps, wo, g, b)


def _ffn_kernel(h_ref, wu_ref, wd_ref, g_ref, b_ref, out_ref, act_ref, *, tm):
    sub = min(tm, SUB_ROWS)
    for r in range(0, tm, sub):
        h = h_ref[0, r:r + sub, :]
        hb = h.astype(MM_DTYPE)
        for c in range(D_FF // FF_CHUNK):
            c0 = c * FF_CHUNK
            gate = jnp.dot(hb, wu_ref[:, c0:c0 + FF_CHUNK], preferred_element_type=F32)
            up = jnp.dot(hb, wu_ref[:, D_FF + c0:D_FF + c0 + FF_CHUNK], preferred_element_type=F32)
            act_ref[r:r + sub, c0:c0 + FF_CHUNK] = (jax.nn.silu(gate) * up).astype(act_ref.dtype)
        f = jnp.dot(act_ref[r:r + sub, :], wd_ref[...], preferred_element_type=F32)
        out_ref[0, r:r + sub, :] = _layer_norm(ALPHA * h + f, g_ref[...], b_ref[...])


def _ffn(h, wu, wd, g, b, *, tm):
    B, T, D = h.shape
    row = lambda bb, i: (bb, i, 0)
    vec = pl.BlockSpec((1, D), lambda bb, i: (0, 0))
    wbytes = _nbytes(wu.shape, MM_DTYPE) + _nbytes(wd.shape, MM_DTYPE)
    return pl.pallas_call(
        functools.partial(_ffn_kernel, tm=tm),
        grid=(B, T // tm),
        in_specs=[pl.BlockSpec((1, tm, D), row), _resident(wu.shape), _resident(wd.shape), vec, vec],
        out_specs=pl.BlockSpec((1, tm, D), row),
        out_shape=jax.ShapeDtypeStruct((B, T, D), F32),
        scratch_shapes=[pltpu.VMEM((tm, D_FF), MM_DTYPE)],
        compiler_params=pltpu.CompilerParams(
            dimension_semantics=("parallel", "parallel"),
            vmem_limit_bytes=_vmem_limit(2 * _nbytes((tm, D), F32), wbytes + _nbytes((tm, D_FF), MM_DTYPE),
                                         6 * _nbytes((tm, D), F32))),
        name="ffn",
    )(h, wu, wd, g, b)


def _rope_tables(row, col, pos):
    def cs(p, dim):
        inv = ROPE_THETA ** (-jnp.arange(0, dim, 2, dtype=F32) / dim)
        ang = p.astype(F32)[:, None] * inv[None, :]
        return jnp.cos(ang), jnp.sin(ang)

    cr, sr = cs(row, HEAD_DIM // 2)
    cc, sc = cs(col, HEAD_DIM // 2)
    cp, sp = cs(pos, HEAD_DIM)
    ca = jnp.concatenate([cr, cr, cc, cc], axis=1)
    sa = jnp.concatenate([-sr, sr, -sc, sc], axis=1)
    cb = jnp.concatenate([cp, cp], axis=1)
    sb = jnp.concatenate([-sp, sp], axis=1)
    return ca, sa, cb, sb


def _real_tables(S):
    t = jnp.arange(S, dtype=jnp.int32)
    return _rope_tables(t // GRID_W, t % GRID_W, t + N_META)


def _meta_tables():
    i = jnp.arange(META_ROWS, dtype=jnp.int32)
    return _rope_tables(-jnp.ones((META_ROWS,), jnp.int32), i, i)


def _layer(xr, xm, p, tabs_r, tabs_m, attn_opts):
    B, S, _ = xr.shape
    qa, ka, va, qb, kb, vb, uc = _proj(xr, p["w_qkv"], p["qg"], p["kg"], tabs_r,
                                       tm=ROW_TILE, cka=ATTN_TK, ckb=WIN_BLK)
    qam, kam, vam, qbm, kbm, vbm, ucm = _proj(xm, p["w_qkv"], p["qg"], p["kg"], tabs_m,
                                              tm=META_ROWS, cka=META_ROWS, ckb=META_ROWS)
    oa = _attn_a(qa, ka, va, kam, vam, tq=ATTN_TQ, tk=ATTN_TK, **attn_opts)
    oam = _attn_a(qam, ka, va, kam, vam, tq=META_ROWS, tk=ATTN_TK, **attn_opts)
    ob = _attn_b(p["sink"], qb, kb, vb, kbm, vbm, tq=WIN_TQ)
    obm = _attn_b_meta(p["sink"], qbm, kb, vb, kbm, vbm)
    d = _pool(uc, ucm, tm=ROW_TILE)
    dm = _pool_meta(ucm, uc, seq_len=N_META + S)
    merge_w = (p["wg"], p["wa"], p["wb"], p["pw"], p["ps"], p["wo"], p["ln1_g"], p["ln1_b"])
    hr = _merge(xr, oa, ob, d, *merge_w, tm=DENSE_TILE)
    hm = _merge(xm, oam, obm, dm, *merge_w, tm=META_ROWS)
    ffn_w = (p["wu"], p["wd"], p["ln2_g"], p["ln2_b"])
    return _ffn(hr, *ffn_w, tm=DENSE_TILE), _ffn(hm, *ffn_w, tm=META_ROWS)


def _trunk(x, meta_tokens, layers, tabs_r, tabs_m):
    B, S, D = x.shape
    xm = jnp.zeros((B, META_ROWS, D), x.dtype).at[:, :N_META].set(meta_tokens.astype(x.dtype)[None])
    xr = x
    for li, p in enumerate(layers):
        xr, xm = _layer(xr, xm, p, tabs_r, tabs_m, ATTN_OPTS_BY_LAYER[li])
    return xr


def _layer_params(w_in, q_norm_g, k_norm_g, sink_logit, pool_w, pool_scale,
                  w_branch_a, w_branch_b, w_out, ln1_g, ln1_b, w_up, w_down, ln2_g, ln2_b):
    mm = lambda w: w.astype(MM_DTYPE)
    vec = lambda v: v.astype(F32).reshape(1, -1)
    layers = []
    for l in range(w_in.shape[0]):
        layers.append(dict(
            w_qkv=mm(w_in[l, :, :QKV_WIDTH]), wg=mm(w_in[l, :, QKV_WIDTH:]),
            qg=vec(q_norm_g[l]) * QK_SCALE, kg=vec(k_norm_g[l]), sink=sink_logit[l].astype(F32),
            pw=mm(pool_w[l]), ps=vec(pool_scale[l]),
            wa=mm(w_branch_a[l]), wb=mm(w_branch_b[l]), wo=mm(w_out[l]),
            ln1_g=vec(ln1_g[l]), ln1_b=vec(ln1_b[l]),
            wu=mm(w_up[l]), wd=mm(w_down[l]),
            ln2_g=vec(ln2_g[l]), ln2_b=vec(ln2_b[l]),
        ))
    return layers


def kernel(x_prompt, x_sample, meta_tokens, w_in, q_norm_g, k_norm_g, sink_logit, pool_w, pool_scale,
           w_branch_a, w_branch_b, w_out, ln1_g, ln1_b, w_up, w_down, ln2_g, ln2_b):
    layers = _layer_params(w_in, q_norm_g, k_norm_g, sink_logit, pool_w, pool_scale,
                           w_branch_a, w_branch_b, w_out, ln1_g, ln1_b, w_up, w_down, ln2_g, ln2_b)
    tabs_m = _meta_tables()
    tabs_s = _real_tables(x_sample.shape[1])
    tabs_p = tuple(t[:x_prompt.shape[1]] for t in tabs_s)
    y_prompt = _trunk(x_prompt, meta_tokens, layers, tabs_p, tabs_m)
    y_sample = _trunk(x_sample, meta_tokens, layers, tabs_s, tabs_m)
    return (y_prompt, y_sample)
```

```python
import functools

import jax
import jax.numpy as jnp
from jax import lax
from jax.experimental import pallas as pl
from jax.experimental.pallas import tpu as pltpu

D_MODEL = 1024
HEAD_DIM = 128
N_HEADS = 8
N_KV = 2
GROUP = N_HEADS // N_KV
WINDOW = 128
N_META = 16
META_ROWS = 128
GRID_W = 64
ROPE_THETA = 10000.0
POOL_WINDOWS = (2, 4, 8, 16)
POOL_DIM = D_MODEL // len(POOL_WINDOWS)
POOL_HALO = 8
D_FF = 2816
DEPTH = 2
ALPHA = (2 * DEPTH) ** 0.25
NEG = -1e30
QKV_WIDTH = 4096
LOG2E = 1.4426950408889634
QK_SCALE = HEAD_DIM ** -0.5 * LOG2E

MM_DTYPE = jnp.bfloat16
F32 = jnp.float32

V7X_VMEM_BYTES = 64 * 1024 * 1024
VMEM_CAP = V7X_VMEM_BYTES - 6 * 1024 * 1024

ROW_TILE = 512
DENSE_TILE = 1024
ATTN_TQ = 512
ATTN_TK = 512
WIN_TQ = 1024
WIN_SUB = 256
WIN_BLK = 128
FF_CHUNK = 256
SUB_ROWS = 256
WEIGHT_ROWS = 16

_NT = (((1,), (1,)), ((), ()))


def _vmem_limit(block_bytes, scratch_bytes=0, temp_bytes=0):
    est = 2 * block_bytes + scratch_bytes + temp_bytes + (8 << 20)
    return int(min(max(est, 16 << 20), VMEM_CAP))


def _nbytes(shape, dtype):
    n = 1
    for s in shape:
        n *= s
    return n * jnp.dtype(dtype).itemsize


def _resident(shape):
    zeros = (0,) * len(shape)
    return pl.BlockSpec(shape, lambda *_: zeros, pipeline_mode=pl.Buffered(1))


def _layer_norm(x, g, b):
    mu = jnp.mean(x, axis=-1, keepdims=True)
    xc = x - mu
    var = jnp.mean(xc * xc, axis=-1, keepdims=True)
    return xc * lax.rsqrt(var + 1e-5) * g + b


def _pool_group(ext_ref, d_ref, pos, seq_len, tm, gi):
    n = tm + 2 * POOL_HALO
    win = POOL_WINDOWS[gi]
    c0 = gi * POOL_DIM

    def ahead(x, k):
        return pltpu.roll(x, n - k, 0) if k % n else x

    s = ext_ref[:, c0:c0 + POOL_DIM]
    span = 1
    while span < win:
        s = s + ahead(s, span)
        span *= 2
    acc = ahead(s, POOL_HALO - win // 2)[0:tm]
    lo = jnp.maximum(pos - win // 2, 0)
    hi = jnp.minimum(pos - win // 2 + win, seq_len)
    mean = acc / (hi - lo).astype(F32)
    centre = ext_ref[POOL_HALO:POOL_HALO + tm, c0:c0 + POOL_DIM]
    d_ref[0, :, c0:c0 + POOL_DIM] = (mean - centre).astype(d_ref.dtype)


def _proj_body(x_ref, x_ext, pos, seq_len, w_ref, qg_ref, kg_ref, ca_ref, sa_ref, cb_ref, sb_ref,
               qa_ref, ka_ref, va_ref, qb_ref, kb_ref, vb_ref, d_ref, ext_ref, *, tm, cka, ckb):
    xe = x_ext.astype(MM_DTYPE)
    for blk in range(2):
        c0 = 3072 + blk * 512
        ext_ref[:, blk * 512:(blk + 1) * 512] = jnp.dot(xe, w_ref[:, c0:c0 + 512], preferred_element_type=F32)
    pool_steps = [functools.partial(_pool_group, ext_ref, d_ref, pos, seq_len, tm, gi)
                  for gi in range(len(POOL_WINDOWS))]

    def pool_step():
        if pool_steps:
            pool_steps.pop(0)()

    sub = min(tm, SUB_ROWS)
    qg = qg_ref[...]
    kg = kg_ref[...]
    lane = lax.broadcasted_iota(jnp.int32, (sub, HEAD_DIM), 1)
    low_quarter = (lane % 64) < 32

    def rms(x, g):
        return x * lax.rsqrt(jnp.mean(x * x, axis=-1, keepdims=True) + 1e-6) * g

    def head(u, j):
        return u[:, j * HEAD_DIM:(j + 1) * HEAD_DIM]

    for r in range(0, tm, sub):
        rows = slice(r, r + sub)
        xb = x_ref[0, rows, :].astype(MM_DTYPE)
        ca = ca_ref[rows, :]
        sa = sa_ref[rows, :]
        cb = cb_ref[rows, :]
        sb = sb_ref[rows, :]

        def rope_a(x):
            partner = jnp.where(low_quarter, pltpu.roll(x, 96, 1), pltpu.roll(x, 32, 1))
            return x * ca + partner * sa

        def rope_b(x):
            return x * cb + pltpu.roll(x, 64, 1) * sb

        def cols(c0, n):
            return jnp.dot(xb, w_ref[:, c0:c0 + n], preferred_element_type=F32)

        def store_vt(v_ref, v, ck):
            step = min(sub, ck)
            for o in range(0, sub, step):
                j, lane0 = (r + o) // ck, (r + o) % ck
                v_ref[0, j, :, lane0:lane0 + step] = v[o:o + step, :].T.astype(v_ref.dtype)

        for blk in range(2):
            u = cols(blk * 512, 512)
            for j in range(4):
                h = blk * 4 + j
                qa_ref[0, rows, h * HEAD_DIM:(h + 1) * HEAD_DIM] = rope_a(rms(head(u, j), qg)).astype(qa_ref.dtype)
        pool_step()
        u = cols(1024, 512)
        for j in range(2):
            ka_ref[0, rows, j * HEAD_DIM:(j + 1) * HEAD_DIM] = rope_a(rms(head(u, j), kg)).astype(ka_ref.dtype)
        store_vt(va_ref, u[:, 256:512], cka)
        for blk in range(2):
            u = cols(1536 + blk * 512, 512)
            for j in range(4):
                h = blk * 4 + j
                qb_ref[0, rows, h * HEAD_DIM:(h + 1) * HEAD_DIM] = (
                    rope_b(head(u, j)) * QK_SCALE).astype(qb_ref.dtype)
        pool_step()
        u = cols(2560, 512)
        for j in range(2):
            kb_ref[0, rows, j * HEAD_DIM:(j + 1) * HEAD_DIM] = rope_b(head(u, j)).astype(kb_ref.dtype)
        store_vt(vb_ref, u[:, 256:512], ckb)
    while pool_steps:
        pool_step()


def _proj_real_kernel(x_ref, prev_ref, meta_ref, next_ref, *rest, tm, n_tiles, seq_len, cka, ckb):
    i = pl.program_id(1)
    x_ext = jnp.concatenate([jnp.where(i == 0, meta_ref[0], prev_ref[0]), x_ref[0],
                             jnp.where(i == n_tiles - 1, 0.0, next_ref[0])], axis=0)
    pos = N_META + i * tm + lax.broadcasted_iota(jnp.int32, (tm, 1), 0)
    _proj_body(x_ref, x_ext, pos, seq_len, *rest, tm=tm, cka=cka, ckb=ckb)


def _proj_meta_kernel(x_ref, first_ref, *rest, seq_len):
    x_ext = jnp.concatenate([jnp.zeros((POOL_HALO, D_MODEL), F32), x_ref[0, 0:N_META, :], first_ref[0],
                             jnp.zeros((META_ROWS - N_META, D_MODEL), F32)], axis=0)
    pos = lax.broadcasted_iota(jnp.int32, (META_ROWS, 1), 0)
    _proj_body(x_ref, x_ext, pos, seq_len, *rest, tm=META_ROWS, cka=META_ROWS, ckb=META_ROWS)


def _proj_call(kernel, grid, x_specs, x_args, w_qkv, qg, kg, tabs, B, T, tm, cka, ckb, row, tab_index):
    D = D_MODEL
    tab = pl.BlockSpec((tm, HEAD_DIM), tab_index)
    small = pl.BlockSpec((1, HEAD_DIM), lambda *_: (0, 0))
    chunked = lambda f: (lambda *idx: row(*idx)[:2] + (0, 0))
    out_shape = (
        jax.ShapeDtypeStruct((B, T, 1024), MM_DTYPE),
        jax.ShapeDtypeStruct((B, T, 256), MM_DTYPE),
        jax.ShapeDtypeStruct((B, T // cka, 256, cka), MM_DTYPE),
        jax.ShapeDtypeStruct((B, T, 1024), MM_DTYPE),
        jax.ShapeDtypeStruct((B, T, 256), MM_DTYPE),
        jax.ShapeDtypeStruct((B, T // ckb, 256, ckb), MM_DTYPE),
        jax.ShapeDtypeStruct((B, T, 1024), MM_DTYPE),
    )
    out_specs = (
        pl.BlockSpec((1, tm, 1024), row),
        pl.BlockSpec((1, tm, 256), row),
        pl.BlockSpec((1, tm // cka, 256, cka), chunked(row)),
        pl.BlockSpec((1, tm, 1024), row),
        pl.BlockSpec((1, tm, 256), row),
        pl.BlockSpec((1, tm // ckb, 256, ckb), chunked(row)),
        pl.BlockSpec((1, tm, 1024), row),
    )
    blocks = (_nbytes((tm, D), F32) + 4 * _nbytes((tm, HEAD_DIM), F32) + _nbytes((tm, 4096), MM_DTYPE))
    ext_bytes = _nbytes((tm + 2 * POOL_HALO, D), F32)
    return pl.pallas_call(
        kernel,
        grid=grid,
        in_specs=x_specs + [_resident(w_qkv.shape), small, small, tab, tab, tab, tab],
        out_specs=out_specs,
        out_shape=out_shape,
        scratch_shapes=[pltpu.VMEM((tm + 2 * POOL_HALO, D), F32)],
        compiler_params=pltpu.CompilerParams(
            dimension_semantics=("parallel",) * len(grid),
            vmem_limit_bytes=_vmem_limit(blocks, _nbytes(w_qkv.shape, MM_DTYPE) + ext_bytes,
                                         2 * ext_bytes + 8 * _nbytes((tm, 512), F32))),
        name="proj",
    )(*x_args, w_qkv, qg, kg, *tabs)


def _proj(x, xm, w_qkv, qg, kg, tabs, *, tm, cka, ckb):
    B, S, D = x.shape
    n_tiles = S // tm
    per = tm // POOL_HALO
    last = S // POOL_HALO - 1
    row = lambda b, i: (b, i, 0)
    halo = lambda f: pl.BlockSpec((1, POOL_HALO, D), f)
    x_specs = [
        pl.BlockSpec((1, tm, D), row),
        halo(lambda b, i: (b, jnp.maximum(i * per - 1, 0), 0)),
        halo(lambda b, i: (b, N_META // POOL_HALO - 1, 0)),
        halo(lambda b, i: (b, jnp.minimum((i + 1) * per, last), 0)),
    ]
    kernel = functools.partial(_proj_real_kernel, tm=tm, n_tiles=n_tiles, seq_len=N_META + S, cka=cka, ckb=ckb)
    return _proj_call(kernel, (B, n_tiles), x_specs, (x, x, xm, x), w_qkv, qg, kg, tabs, B, S, tm, cka, ckb,
                      row, lambda b, i: (i, 0))


def _proj_meta(xm, x, w_qkv, qg, kg, tabs):
    B, _, D = xm.shape
    row = lambda b: (b, 0, 0)
    x_specs = [pl.BlockSpec((1, META_ROWS, D), row), pl.BlockSpec((1, POOL_HALO, D), row)]
    kernel = functools.partial(_proj_meta_kernel, seq_len=N_META + x.shape[1])
    return _proj_call(kernel, (B,), x_specs, (xm, x), w_qkv, qg, kg, tabs, B, META_ROWS, META_ROWS,
                      META_ROWS, META_ROWS, row, lambda b: (0, 0))


def _attn_a_kernel(q_ref, k_ref, vt_ref, km_ref, vmt_ref, o_ref,
                   acc_ref, m_ref, l_ref, s_ref, p_ref, ms_ref, as_ref, ac_ref, *, tq, tk, n_chunks):
    def q_head(h):
        return q_ref[0, :, h * HEAD_DIM:(h + 1) * HEAD_DIM]

    def scores(c, slot, h):
        kc = k_ref[0, pl.ds(pl.multiple_of(c * tk, tk), tk), :]
        s = lax.dot_general(kc, q_head(h), _NT, preferred_element_type=F32)
        s_ref[slot, h] = s
        m_old = m_ref[h]
        m_new = jnp.maximum(m_old, jnp.max(s, axis=0, keepdims=True))
        m_ref[h] = m_new
        ms_ref[slot, h] = m_new
        as_ref[slot, h] = jnp.exp2(m_old - m_new)

    def weights(slot, h):
        a = as_ref[slot, h]
        m = ms_ref[slot, h]
        ac_ref[slot, h] = a
        part = jnp.zeros((8, tq), F32)
        for r in range(0, tk, WEIGHT_ROWS):
            p = jnp.exp2(s_ref[slot, h, r:r + WEIGHT_ROWS, :] - m)
            p_ref[slot, h, r:r + WEIGHT_ROWS, :] = p.astype(p_ref.dtype)
            for i in range(WEIGHT_ROWS // 8):
                part = part + p[i * 8:(i + 1) * 8, :]
        l_ref[h] = a * l_ref[h] + jnp.sum(part, axis=0, keepdims=True)

    def values(c, slot, h):
        pv = jnp.dot(vt_ref[0, c], p_ref[slot, h], preferred_element_type=F32)
        acc_ref[h] = ac_ref[slot, h] * acc_ref[h] + pv

    key_row = lax.broadcasted_iota(jnp.int32, (META_ROWS, tq), 0)
    s_meta = [lax.dot_general(km_ref[0], q_head(h), _NT, preferred_element_type=F32) for h in range(GROUP)]
    p_meta = []
    for h in range(GROUP):
        s = jnp.where(key_row < N_META, s_meta[h], NEG)
        m = jnp.max(s, axis=0, keepdims=True)
        p = jnp.exp2(s - m)
        m_ref[h] = m
        l_ref[h] = jnp.sum(p, axis=0, keepdims=True)
        p_meta.append(p.astype(MM_DTYPE))
    for h in range(GROUP):
        acc_ref[h] = jnp.dot(vmt_ref[0, 0], p_meta[h], preferred_element_type=F32)
    for h in range(GROUP):
        scores(0, 0, h)
    for h in range(GROUP):
        scores(1, 1, h)
        weights(0, h)

    def body(j, carry):
        c1 = 2 * j + 1
        for h in range(GROUP):
            scores(c1 + 1, 0, h)
            values(c1 - 1, 0, h)
            weights(1, h)
        for h in range(GROUP):
            scores(c1 + 2, 1, h)
            values(c1, 1, h)
            weights(0, h)
        return carry

    lax.fori_loop(0, n_chunks // 2 - 1, body, 0)
    for h in range(GROUP):
        values(n_chunks - 2, 0, h)
        weights(1, h)
    for h in range(GROUP):
        values(n_chunks - 1, 1, h)
    for h in range(GROUP):
        o = acc_ref[h] / l_ref[h]
        o_ref[0, :, h * HEAD_DIM:(h + 1) * HEAD_DIM] = o.T.astype(o_ref.dtype)


def _attn_a(q, k, vt, km, vmt, *, tq, tk):
    B, Tq, _ = q.shape
    S = k.shape[1]
    n_chunks = S // tk
    assert n_chunks % 2 == 0
    blocks = (_nbytes((tq, 512), MM_DTYPE) * 2 + 2 * _nbytes((S, HEAD_DIM), MM_DTYPE)
              + 2 * _nbytes((META_ROWS, HEAD_DIM), MM_DTYPE))
    stat = pltpu.VMEM((2, GROUP, 1, tq), F32)
    scratch = (GROUP * (_nbytes((HEAD_DIM, tq), F32) + 8 * _nbytes((8, tq), F32))
               + 2 * GROUP * (_nbytes((tk, tq), F32) + _nbytes((tk, tq), MM_DTYPE)))
    return pl.pallas_call(
        functools.partial(_attn_a_kernel, tq=tq, tk=tk, n_chunks=n_chunks),
        grid=(B, N_KV, Tq // tq),
        in_specs=[
            pl.BlockSpec((1, tq, GROUP * HEAD_DIM), lambda b, g, i: (b, i, g)),
            pl.BlockSpec((1, S, HEAD_DIM), lambda b, g, i: (b, 0, g)),
            pl.BlockSpec((1, n_chunks, HEAD_DIM, tk), lambda b, g, i: (b, 0, g, 0)),
            pl.BlockSpec((1, META_ROWS, HEAD_DIM), lambda b, g, i: (b, 0, g)),
            pl.BlockSpec((1, 1, HEAD_DIM, META_ROWS), lambda b, g, i: (b, 0, g, 0)),
        ],
        out_specs=pl.BlockSpec((1, tq, GROUP * HEAD_DIM), lambda b, g, i: (b, i, g)),
        out_shape=jax.ShapeDtypeStruct((B, Tq, N_HEADS * HEAD_DIM), MM_DTYPE),
        scratch_shapes=[
            pltpu.VMEM((GROUP, HEAD_DIM, tq), F32),
            pltpu.VMEM((GROUP, 1, tq), F32),
            pltpu.VMEM((GROUP, 1, tq), F32),
            pltpu.VMEM((2, GROUP, tk, tq), F32),
            pltpu.VMEM((2, GROUP, tk, tq), MM_DTYPE),
            stat, stat, stat,
        ],
        compiler_params=pltpu.CompilerParams(
            dimension_semantics=("parallel", "parallel", "parallel"),
            vmem_limit_bytes=_vmem_limit(blocks, scratch, 6 * _nbytes((tk, tq), F32))),
        name="attn_global",
    )(q, k, vt, km, vmt)


def _win_attend(sink_ref, q_ref, o_ref, problems):
    g = pl.program_id(1)
    scores = [[lax.dot_general(k_all, q_ref[0, col0:col0 + width, h * HEAD_DIM:(h + 1) * HEAD_DIM], _NT,
                               preferred_element_type=F32) for h in range(GROUP)]
              for col0, width, k_all, _, _ in problems]
    for (col0, width, _, vt_all, mask), s_heads in zip(problems, scores):
        for h in range(GROUP):
            s = jnp.where(mask, s_heads[h], NEG)
            sink = sink_ref[g * GROUP + h] * LOG2E
            m = jnp.maximum(jnp.max(s, axis=0, keepdims=True), sink)
            p = jnp.exp2(s - m)
            den = jnp.sum(p, axis=0, keepdims=True) + jnp.exp2(sink - m)
            o = jnp.dot(vt_all, p.astype(MM_DTYPE), preferred_element_type=F32) / den
            o_ref[0, col0:col0 + width, h * HEAD_DIM:(h + 1) * HEAD_DIM] = o.T.astype(o_ref.dtype)


def _attn_b_kernel(sink_ref, q_ref, kp_ref, kc_ref, kn_ref, vp_ref, vc_ref, vn_ref, km_ref, vmt_ref, o_ref,
                   *, tq, n_tiles):
    i = pl.program_id(2)
    nb = tq // WIN_BLK
    n_sub = tq // WIN_SUB
    k_real = jnp.concatenate([kp_ref[0], kc_ref[0], kn_ref[0]], axis=0)
    vt_real = jnp.concatenate([vp_ref[0, 0]] + [vc_ref[0, j] for j in range(nb)] + [vn_ref[0, 0]], axis=1)
    n_keys = META_ROWS + 2 * WIN_BLK + WIN_SUB
    r = lax.broadcasted_iota(jnp.int32, (n_keys, WIN_SUB), 0)
    c = lax.broadcasted_iota(jnp.int32, (n_keys, WIN_SUB), 1)
    rel = r - c - (META_ROWS + WIN_BLK)
    in_win = (rel >= -WINDOW) & (rel <= WINDOW) & (r >= META_ROWS)
    problems = []
    for j in range(n_sub):
        span = slice(j * WIN_SUB, j * WIN_SUB + WIN_SUB + 2 * WIN_BLK)
        k_all = jnp.concatenate([km_ref[0], k_real[span, :]], axis=0)
        vt_all = jnp.concatenate([vmt_ref[0, 0], vt_real[:, span]], axis=1)
        valid = in_win
        if j == 0:
            valid = valid & (r >= META_ROWS + jnp.where(i == 0, WIN_BLK, 0))
        if j == n_sub - 1:
            valid = valid & (r < n_keys - jnp.where(i == n_tiles - 1, WIN_BLK, 0))
        problems.append((j * WIN_SUB, WIN_SUB, k_all, vt_all, (r < N_META) | valid))
    _win_attend(sink_ref, q_ref, o_ref, problems)


def _attn_b_meta_kernel(sink_ref, q_ref, k0_ref, v0_ref, km_ref, vmt_ref, o_ref):
    k_all = jnp.concatenate([km_ref[0], k0_ref[0]], axis=0)
    vt_all = jnp.concatenate([vmt_ref[0, 0], v0_ref[0, 0]], axis=1)
    r = lax.broadcasted_iota(jnp.int32, (META_ROWS + WIN_BLK, META_ROWS), 0)
    c = lax.broadcasted_iota(jnp.int32, (META_ROWS + WIN_BLK, META_ROWS), 1)
    mask = (r < N_META) | ((r >= META_ROWS) & (r - META_ROWS + N_META - c <= WINDOW))
    _win_attend(sink_ref, q_ref, o_ref, [(0, META_ROWS, k_all, vt_all, mask)])


def _attn_b(sink, q, k, vt, km, vmt, *, tq):
    B, S, _ = q.shape
    n_tiles = S // tq
    nb = tq // WIN_BLK
    last_blk = S // WIN_BLK - 1
    kblk = lambda shape, f: pl.BlockSpec(shape, f)
    n_keys = META_ROWS + 2 * WIN_BLK + tq
    blocks = 2 * _nbytes((tq, 512), MM_DTYPE) + 2 * _nbytes((n_keys, HEAD_DIM), MM_DTYPE)
    grid_spec = pltpu.PrefetchScalarGridSpec(
        num_scalar_prefetch=1,
        grid=(B, N_KV, n_tiles),
        in_specs=[
            pl.BlockSpec((1, tq, GROUP * HEAD_DIM), lambda b, g, i, s: (b, i, g)),
            kblk((1, WIN_BLK, HEAD_DIM), lambda b, g, i, s: (b, jnp.maximum(i * nb - 1, 0), g)),
            kblk((1, tq, HEAD_DIM), lambda b, g, i, s: (b, i, g)),
            kblk((1, WIN_BLK, HEAD_DIM), lambda b, g, i, s: (b, jnp.minimum((i + 1) * nb, last_blk), g)),
            kblk((1, 1, HEAD_DIM, WIN_BLK), lambda b, g, i, s: (b, jnp.maximum(i * nb - 1, 0), g, 0)),
            kblk((1, nb, HEAD_DIM, WIN_BLK), lambda b, g, i, s: (b, i, g, 0)),
            kblk((1, 1, HEAD_DIM, WIN_BLK), lambda b, g, i, s: (b, jnp.minimum((i + 1) * nb, last_blk), g, 0)),
            kblk((1, META_ROWS, HEAD_DIM), lambda b, g, i, s: (b, 0, g)),
            kblk((1, 1, HEAD_DIM, META_ROWS), lambda b, g, i, s: (b, 0, g, 0)),
        ],
        out_specs=pl.BlockSpec((1, tq, GROUP * HEAD_DIM), lambda b, g, i, s: (b, i, g)),
    )
    return pl.pallas_call(
        functools.partial(_attn_b_kernel, tq=tq, n_tiles=n_tiles),
        grid_spec=grid_spec,
        out_shape=jax.ShapeDtypeStruct((B, S, N_HEADS * HEAD_DIM), MM_DTYPE),
        compiler_params=pltpu.CompilerParams(
            dimension_semantics=("parallel", "parallel", "parallel"),
            vmem_limit_bytes=_vmem_limit(blocks, 0, 8 * _nbytes((n_keys, tq), F32))),
        name="attn_window",
    )(sink, q, k, k, k, vt, vt, vt, km, vmt)


def _attn_b_meta(sink, qm, k, vt, km, vmt):
    B = qm.shape[0]
    blocks = 2 * _nbytes((META_ROWS, 512), MM_DTYPE) + 4 * _nbytes((META_ROWS, HEAD_DIM), MM_DTYPE)
    grid_spec = pltpu.PrefetchScalarGridSpec(
        num_scalar_prefetch=1,
        grid=(B, N_KV),
        in_specs=[
            pl.BlockSpec((1, META_ROWS, GROUP * HEAD_DIM), lambda b, g, s: (b, 0, g)),
            pl.BlockSpec((1, WIN_BLK, HEAD_DIM), lambda b, g, s: (b, 0, g)),
            pl.BlockSpec((1, 1, HEAD_DIM, WIN_BLK), lambda b, g, s: (b, 0, g, 0)),
            pl.BlockSpec((1, META_ROWS, HEAD_DIM), lambda b, g, s: (b, 0, g)),
            pl.BlockSpec((1, 1, HEAD_DIM, META_ROWS), lambda b, g, s: (b, 0, g, 0)),
        ],
        out_specs=pl.BlockSpec((1, META_ROWS, GROUP * HEAD_DIM), lambda b, g, s: (b, 0, g)),
    )
    return pl.pallas_call(
        _attn_b_meta_kernel,
        grid_spec=grid_spec,
        out_shape=jax.ShapeDtypeStruct((B, META_ROWS, N_HEADS * HEAD_DIM), MM_DTYPE),
        compiler_params=pltpu.CompilerParams(
            dimension_semantics=("parallel", "parallel"),
            vmem_limit_bytes=_vmem_limit(blocks, 0, 8 * _nbytes((2 * META_ROWS, META_ROWS), F32))),
        name="attn_window_meta",
    )(sink, qm, k, vt, km, vmt)


def _merge_kernel(h_ref, oa_ref, ob_ref, d_ref, wg_ref, wa_ref, wb_ref, pw_ref, ps_ref, wo_ref, g_ref, b_ref,
                  out_ref, *, tm):
    def gate(hb, j):
        return jax.nn.sigmoid(jnp.dot(hb, wg_ref[:, j * D_MODEL:(j + 1) * D_MODEL], preferred_element_type=F32))

    sub = min(tm, SUB_ROWS)
    for r in range(0, tm, sub):
        rows = slice(r, r + sub)
        h = h_ref[0, rows, :]
        hb = h.astype(MM_DTYPE)
        ya = jnp.dot(oa_ref[0, rows, :], wa_ref[...], preferred_element_type=F32)
        yb = jnp.dot(ob_ref[0, rows, :], wb_ref[...], preferred_element_type=F32)
        yc = jnp.concatenate(
            [jnp.dot(d_ref[0, rows, gi * POOL_DIM:(gi + 1) * POOL_DIM], pw_ref[gi], preferred_element_type=F32)
             for gi in range(len(POOL_WINDOWS))], axis=1) * ps_ref[...]
        merged = gate(hb, 0) * ya + gate(hb, 1) * yb + gate(hb, 2) * yc
        y = jnp.dot(merged.astype(MM_DTYPE), wo_ref[...], preferred_element_type=F32)
        out_ref[0, rows, :] = _layer_norm(ALPHA * h + y, g_ref[...], b_ref[...])


def _merge(h, oa, ob, d, wg, wa, wb, pw, ps, wo, g, b, *, tm):
    B, T, D = h.shape
    row = lambda bb, i: (bb, i, 0)
    vec = pl.BlockSpec((1, D), lambda bb, i: (0, 0))
    wbytes = sum(_nbytes(w.shape, MM_DTYPE) for w in (wg, wa, wb, pw, wo))
    blocks = 2 * _nbytes((tm, D), F32) + 3 * _nbytes((tm, D), MM_DTYPE)
    return pl.pallas_call(
        functools.partial(_merge_kernel, tm=tm),
        grid=(B, T // tm),
        in_specs=[pl.BlockSpec((1, tm, D), row)] * 4
        + [_resident(wg.shape), _resident(wa.shape), _resident(wb.shape), _resident(pw.shape), vec,
           _resident(wo.shape), vec, vec],
        out_specs=pl.BlockSpec((1, tm, D), row),
        out_shape=jax.ShapeDtypeStruct((B, T, D), F32),
        compiler_params=pltpu.CompilerParams(
            dimension_semantics=("parallel", "parallel"),
            vmem_limit_bytes=_vmem_limit(blocks, wbytes, 12 * _nbytes((min(tm, SUB_ROWS), D), F32))),
        name="merge",
    )(h, oa, ob, d, wg, wa, wb, pw, ps, wo, g, b)


def _ffn_kernel(h_ref, wu_ref, wd_ref, g_ref, b_ref, out_ref, act_ref, *, tm):
    sub = min(tm, SUB_ROWS)
    for r in range(0, tm, sub):
        h = h_ref[0, r:r + sub, :]
        hb = h.astype(MM_DTYPE)
        for c in range(D_FF // FF_CHUNK):
            c0 = c * FF_CHUNK
            gate = jnp.dot(hb, wu_ref[:, c0:c0 + FF_CHUNK], preferred_element_type=F32)
            up = jnp.dot(hb, wu_ref[:, D_FF + c0:D_FF + c0 + FF_CHUNK], preferred_element_type=F32)
            act_ref[r:r + sub, c0:c0 + FF_CHUNK] = (jax.nn.silu(gate) * up).astype(act_ref.dtype)
        f = jnp.dot(act_ref[r:r + sub, :], wd_ref[...], preferred_element_type=F32)
        out_ref[0, r:r + sub, :] = _layer_norm(ALPHA * h + f, g_ref[...], b_ref[...])


def _ffn(h, wu, wd, g, b, *, tm):
    B, T, D = h.shape
    row = lambda bb, i: (bb, i, 0)
    vec = pl.BlockSpec((1, D), lambda bb, i: (0, 0))
    wbytes = _nbytes(wu.shape, MM_DTYPE) + _nbytes(wd.shape, MM_DTYPE)
    return pl.pallas_call(
        functools.partial(_ffn_kernel, tm=tm),
        grid=(B, T // tm),
        in_specs=[pl.BlockSpec((1, tm, D), row), _resident(wu.shape), _resident(wd.shape), vec, vec],
        out_specs=pl.BlockSpec((1, tm, D), row),
        out_shape=jax.ShapeDtypeStruct((B, T, D), F32),
        scratch_shapes=[pltpu.VMEM((tm, D_FF), MM_DTYPE)],
        compiler_params=pltpu.CompilerParams(
            dimension_semantics=("parallel", "parallel"),
            vmem_limit_bytes=_vmem_limit(2 * _nbytes((tm, D), F32), wbytes + _nbytes((tm, D_FF), MM_DTYPE),
                                         6 * _nbytes((tm, D), F32))),
        name="ffn",
    )(h, wu, wd, g, b)


def _rope_tables(row, col, pos):
    def cs(p, dim):
        inv = ROPE_THETA ** (-jnp.arange(0, dim, 2, dtype=F32) / dim)
        ang = p.astype(F32)[:, None] * inv[None, :]
        return jnp.cos(ang), jnp.sin(ang)

    cr, sr = cs(row, HEAD_DIM // 2)
    cc, sc = cs(col, HEAD_DIM // 2)
    cp, sp = cs(pos, HEAD_DIM)
    ca = jnp.concatenate([cr, cr, cc, cc], axis=1)
    sa = jnp.concatenate([-sr, sr, -sc, sc], axis=1)
    cb = jnp.concatenate([cp, cp], axis=1)
    sb = jnp.concatenate([-sp, sp], axis=1)
    return ca, sa, cb, sb


def _real_tables(S):
    t = jnp.arange(S, dtype=jnp.int32)
    return _rope_tables(t // GRID_W, t % GRID_W, t + N_META)


def _meta_tables():
    i = jnp.arange(META_ROWS, dtype=jnp.int32)
    return _rope_tables(-jnp.ones((META_ROWS,), jnp.int32), i, i)


def _layer(xr, xm, p, tabs_r, tabs_m):
    qa, ka, va, qb, kb, vb, d = _proj(xr, xm, p["w_qkv"], p["qg"], p["kg"], tabs_r,
                                      tm=ROW_TILE, cka=ATTN_TK, ckb=WIN_BLK)
    qam, kam, vam, qbm, kbm, vbm, dm = _proj_meta(xm, xr, p["w_qkv"], p["qg"], p["kg"], tabs_m)
    oa = _attn_a(qa, ka, va, kam, vam, tq=ATTN_TQ, tk=ATTN_TK)
    oam = _attn_a(qam, ka, va, kam, vam, tq=META_ROWS, tk=ATTN_TK)
    ob = _attn_b(p["sink"], qb, kb, vb, kbm, vbm, tq=WIN_TQ)
    obm = _attn_b_meta(p["sink"], qbm, kb, vb, kbm, vbm)
    merge_w = (p["wg"], p["wa"], p["wb"], p["pw"], p["ps"], p["wo"], p["ln1_g"], p["ln1_b"])
    hr = _merge(xr, oa, ob, d, *merge_w, tm=DENSE_TILE)
    hm = _merge(xm, oam, obm, dm, *merge_w, tm=META_ROWS)
    ffn_w = (p["wu"], p["wd"], p["ln2_g"], p["ln2_b"])
    return _ffn(hr, *ffn_w, tm=DENSE_TILE), _ffn(hm, *ffn_w, tm=META_ROWS)


def _trunk(x, meta_tokens, layers, tabs_r, tabs_m):
    B, S, D = x.shape
    xm = jnp.zeros((B, META_ROWS, D), x.dtype).at[:, :N_META].set(meta_tokens.astype(x.dtype)[None])
    xr = x
    for p in layers:
        xr, xm = _layer(xr, xm, p, tabs_r, tabs_m)
    return xr


def _layer_params(w_in, q_norm_g, k_norm_g, sink_logit, pool_w, pool_scale,
                  w_branch_a, w_branch_b, w_out, ln1_g, ln1_b, w_up, w_down, ln2_g, ln2_b):
    mm = lambda w: w.astype(MM_DTYPE)
    vec = lambda v: v.astype(F32).reshape(1, -1)
    layers = []
    for l in range(w_in.shape[0]):
        layers.append(dict(
            w_qkv=mm(w_in[l, :, :QKV_WIDTH]), wg=mm(w_in[l, :, QKV_WIDTH:]),
            qg=vec(q_norm_g[l]) * QK_SCALE, kg=vec(k_norm_g[l]), sink=sink_logit[l].astype(F32),
            pw=mm(pool_w[l]), ps=vec(pool_scale[l]),
            wa=mm(w_branch_a[l]), wb=mm(w_branch_b[l]), wo=mm(w_out[l]),
            ln1_g=vec(ln1_g[l]), ln1_b=vec(ln1_b[l]),
            wu=mm(w_up[l]), wd=mm(w_down[l]),
            ln2_g=vec(ln2_g[l]), ln2_b=vec(ln2_b[l]),
        ))
    return layers


def kernel(x_prompt, x_sample, meta_tokens, w_in, q_norm_g, k_norm_g, sink_logit, pool_w, pool_scale,
           w_branch_a, w_branch_b, w_out, ln1_g, ln1_b, w_up, w_down, ln2_g, ln2_b):
    layers = _layer_params(w_in, q_norm_g, k_norm_g, sink_logit, pool_w, pool_scale,
                           w_branch_a, w_branch_b, w_out, ln1_g, ln1_b, w_up, w_down, ln2_g, ln2_b)
    tabs_m = _meta_tables()
    tabs_s = _real_tables(x_sample.shape[1])
    tabs_p = tuple(t[:x_prompt.shape[1]] for t in tabs_s)
    y_prompt = _trunk(x_prompt, meta_tokens, layers, tabs_p, tabs_m)
    y_sample = _trunk(x_sample, meta_tokens, layers, tabs_s, tabs_m)
    return (y_prompt, y_sample)
```

```python
import functools

import jax
import jax.numpy as jnp
from jax import lax
from jax.experimental import pallas as pl
from jax.experimental.pallas import tpu as pltpu

D_MODEL = 1024
HEAD_DIM = 128
N_HEADS = 8
N_KV = 2
GROUP = N_HEADS // N_KV
WINDOW = 128
N_META = 16
META_ROWS = 128
GRID_W = 64
ROPE_THETA = 10000.0
POOL_WINDOWS = (2, 4, 8, 16)
POOL_DIM = D_MODEL // len(POOL_WINDOWS)
POOL_HALO = 8
D_FF = 2816
DEPTH = 2
ALPHA = (2 * DEPTH) ** 0.25
NEG = -1e30
QKV_WIDTH = 4096
LOG2E = 1.4426950408889634
QK_SCALE = HEAD_DIM ** -0.5 * LOG2E

MM_DTYPE = jnp.bfloat16
F32 = jnp.float32

V7X_VMEM_BYTES = 64 * 1024 * 1024
VMEM_CAP = V7X_VMEM_BYTES - 6 * 1024 * 1024

ROW_TILE = 512
DENSE_TILE = 1024
ATTN_TQ = 512
ATTN_TK = 512
WIN_TQ = 1024
WIN_SUB = 256
WIN_BLK = 128
FF_CHUNK = 256
SUB_ROWS = 256
WEIGHT_ROWS = 16
LOOP_TRIPS = 2

_NT = (((1,), (1,)), ((), ()))


def _vmem_limit(block_bytes, scratch_bytes=0, temp_bytes=0):
    est = 2 * block_bytes + scratch_bytes + temp_bytes + (8 << 20)
    return int(min(max(est, 16 << 20), VMEM_CAP))


def _nbytes(shape, dtype):
    n = 1
    for s in shape:
        n *= s
    return n * jnp.dtype(dtype).itemsize


def _resident(shape):
    zeros = (0,) * len(shape)
    return pl.BlockSpec(shape, lambda *_: zeros, pipeline_mode=pl.Buffered(1))


def _layer_norm(x, g, b):
    mu = jnp.mean(x, axis=-1, keepdims=True)
    xc = x - mu
    var = jnp.mean(xc * xc, axis=-1, keepdims=True)
    return xc * lax.rsqrt(var + 1e-5) * g + b


def _pool_group(ext_ref, d_ref, pos, seq_len, tm, gi):
    n = tm + 2 * POOL_HALO
    win = POOL_WINDOWS[gi]
    c0 = gi * POOL_DIM

    def ahead(x, k):
        return pltpu.roll(x, n - k, 0) if k % n else x

    s = ext_ref[:, c0:c0 + POOL_DIM]
    span = 1
    while span < win:
        s = s + ahead(s, span)
        span *= 2
    acc = ahead(s, POOL_HALO - win // 2)[0:tm]
    lo = jnp.maximum(pos - win // 2, 0)
    hi = jnp.minimum(pos - win // 2 + win, seq_len)
    mean = acc / (hi - lo).astype(F32)
    centre = ext_ref[POOL_HALO:POOL_HALO + tm, c0:c0 + POOL_DIM]
    d_ref[0, :, c0:c0 + POOL_DIM] = (mean - centre).astype(d_ref.dtype)


def _proj_body(x_ref, x_ext, pos, seq_len, w_ref, qg_ref, kg_ref, ca_ref, sa_ref, cb_ref, sb_ref,
               qa_ref, ka_ref, va_ref, qb_ref, kb_ref, vb_ref, d_ref, ext_ref, *, tm, cka, ckb):
    xe = x_ext.astype(MM_DTYPE)
    for blk in range(2):
        c0 = 3072 + blk * 512
        ext_ref[:, blk * 512:(blk + 1) * 512] = jnp.dot(xe, w_ref[:, c0:c0 + 512], preferred_element_type=F32)
    pool_steps = [functools.partial(_pool_group, ext_ref, d_ref, pos, seq_len, tm, gi)
                  for gi in range(len(POOL_WINDOWS))]

    def pool_step():
        if pool_steps:
            pool_steps.pop(0)()

    sub = min(tm, SUB_ROWS)
    qg = qg_ref[...]
    kg = kg_ref[...]
    lane = lax.broadcasted_iota(jnp.int32, (sub, HEAD_DIM), 1)
    low_quarter = (lane % 64) < 32

    def rms(x, g):
        return x * lax.rsqrt(jnp.mean(x * x, axis=-1, keepdims=True) + 1e-6) * g

    def head(u, j):
        return u[:, j * HEAD_DIM:(j + 1) * HEAD_DIM]

    for r in range(0, tm, sub):
        rows = slice(r, r + sub)
        xb = x_ref[0, rows, :].astype(MM_DTYPE)
        ca = ca_ref[rows, :]
        sa = sa_ref[rows, :]
        cb = cb_ref[rows, :]
        sb = sb_ref[rows, :]

        def rope_a(x):
            partner = jnp.where(low_quarter, pltpu.roll(x, 96, 1), pltpu.roll(x, 32, 1))
            return x * ca + partner * sa

        def rope_b(x):
            return x * cb + pltpu.roll(x, 64, 1) * sb

        def cols(c0, n):
            return jnp.dot(xb, w_ref[:, c0:c0 + n], preferred_element_type=F32)

        def store_vt(v_ref, v, ck):
            step = min(sub, ck)
            for o in range(0, sub, step):
                j, lane0 = (r + o) // ck, (r + o) % ck
                v_ref[0, j, :, lane0:lane0 + step] = v[o:o + step, :].T.astype(v_ref.dtype)

        for blk in range(2):
            u = cols(blk * 512, 512)
            for j in range(4):
                h = blk * 4 + j
                qa_ref[0, rows, h * HEAD_DIM:(h + 1) * HEAD_DIM] = rope_a(rms(head(u, j), qg)).astype(qa_ref.dtype)
        pool_step()
        u = cols(1024, 512)
        for j in range(2):
            ka_ref[0, rows, j * HEAD_DIM:(j + 1) * HEAD_DIM] = rope_a(rms(head(u, j), kg)).astype(ka_ref.dtype)
        store_vt(va_ref, u[:, 256:512], cka)
        for blk in range(2):
            u = cols(1536 + blk * 512, 512)
            for j in range(4):
                h = blk * 4 + j
                qb_ref[0, rows, h * HEAD_DIM:(h + 1) * HEAD_DIM] = (
                    rope_b(head(u, j)) * QK_SCALE).astype(qb_ref.dtype)
        pool_step()
        u = cols(2560, 512)
        for j in range(2):
            kb_ref[0, rows, j * HEAD_DIM:(j + 1) * HEAD_DIM] = rope_b(head(u, j)).astype(kb_ref.dtype)
        store_vt(vb_ref, u[:, 256:512], ckb)
    while pool_steps:
        pool_step()


def _proj_real_kernel(x_ref, prev_ref, meta_ref, next_ref, *rest, tm, n_tiles, seq_len, cka, ckb):
    i = pl.program_id(1)
    x_ext = jnp.concatenate([jnp.where(i == 0, meta_ref[0], prev_ref[0]), x_ref[0],
                             jnp.where(i == n_tiles - 1, 0.0, next_ref[0])], axis=0)
    pos = N_META + i * tm + lax.broadcasted_iota(jnp.int32, (tm, 1), 0)
    _proj_body(x_ref, x_ext, pos, seq_len, *rest, tm=tm, cka=cka, ckb=ckb)


def _proj_meta_kernel(x_ref, first_ref, *rest, seq_len):
    x_ext = jnp.concatenate([jnp.zeros((POOL_HALO, D_MODEL), F32), x_ref[0, 0:N_META, :], first_ref[0],
                             jnp.zeros((META_ROWS - N_META, D_MODEL), F32)], axis=0)
    pos = lax.broadcasted_iota(jnp.int32, (META_ROWS, 1), 0)
    _proj_body(x_ref, x_ext, pos, seq_len, *rest, tm=META_ROWS, cka=META_ROWS, ckb=META_ROWS)


def _proj_call(kernel, grid, x_specs, x_args, w_qkv, qg, kg, tabs, B, T, tm, cka, ckb, row, tab_index):
    D = D_MODEL
    tab = pl.BlockSpec((tm, HEAD_DIM), tab_index)
    small = pl.BlockSpec((1, HEAD_DIM), lambda *_: (0, 0))
    chunked = lambda f: (lambda *idx: row(*idx)[:2] + (0, 0))
    out_shape = (
        jax.ShapeDtypeStruct((B, T, 1024), MM_DTYPE),
        jax.ShapeDtypeStruct((B, T, 256), MM_DTYPE),
        jax.ShapeDtypeStruct((B, T // cka, 256, cka), MM_DTYPE),
        jax.ShapeDtypeStruct((B, T, 1024), MM_DTYPE),
        jax.ShapeDtypeStruct((B, T, 256), MM_DTYPE),
        jax.ShapeDtypeStruct((B, T // ckb, 256, ckb), MM_DTYPE),
        jax.ShapeDtypeStruct((B, T, 1024), MM_DTYPE),
    )
    out_specs = (
        pl.BlockSpec((1, tm, 1024), row),
        pl.BlockSpec((1, tm, 256), row),
        pl.BlockSpec((1, tm // cka, 256, cka), chunked(row)),
        pl.BlockSpec((1, tm, 1024), row),
        pl.BlockSpec((1, tm, 256), row),
        pl.BlockSpec((1, tm // ckb, 256, ckb), chunked(row)),
        pl.BlockSpec((1, tm, 1024), row),
    )
    blocks = (_nbytes((tm, D), F32) + 4 * _nbytes((tm, HEAD_DIM), F32) + _nbytes((tm, 4096), MM_DTYPE))
    ext_bytes = _nbytes((tm + 2 * POOL_HALO, D), F32)
    return pl.pallas_call(
        kernel,
        grid=grid,
        in_specs=x_specs + [_resident(w_qkv.shape), small, small, tab, tab, tab, tab],
        out_specs=out_specs,
        out_shape=out_shape,
        scratch_shapes=[pltpu.VMEM((tm + 2 * POOL_HALO, D), F32)],
        compiler_params=pltpu.CompilerParams(
            dimension_semantics=("parallel",) * len(grid),
            vmem_limit_bytes=_vmem_limit(blocks, _nbytes(w_qkv.shape, MM_DTYPE) + ext_bytes,
                                         2 * ext_bytes + 8 * _nbytes((tm, 512), F32))),
        name="proj",
    )(*x_args, w_qkv, qg, kg, *tabs)


def _proj(x, xm, w_qkv, qg, kg, tabs, *, tm, cka, ckb):
    B, S, D = x.shape
    n_tiles = S // tm
    per = tm // POOL_HALO
    last = S // POOL_HALO - 1
    row = lambda b, i: (b, i, 0)
    halo = lambda f: pl.BlockSpec((1, POOL_HALO, D), f)
    x_specs = [
        pl.BlockSpec((1, tm, D), row),
        halo(lambda b, i: (b, jnp.maximum(i * per - 1, 0), 0)),
        halo(lambda b, i: (b, N_META // POOL_HALO - 1, 0)),
        halo(lambda b, i: (b, jnp.minimum((i + 1) * per, last), 0)),
    ]
    kernel = functools.partial(_proj_real_kernel, tm=tm, n_tiles=n_tiles, seq_len=N_META + S, cka=cka, ckb=ckb)
    return _proj_call(kernel, (B, n_tiles), x_specs, (x, x, xm, x), w_qkv, qg, kg, tabs, B, S, tm, cka, ckb,
                      row, lambda b, i: (i, 0))


def _proj_meta(xm, x, w_qkv, qg, kg, tabs):
    B, _, D = xm.shape
    row = lambda b: (b, 0, 0)
    x_specs = [pl.BlockSpec((1, META_ROWS, D), row), pl.BlockSpec((1, POOL_HALO, D), row)]
    kernel = functools.partial(_proj_meta_kernel, seq_len=N_META + x.shape[1])
    return _proj_call(kernel, (B,), x_specs, (xm, x), w_qkv, qg, kg, tabs, B, META_ROWS, META_ROWS,
                      META_ROWS, META_ROWS, row, lambda b: (0, 0))


def _attn_a_kernel(q_ref, k_ref, vt_ref, km_ref, vmt_ref, o_ref,
                   acc_ref, m_ref, l_ref, s_ref, p_ref, ms_ref, as_ref, ac_ref, qp_ref, *, tq, tk, n_chunks, packed):
    heads = 1 if packed else GROUP
    if packed:
        qp_ref[...] = jnp.zeros(qp_ref.shape, qp_ref.dtype)
        for h in range(GROUP):
            qp_ref[h * N_META:(h + 1) * N_META, :] = q_ref[0, 0:N_META, h * HEAD_DIM:(h + 1) * HEAD_DIM]

    def q_head(h):
        return qp_ref[...] if packed else q_ref[0, :, h * HEAD_DIM:(h + 1) * HEAD_DIM]

    def scores(c, slot, h):
        kc = k_ref[0, pl.ds(pl.multiple_of(c * tk, tk), tk), :]
        s = lax.dot_general(kc, q_head(h), _NT, preferred_element_type=F32)
        s_ref[slot, h] = s
        m_old = m_ref[h]
        m_new = jnp.maximum(m_old, jnp.max(s, axis=0, keepdims=True))
        m_ref[h] = m_new
        ms_ref[slot, h] = m_new
        as_ref[slot, h] = jnp.exp2(m_old - m_new)

    def weights(slot, h):
        a = as_ref[slot, h]
        m = ms_ref[slot, h]
        ac_ref[slot, h] = a
        part = jnp.zeros((8, tq), F32)
        for r in range(0, tk, WEIGHT_ROWS):
            p = jnp.exp2(s_ref[slot, h, r:r + WEIGHT_ROWS, :] - m)
            p_ref[slot, h, r:r + WEIGHT_ROWS, :] = p.astype(p_ref.dtype)
            for i in range(WEIGHT_ROWS // 8):
                part = part + p[i * 8:(i + 1) * 8, :]
        l_ref[h] = a * l_ref[h] + jnp.sum(part, axis=0, keepdims=True)

    def values(c, slot, h):
        pv = jnp.dot(vt_ref[0, c], p_ref[slot, h], preferred_element_type=F32)
        acc_ref[h] = ac_ref[slot, h] * acc_ref[h] + pv

    key_row = lax.broadcasted_iota(jnp.int32, (META_ROWS, tq), 0)
    s_meta = [lax.dot_general(km_ref[0], q_head(h), _NT, preferred_element_type=F32) for h in range(heads)]
    p_meta = []
    for h in range(heads):
        s = jnp.where(key_row < N_META, s_meta[h], NEG)
        m = jnp.max(s, axis=0, keepdims=True)
        p = jnp.exp2(s - m)
        m_ref[h] = m
        l_ref[h] = jnp.sum(p, axis=0, keepdims=True)
        p_meta.append(p.astype(MM_DTYPE))
    for h in range(heads):
        acc_ref[h] = jnp.dot(vmt_ref[0, 0], p_meta[h], preferred_element_type=F32)
    for h in range(heads):
        scores(0, 0, h)
    for h in range(heads):
        scores(1, 1, h)
        weights(0, h)

    def trip(c, parity, with_scores=True):
        for h in range(heads):
            if with_scores:
                scores(c + 1, 1 - parity, h)
            values(c - 1, 1 - parity, h)
            weights(parity, h)

    def body(j, carry):
        for u in range(LOOP_TRIPS):
            trip(LOOP_TRIPS * j + 1 + u, (1 + u) % 2)
        return carry

    looped = (n_chunks - 2) // LOOP_TRIPS * LOOP_TRIPS
    lax.fori_loop(0, looped // LOOP_TRIPS, body, 0)
    for c in range(looped + 1, n_chunks):
        trip(c, c % 2, with_scores=c + 1 < n_chunks)
    for h in range(heads):
        values(n_chunks - 1, (n_chunks - 1) % 2, h)
    if packed:
        o = (acc_ref[0] / l_ref[0]).T.astype(o_ref.dtype)
        o_ref[0] = jnp.zeros(o_ref.shape[1:], o_ref.dtype)
        for h in range(GROUP):
            o_ref[0, 0:N_META, h * HEAD_DIM:(h + 1) * HEAD_DIM] = o[h * N_META:(h + 1) * N_META, :]
        return
    for h in range(GROUP):
        o = acc_ref[h] / l_ref[h]
        o_ref[0, :, h * HEAD_DIM:(h + 1) * HEAD_DIM] = o.T.astype(o_ref.dtype)


def _attn_a(q, k, vt, km, vmt, *, tq, tk, packed=False):
    B, Tq, _ = q.shape
    S = k.shape[1]
    n_chunks = S // tk
    assert n_chunks % 2 == 0
    blocks = (_nbytes((tq, 512), MM_DTYPE) * 2 + 2 * _nbytes((S, HEAD_DIM), MM_DTYPE)
              + 2 * _nbytes((META_ROWS, HEAD_DIM), MM_DTYPE))
    heads = 1 if packed else GROUP
    stat = pltpu.VMEM((2, heads, 1, tq), F32)
    scratch = (heads * (_nbytes((HEAD_DIM, tq), F32) + 8 * _nbytes((8, tq), F32))
               + 2 * heads * (_nbytes((tk, tq), F32) + _nbytes((tk, tq), MM_DTYPE)))
    return pl.pallas_call(
        functools.partial(_attn_a_kernel, tq=tq, tk=tk, n_chunks=n_chunks, packed=packed),
        grid=(B, N_KV, Tq // tq),
        in_specs=[
            pl.BlockSpec((1, tq, GROUP * HEAD_DIM), lambda b, g, i: (b, i, g)),
            pl.BlockSpec((1, S, HEAD_DIM), lambda b, g, i: (b, 0, g)),
            pl.BlockSpec((1, n_chunks, HEAD_DIM, tk), lambda b, g, i: (b, 0, g, 0)),
            pl.BlockSpec((1, META_ROWS, HEAD_DIM), lambda b, g, i: (b, 0, g)),
            pl.BlockSpec((1, 1, HEAD_DIM, META_ROWS), lambda b, g, i: (b, 0, g, 0)),
        ],
        out_specs=pl.BlockSpec((1, tq, GROUP * HEAD_DIM), lambda b, g, i: (b, i, g)),
        out_shape=jax.ShapeDtypeStruct((B, Tq, N_HEADS * HEAD_DIM), MM_DTYPE),
        scratch_shapes=[
            pltpu.VMEM((heads, HEAD_DIM, tq), F32),
            pltpu.VMEM((heads, 1, tq), F32),
            pltpu.VMEM((heads, 1, tq), F32),
            pltpu.VMEM((2, heads, tk, tq), F32),
            pltpu.VMEM((2, heads, tk, tq), MM_DTYPE),
            stat, stat, stat,
            pltpu.VMEM((META_ROWS, HEAD_DIM), MM_DTYPE),
        ],
        compiler_params=pltpu.CompilerParams(
            dimension_semantics=("parallel", "parallel", "parallel"),
            vmem_limit_bytes=_vmem_limit(blocks, scratch, 10 * _nbytes((tk, tq), F32))),
        name="attn_global",
    )(q, k, vt, km, vmt)


def _win_attend(sink_ref, q_ref, o_ref, problems):
    g = pl.program_id(1)
    scores = [[lax.dot_general(k_all, q_ref[0, col0:col0 + width, h * HEAD_DIM:(h + 1) * HEAD_DIM], _NT,
                               preferred_element_type=F32) for h in range(GROUP)]
              for col0, width, k_all, _, _ in problems]
    for (col0, width, _, vt_all, mask), s_heads in zip(problems, scores):
        for h in range(GROUP):
            s = jnp.where(mask, s_heads[h], NEG)
            sink = sink_ref[g * GROUP + h] * LOG2E
            m = jnp.maximum(jnp.max(s, axis=0, keepdims=True), sink)
            p = jnp.exp2(s - m)
            den = jnp.sum(p, axis=0, keepdims=True) + jnp.exp2(sink - m)
            o = jnp.dot(vt_all, p.astype(MM_DTYPE), preferred_element_type=F32) / den
            o_ref[0, col0:col0 + width, h * HEAD_DIM:(h + 1) * HEAD_DIM] = o.T.astype(o_ref.dtype)


def _attn_b_kernel(sink_ref, q_ref, kp_ref, kc_ref, kn_ref, vp_ref, vc_ref, vn_ref, km_ref, vmt_ref, o_ref,
                   *, tq, n_tiles):
    i = pl.program_id(2)
    nb = tq // WIN_BLK
    n_sub = tq // WIN_SUB
    k_real = jnp.concatenate([kp_ref[0], kc_ref[0], kn_ref[0]], axis=0)
    vt_real = jnp.concatenate([vp_ref[0, 0]] + [vc_ref[0, j] for j in range(nb)] + [vn_ref[0, 0]], axis=1)
    n_keys = META_ROWS + 2 * WIN_BLK + WIN_SUB
    r = lax.broadcasted_iota(jnp.int32, (n_keys, WIN_SUB), 0)
    c = lax.broadcasted_iota(jnp.int32, (n_keys, WIN_SUB), 1)
    rel = r - c - (META_ROWS + WIN_BLK)
    in_win = (rel >= -WINDOW) & (rel <= WINDOW) & (r >= META_ROWS)
    problems = []
    for j in range(n_sub):
        span = slice(j * WIN_SUB, j * WIN_SUB + WIN_SUB + 2 * WIN_BLK)
        k_all = jnp.concatenate([km_ref[0], k_real[span, :]], axis=0)
        vt_all = jnp.concatenate([vmt_ref[0, 0], vt_real[:, span]], axis=1)
        valid = in_win
        if j == 0:
            valid = valid & (r >= META_ROWS + jnp.where(i == 0, WIN_BLK, 0))
        if j == n_sub - 1:
            valid = valid & (r < n_keys - jnp.where(i == n_tiles - 1, WIN_BLK, 0))
        problems.append((j * WIN_SUB, WIN_SUB, k_all, vt_all, (r < N_META) | valid))
    _win_attend(sink_ref, q_ref, o_ref, problems)


def _attn_b_meta_kernel(sink_ref, q_ref, k0_ref, v0_ref, km_ref, vmt_ref, o_ref):
    k_all = jnp.concatenate([km_ref[0], k0_ref[0]], axis=0)
    vt_all = jnp.concatenate([vmt_ref[0, 0], v0_ref[0, 0]], axis=1)
    r = lax.broadcasted_iota(jnp.int32, (META_ROWS + WIN_BLK, META_ROWS), 0)
    c = lax.broadcasted_iota(jnp.int32, (META_ROWS + WIN_BLK, META_ROWS), 1)
    mask = (r < N_META) | ((r >= META_ROWS) & (r - META_ROWS + N_META - c <= WINDOW))
    _win_attend(sink_ref, q_ref, o_ref, [(0, META_ROWS, k_all, vt_all, mask)])


def _attn_b(sink, q, k, vt, km, vmt, *, tq):
    B, S, _ = q.shape
    n_tiles = S // tq
    nb = tq // WIN_BLK
    last_blk = S // WIN_BLK - 1
    kblk = lambda shape, f: pl.BlockSpec(shape, f)
    n_keys = META_ROWS + 2 * WIN_BLK + tq
    blocks = 2 * _nbytes((tq, 512), MM_DTYPE) + 2 * _nbytes((n_keys, HEAD_DIM), MM_DTYPE)
    grid_spec = pltpu.PrefetchScalarGridSpec(
        num_scalar_prefetch=1,
        grid=(B, N_KV, n_tiles),
        in_specs=[
            pl.BlockSpec((1, tq, GROUP * HEAD_DIM), lambda b, g, i, s: (b, i, g)),
            kblk((1, WIN_BLK, HEAD_DIM), lambda b, g, i, s: (b, jnp.maximum(i * nb - 1, 0), g)),
            kblk((1, tq, HEAD_DIM), lambda b, g, i, s: (b, i, g)),
            kblk((1, WIN_BLK, HEAD_DIM), lambda b, g, i, s: (b, jnp.minimum((i + 1) * nb, last_blk), g)),
            kblk((1, 1, HEAD_DIM, WIN_BLK), lambda b, g, i, s: (b, jnp.maximum(i * nb - 1, 0), g, 0)),
            kblk((1, nb, HEAD_DIM, WIN_BLK), lambda b, g, i, s: (b, i, g, 0)),
            kblk((1, 1, HEAD_DIM, WIN_BLK), lambda b, g, i, s: (b, jnp.minimum((i + 1) * nb, last_blk), g, 0)),
            kblk((1, META_ROWS, HEAD_DIM), lambda b, g, i, s: (b, 0, g)),
            kblk((1, 1, HEAD_DIM, META_ROWS), lambda b, g, i, s: (b, 0, g, 0)),
        ],
        out_specs=pl.BlockSpec((1, tq, GROUP * HEAD_DIM), lambda b, g, i, s: (b, i, g)),
    )
    return pl.pallas_call(
        functools.partial(_attn_b_kernel, tq=tq, n_tiles=n_tiles),
        grid_spec=grid_spec,
        out_shape=jax.ShapeDtypeStruct((B, S, N_HEADS * HEAD_DIM), MM_DTYPE),
        compiler_params=pltpu.CompilerParams(
            dimension_semantics=("parallel", "parallel", "parallel"),
            vmem_limit_bytes=_vmem_limit(blocks, 0, 8 * _nbytes((n_keys, tq), F32))),
        name="attn_window",
    )(sink, q, k, k, k, vt, vt, vt, km, vmt)


def _attn_b_meta(sink, qm, k, vt, km, vmt):
    B = qm.shape[0]
    blocks = 2 * _nbytes((META_ROWS, 512), MM_DTYPE) + 4 * _nbytes((META_ROWS, HEAD_DIM), MM_DTYPE)
    grid_spec = pltpu.PrefetchScalarGridSpec(
        num_scalar_prefetch=1,
        grid=(B, N_KV),
        in_specs=[
            pl.BlockSpec((1, META_ROWS, GROUP * HEAD_DIM), lambda b, g, s: (b, 0, g)),
            pl.BlockSpec((1, WIN_BLK, HEAD_DIM), lambda b, g, s: (b, 0, g)),
            pl.BlockSpec((1, 1, HEAD_DIM, WIN_BLK), lambda b, g, s: (b, 0, g, 0)),
            pl.BlockSpec((1, META_ROWS, HEAD_DIM), lambda b, g, s: (b, 0, g)),
            pl.BlockSpec((1, 1, HEAD_DIM, META_ROWS), lambda b, g, s: (b, 0, g, 0)),
        ],
        out_specs=pl.BlockSpec((1, META_ROWS, GROUP * HEAD_DIM), lambda b, g, s: (b, 0, g)),
    )
    return pl.pallas_call(
        _attn_b_meta_kernel,
        grid_spec=grid_spec,
        out_shape=jax.ShapeDtypeStruct((B, META_ROWS, N_HEADS * HEAD_DIM), MM_DTYPE),
        compiler_params=pltpu.CompilerParams(
            dimension_semantics=("parallel", "parallel"),
            vmem_limit_bytes=_vmem_limit(blocks, 0, 8 * _nbytes((2 * META_ROWS, META_ROWS), F32))),
        name="attn_window_meta",
    )(sink, qm, k, vt, km, vmt)


def _merge_kernel(h_ref, oa_ref, ob_ref, d_ref, wg_ref, wa_ref, wb_ref, pw_ref, ps_ref, wo_ref, g_ref, b_ref,
                  out_ref, *, tm):
    def gate(hb, j):
        return jax.nn.sigmoid(jnp.dot(hb, wg_ref[:, j * D_MODEL:(j + 1) * D_MODEL], preferred_element_type=F32))

    sub = min(tm, SUB_ROWS)
    for r in range(0, tm, sub):
        rows = slice(r, r + sub)
        h = h_ref[0, rows, :]
        hb = h.astype(MM_DTYPE)
        ya = jnp.dot(oa_ref[0, rows, :], wa_ref[...], preferred_element_type=F32)
        yb = jnp.dot(ob_ref[0, rows, :], wb_ref[...], preferred_element_type=F32)
        yc = jnp.concatenate(
            [jnp.dot(d_ref[0, rows, gi * POOL_DIM:(gi + 1) * POOL_DIM], pw_ref[gi], preferred_element_type=F32)
             for gi in range(len(POOL_WINDOWS))], axis=1) * ps_ref[...]
        merged = gate(hb, 0) * ya + gate(hb, 1) * yb + gate(hb, 2) * yc
        y = jnp.dot(merged.astype(MM_DTYPE), wo_ref[...], preferred_element_type=F32)
        out_ref[0, rows, :] = _layer_norm(ALPHA * h + y, g_ref[...], b_ref[...])


def _merge(h, oa, ob, d, wg, wa, wb, pw, ps, wo, g, b, *, tm):
    B, T, D = h.shape
    row = lambda bb, i: (bb, i, 0)
    vec = pl.BlockSpec((1, D), lambda bb, i: (0, 0))
    wbytes = sum(_nbytes(w.shape, MM_DTYPE) for w in (wg, wa, wb, pw, wo))
    blocks = 2 * _nbytes((tm, D), F32) + 3 * _nbytes((tm, D), MM_DTYPE)
    return pl.pallas_call(
        functools.partial(_merge_kernel, tm=tm),
        grid=(B, T // tm),
        in_specs=[pl.BlockSpec((1, tm, D), row)] * 4
        + [_resident(wg.shape), _resident(wa.shape), _resident(wb.shape), _resident(pw.shape), vec,
           _resident(wo.shape), vec, vec],
        out_specs=pl.BlockSpec((1, tm, D), row),
        out_shape=jax.ShapeDtypeStruct((B, T, D), F32),
        compiler_params=pltpu.CompilerParams(
            dimension_semantics=("parallel", "parallel"),
            vmem_limit_bytes=_vmem_limit(blocks, wbytes, 12 * _nbytes((min(tm, SUB_ROWS), D), F32))),
        name="merge",
    )(h, oa, ob, d, wg, wa, wb, pw, ps, wo, g, b)


def _ffn_kernel(h_ref, wu_ref, wd_ref, g_ref, b_ref, out_ref, act_ref, *, tm):
    sub = min(tm, SUB_ROWS)
    for r in range(0, tm, sub):
        h = h_ref[0, r:r + sub, :]
        hb = h.astype(MM_DTYPE)
        for c in range(D_FF // FF_CHUNK):
            c0 = c * FF_CHUNK
            gate = jnp.dot(hb, wu_ref[:, c0:c0 + FF_CHUNK], preferred_element_type=F32)
            up = jnp.dot(hb, wu_ref[:, D_FF + c0:D_FF + c0 + FF_CHUNK], preferred_element_type=F32)
            act_ref[r:r + sub, c0:c0 + FF_CHUNK] = (jax.nn.silu(gate) * up).astype(act_ref.dtype)
        f = jnp.dot(act_ref[r:r + sub, :], wd_ref[...], preferred_element_type=F32)
        out_ref[0, r:r + sub, :] = _layer_norm(ALPHA * h + f, g_ref[...], b_ref[...])


def _ffn(h, wu, wd, g, b, *, tm):
    B, T, D = h.shape
    row = lambda bb, i: (bb, i, 0)
    vec = pl.BlockSpec((1, D), lambda bb, i: (0, 0))
    wbytes = _nbytes(wu.shape, MM_DTYPE) + _nbytes(wd.shape, MM_DTYPE)
    return pl.pallas_call(
        functools.partial(_ffn_kernel, tm=tm),
        grid=(B, T // tm),
        in_specs=[pl.BlockSpec((1, tm, D), row), _resident(wu.shape), _resident(wd.shape), vec, vec],
        out_specs=pl.BlockSpec((1, tm, D), row),
        out_shape=jax.ShapeDtypeStruct((B, T, D), F32),
        scratch_shapes=[pltpu.VMEM((tm, D_FF), MM_DTYPE)],
        compiler_params=pltpu.CompilerParams(
            dimension_semantics=("parallel", "parallel"),
            vmem_limit_bytes=_vmem_limit(2 * _nbytes((tm, D), F32), wbytes + _nbytes((tm, D_FF), MM_DTYPE),
                                         6 * _nbytes((tm, D), F32))),
        name="ffn",
    )(h, wu, wd, g, b)


def _rope_tables(row, col, pos):
    def cs(p, dim):
        inv = ROPE_THETA ** (-jnp.arange(0, dim, 2, dtype=F32) / dim)
        ang = p.astype(F32)[:, None] * inv[None, :]
        return jnp.cos(ang), jnp.sin(ang)

    cr, sr = cs(row, HEAD_DIM // 2)
    cc, sc = cs(col, HEAD_DIM // 2)
    cp, sp = cs(pos, HEAD_DIM)
    ca = jnp.concatenate([cr, cr, cc, cc], axis=1)
    sa = jnp.concatenate([-sr, sr, -sc, sc], axis=1)
    cb = jnp.concatenate([cp, cp], axis=1)
    sb = jnp.concatenate([-sp, sp], axis=1)
    return ca, sa, cb, sb


def _real_tables(S):
    t = jnp.arange(S, dtype=jnp.int32)
    return _rope_tables(t // GRID_W, t % GRID_W, t + N_META)


def _meta_tables():
    i = jnp.arange(META_ROWS, dtype=jnp.int32)
    return _rope_tables(-jnp.ones((META_ROWS,), jnp.int32), i, i)


def _layer(xr, xm, p, tabs_r, tabs_m):
    qa, ka, va, qb, kb, vb, d = _proj(xr, xm, p["w_qkv"], p["qg"], p["kg"], tabs_r,
                                      tm=ROW_TILE, cka=ATTN_TK, ckb=WIN_BLK)
    qam, kam, vam, qbm, kbm, vbm, dm = _proj_meta(xm, xr, p["w_qkv"], p["qg"], p["kg"], tabs_m)
    oa = _attn_a(qa, ka, va, kam, vam, tq=ATTN_TQ, tk=ATTN_TK)
    oam = _attn_a(qam, ka, va, kam, vam, tq=META_ROWS, tk=ATTN_TK, packed=True)
    ob = _attn_b(p["sink"], qb, kb, vb, kbm, vbm, tq=WIN_TQ)
    obm = _attn_b_meta(p["sink"], qbm, kb, vb, kbm, vbm)
    merge_w = (p["wg"], p["wa"], p["wb"], p["pw"], p["ps"], p["wo"], p["ln1_g"], p["ln1_b"])
    hr = _merge(xr, oa, ob, d, *merge_w, tm=DENSE_TILE)
    hm = _merge(xm, oam, obm, dm, *merge_w, tm=META_ROWS)
    ffn_w = (p["wu"], p["wd"], p["ln2_g"], p["ln2_b"])
    return _ffn(hr, *ffn_w, tm=DENSE_TILE), _ffn(hm, *ffn_w, tm=META_ROWS)


def _trunk(x, meta_tokens, layers, tabs_r, tabs_m):
    B, S, D = x.shape
    xm = jnp.zeros((B, META_ROWS, D), x.dtype).at[:, :N_META].set(meta_tokens.astype(x.dtype)[None])
    xr = x
    for p in layers:
        xr, xm = _layer(xr, xm, p, tabs_r, tabs_m)
    return xr


def _layer_params(w_in, q_norm_g, k_norm_g, sink_logit, pool_w, pool_scale,
                  w_branch_a, w_branch_b, w_out, ln1_g, ln1_b, w_up, w_down, ln2_g, ln2_b):
    mm = lambda w: w.astype(MM_DTYPE)
    vec = lambda v: v.astype(F32).reshape(1, -1)
    layers = []
    for l in range(w_in.shape[0]):
        layers.append(dict(
            w_qkv=mm(w_in[l, :, :QKV_WIDTH]), wg=mm(w_in[l, :, QKV_WIDTH:]),
            qg=vec(q_norm_g[l]) * QK_SCALE, kg=vec(k_norm_g[l]), sink=sink_logit[l].astype(F32),
            pw=mm(pool_w[l]), ps=vec(pool_scale[l]),
            wa=mm(w_branch_a[l]), wb=mm(w_branch_b[l]), wo=mm(w_out[l]),
            ln1_g=vec(ln1_g[l]), ln1_b=vec(ln1_b[l]),
            wu=mm(w_up[l]), wd=mm(w_down[l]),
            ln2_g=vec(ln2_g[l]), ln2_b=vec(ln2_b[l]),
        ))
    return layers


def kernel(x_prompt, x_sample, meta_tokens, w_in, q_norm_g, k_norm_g, sink_logit, pool_w, pool_scale,
           w_branch_a, w_branch_b, w_out, ln1_g, ln1_b, w_up, w_down, ln2_g, ln2_b):
    layers = _layer_params(w_in, q_norm_g, k_norm_g, sink_logit, pool_w, pool_scale,
                           w_branch_a, w_branch_b, w_out, ln1_g, ln1_b, w_up, w_down, ln2_g, ln2_b)
    tabs_m = _meta_tables()
    tabs_s = _real_tables(x_sample.shape[1])
    tabs_p = tuple(t[:x_prompt.shape[1]] for t in tabs_s)
    y_prompt = _trunk(x_prompt, meta_tokens, layers, tabs_p, tabs_m)
    y_sample = _trunk(x_sample, meta_tokens, layers, tabs_s, tabs_m)
    return (y_prompt, y_sample)
```

```python
import functools

import jax
import jax.numpy as jnp
from jax import lax
from jax.experimental import pallas as pl
from jax.experimental.pallas import tpu as pltpu

D_MODEL = 1024
HEAD_DIM = 128
N_HEADS = 8
N_KV = 2
GROUP = N_HEADS // N_KV
WINDOW = 128
N_META = 16
META_ROWS = 128
GRID_W = 64
ROPE_THETA = 10000.0
POOL_WINDOWS = (2, 4, 8, 16)
POOL_DIM = D_MODEL // len(POOL_WINDOWS)
POOL_HALO = 8
D_FF = 2816
DEPTH = 2
ALPHA = (2 * DEPTH) ** 0.25
NEG = -1e30
QKV_WIDTH = 4096
LOG2E = 1.4426950408889634
QK_SCALE = HEAD_DIM ** -0.5 * LOG2E

MM_DTYPE = jnp.bfloat16
F32 = jnp.float32

V7X_VMEM_BYTES = 64 * 1024 * 1024
VMEM_CAP = V7X_VMEM_BYTES - 6 * 1024 * 1024

ROW_TILE = 1024
DENSE_TILE = 1024
ATTN_TQ = 512
ATTN_TK = 512
WIN_TQ = 1024
WIN_SUB = 256
WIN_BLK = 128
FF_CHUNK = 256
SUB_ROWS = 256
WEIGHT_ROWS = 16
LOOP_TRIPS = 2

_NT = (((1,), (1,)), ((), ()))


def _vmem_limit(block_bytes, scratch_bytes=0, temp_bytes=0):
    est = 2 * block_bytes + scratch_bytes + temp_bytes + (8 << 20)
    return int(min(max(est, 16 << 20), VMEM_CAP))


def _nbytes(shape, dtype):
    n = 1
    for s in shape:
        n *= s
    return n * jnp.dtype(dtype).itemsize


def _resident(shape):
    zeros = (0,) * len(shape)
    return pl.BlockSpec(shape, lambda *_: zeros, pipeline_mode=pl.Buffered(1))


def _layer_norm(x, g, b):
    mu = jnp.mean(x, axis=-1, keepdims=True)
    xc = x - mu
    var = jnp.mean(xc * xc, axis=-1, keepdims=True)
    return xc * lax.rsqrt(var + 1e-5) * g + b


def _pool_group(ext_ref, d_ref, pos, seq_len, tm, gi):
    n = tm + 2 * POOL_HALO
    win = POOL_WINDOWS[gi]
    c0 = gi * POOL_DIM

    def ahead(x, k):
        return pltpu.roll(x, n - k, 0) if k % n else x

    s = ext_ref[:, c0:c0 + POOL_DIM]
    span = 1
    while span < win:
        s = s + ahead(s, span)
        span *= 2
    acc = ahead(s, POOL_HALO - win // 2)[0:tm]
    lo = jnp.maximum(pos - win // 2, 0)
    hi = jnp.minimum(pos - win // 2 + win, seq_len)
    mean = acc / (hi - lo).astype(F32)
    centre = ext_ref[POOL_HALO:POOL_HALO + tm, c0:c0 + POOL_DIM]
    d_ref[0, :, c0:c0 + POOL_DIM] = (mean - centre).astype(d_ref.dtype)


def _proj_body(x_ref, x_ext, pos, seq_len, w_ref, qg_ref, kg_ref, ca_ref, sa_ref, cb_ref, sb_ref,
               qa_ref, ka_ref, va_ref, qb_ref, kb_ref, vb_ref, d_ref, ext_ref, *, tm, cka, ckb):
    xe = x_ext.astype(MM_DTYPE)
    for blk in range(2):
        c0 = 3072 + blk * 512
        ext_ref[:, blk * 512:(blk + 1) * 512] = jnp.dot(xe, w_ref[:, c0:c0 + 512], preferred_element_type=F32)
    pool_steps = [functools.partial(_pool_group, ext_ref, d_ref, pos, seq_len, tm, gi)
                  for gi in range(len(POOL_WINDOWS))]

    def pool_step():
        if pool_steps:
            pool_steps.pop(0)()

    sub = min(tm, SUB_ROWS)
    qg = qg_ref[...]
    kg = kg_ref[...]
    lane = lax.broadcasted_iota(jnp.int32, (sub, HEAD_DIM), 1)
    low_quarter = (lane % 64) < 32

    def rms(x, g):
        return x * lax.rsqrt(jnp.mean(x * x, axis=-1, keepdims=True) + 1e-6) * g

    def head(u, j):
        return u[:, j * HEAD_DIM:(j + 1) * HEAD_DIM]

    for r in range(0, tm, sub):
        rows = slice(r, r + sub)
        xb = x_ref[0, rows, :].astype(MM_DTYPE)
        ca = ca_ref[rows, :]
        sa = sa_ref[rows, :]
        cb = cb_ref[rows, :]
        sb = sb_ref[rows, :]

        def rope_a(x):
            partner = jnp.where(low_quarter, pltpu.roll(x, 96, 1), pltpu.roll(x, 32, 1))
            return x * ca + partner * sa

        def rope_b(x):
            return x * cb + pltpu.roll(x, 64, 1) * sb

        def cols(c0, n):
            return jnp.dot(xb, w_ref[:, c0:c0 + n], preferred_element_type=F32)

        def store_vt(v_ref, v, ck):
            step = min(sub, ck)
            for o in range(0, sub, step):
                j, lane0 = (r + o) // ck, (r + o) % ck
                v_ref[0, j, :, lane0:lane0 + step] = v[o:o + step, :].T.astype(v_ref.dtype)

        for blk in range(2):
            u = cols(blk * 512, 512)
            for j in range(4):
                h = blk * 4 + j
                qa_ref[0, rows, h * HEAD_DIM:(h + 1) * HEAD_DIM] = rope_a(rms(head(u, j), qg)).astype(qa_ref.dtype)
        pool_step()
        u = cols(1024, 512)
        for j in range(2):
            ka_ref[0, rows, j * HEAD_DIM:(j + 1) * HEAD_DIM] = rope_a(rms(head(u, j), kg)).astype(ka_ref.dtype)
        store_vt(va_ref, u[:, 256:512], cka)
        for blk in range(2):
            u = cols(1536 + blk * 512, 512)
            for j in range(4):
                h = blk * 4 + j
                qb_ref[0, rows, h * HEAD_DIM:(h + 1) * HEAD_DIM] = (
                    rope_b(head(u, j)) * QK_SCALE).astype(qb_ref.dtype)
        pool_step()
        u = cols(2560, 512)
        for j in range(2):
            kb_ref[0, rows, j * HEAD_DIM:(j + 1) * HEAD_DIM] = rope_b(head(u, j)).astype(kb_ref.dtype)
        store_vt(vb_ref, u[:, 256:512], ckb)
    while pool_steps:
        pool_step()


def _proj_real_kernel(x_ref, prev_ref, meta_ref, next_ref, *rest, tm, n_tiles, seq_len, cka, ckb):
    i = pl.program_id(1)
    x_ext = jnp.concatenate([jnp.where(i == 0, meta_ref[0], prev_ref[0]), x_ref[0],
                             jnp.where(i == n_tiles - 1, 0.0, next_ref[0])], axis=0)
    pos = N_META + i * tm + lax.broadcasted_iota(jnp.int32, (tm, 1), 0)
    _proj_body(x_ref, x_ext, pos, seq_len, *rest, tm=tm, cka=cka, ckb=ckb)


def _proj_meta_kernel(x_ref, first_ref, *rest, seq_len):
    x_ext = jnp.concatenate([jnp.zeros((POOL_HALO, D_MODEL), F32), x_ref[0, 0:N_META, :], first_ref[0],
                             jnp.zeros((META_ROWS - N_META, D_MODEL), F32)], axis=0)
    pos = lax.broadcasted_iota(jnp.int32, (META_ROWS, 1), 0)
    _proj_body(x_ref, x_ext, pos, seq_len, *rest, tm=META_ROWS, cka=META_ROWS, ckb=META_ROWS)


def _proj_call(kernel, grid, x_specs, x_args, w_qkv, qg, kg, tabs, B, T, tm, cka, ckb, row, tab_index):
    D = D_MODEL
    tab = pl.BlockSpec((tm, HEAD_DIM), tab_index)
    small = pl.BlockSpec((1, HEAD_DIM), lambda *_: (0, 0))
    chunked = lambda f: (lambda *idx: row(*idx)[:2] + (0, 0))
    out_shape = (
        jax.ShapeDtypeStruct((B, T, 1024), MM_DTYPE),
        jax.ShapeDtypeStruct((B, T, 256), MM_DTYPE),
        jax.ShapeDtypeStruct((B, T // cka, 256, cka), MM_DTYPE),
        jax.ShapeDtypeStruct((B, T, 1024), MM_DTYPE),
        jax.ShapeDtypeStruct((B, T, 256), MM_DTYPE),
        jax.ShapeDtypeStruct((B, T // ckb, 256, ckb), MM_DTYPE),
        jax.ShapeDtypeStruct((B, T, 1024), MM_DTYPE),
    )
    out_specs = (
        pl.BlockSpec((1, tm, 1024), row),
        pl.BlockSpec((1, tm, 256), row),
        pl.BlockSpec((1, tm // cka, 256, cka), chunked(row)),
        pl.BlockSpec((1, tm, 1024), row),
        pl.BlockSpec((1, tm, 256), row),
        pl.BlockSpec((1, tm // ckb, 256, ckb), chunked(row)),
        pl.BlockSpec((1, tm, 1024), row),
    )
    blocks = (_nbytes((tm, D), F32) + 4 * _nbytes((tm, HEAD_DIM), F32) + _nbytes((tm, 4096), MM_DTYPE))
    ext_bytes = _nbytes((tm + 2 * POOL_HALO, D), F32)
    return pl.pallas_call(
        kernel,
        grid=grid,
        in_specs=x_specs + [_resident(w_qkv.shape), small, small, tab, tab, tab, tab],
        out_specs=out_specs,
        out_shape=out_shape,
        scratch_shapes=[pltpu.VMEM((tm + 2 * POOL_HALO, D), F32)],
        compiler_params=pltpu.CompilerParams(
            dimension_semantics=("parallel",) * len(grid),
            vmem_limit_bytes=_vmem_limit(blocks, _nbytes(w_qkv.shape, MM_DTYPE) + ext_bytes,
                                         2 * ext_bytes + 8 * _nbytes((tm, 512), F32))),
        name="proj",
    )(*x_args, w_qkv, qg, kg, *tabs)


def _proj(x, xm, w_qkv, qg, kg, tabs, *, tm, cka, ckb):
    B, S, D = x.shape
    n_tiles = S // tm
    per = tm // POOL_HALO
    last = S // POOL_HALO - 1
    row = lambda b, i: (b, i, 0)
    halo = lambda f: pl.BlockSpec((1, POOL_HALO, D), f)
    x_specs = [
        pl.BlockSpec((1, tm, D), row),
        halo(lambda b, i: (b, jnp.maximum(i * per - 1, 0), 0)),
        halo(lambda b, i: (b, N_META // POOL_HALO - 1, 0)),
        halo(lambda b, i: (b, jnp.minimum((i + 1) * per, last), 0)),
    ]
    kernel = functools.partial(_proj_real_kernel, tm=tm, n_tiles=n_tiles, seq_len=N_META + S, cka=cka, ckb=ckb)
    return _proj_call(kernel, (B, n_tiles), x_specs, (x, x, xm, x), w_qkv, qg, kg, tabs, B, S, tm, cka, ckb,
                      row, lambda b, i: (i, 0))


def _proj_meta(xm, x, w_qkv, qg, kg, tabs):
    B, _, D = xm.shape
    row = lambda b: (b, 0, 0)
    x_specs = [pl.BlockSpec((1, META_ROWS, D), row), pl.BlockSpec((1, POOL_HALO, D), row)]
    kernel = functools.partial(_proj_meta_kernel, seq_len=N_META + x.shape[1])
    return _proj_call(kernel, (B,), x_specs, (xm, x), w_qkv, qg, kg, tabs, B, META_ROWS, META_ROWS,
                      META_ROWS, META_ROWS, row, lambda b: (0, 0))


def _attn_a_kernel(q_ref, k_ref, vt_ref, km_ref, vmt_ref, o_ref,
                   acc_ref, m_ref, l_ref, s_ref, p_ref, ms_ref, as_ref, ac_ref, qp_ref, *, tq, tk, n_chunks, packed):
    heads = 1 if packed else GROUP
    if packed:
        qp_ref[...] = jnp.zeros(qp_ref.shape, qp_ref.dtype)
        for h in range(GROUP):
            qp_ref[h * N_META:(h + 1) * N_META, :] = q_ref[0, 0:N_META, h * HEAD_DIM:(h + 1) * HEAD_DIM]

    def q_head(h):
        return qp_ref[...] if packed else q_ref[0, :, h * HEAD_DIM:(h + 1) * HEAD_DIM]

    def scores(c, slot, h):
        kc = k_ref[0, pl.ds(pl.multiple_of(c * tk, tk), tk), :]
        s = lax.dot_general(kc, q_head(h), _NT, preferred_element_type=F32)
        s_ref[slot, h] = s
        m_old = m_ref[h]
        m_new = jnp.maximum(m_old, jnp.max(s, axis=0, keepdims=True))
        m_ref[h] = m_new
        ms_ref[slot, h] = m_new
        as_ref[slot, h] = jnp.exp2(m_old - m_new)

    def weights(slot, h):
        a = as_ref[slot, h]
        m = ms_ref[slot, h]
        ac_ref[slot, h] = a
        part = jnp.zeros((8, tq), F32)
        for r in range(0, tk, WEIGHT_ROWS):
            p = jnp.exp2(s_ref[slot, h, r:r + WEIGHT_ROWS, :] - m)
            p_ref[slot, h, r:r + WEIGHT_ROWS, :] = p.astype(p_ref.dtype)
            for i in range(WEIGHT_ROWS // 8):
                part = part + p[i * 8:(i + 1) * 8, :]
        l_ref[h] = a * l_ref[h] + jnp.sum(part, axis=0, keepdims=True)

    def values(c, slot, h):
        pv = jnp.dot(vt_ref[0, c], p_ref[slot, h], preferred_element_type=F32)
        acc_ref[h] = ac_ref[slot, h] * acc_ref[h] + pv

    key_row = lax.broadcasted_iota(jnp.int32, (META_ROWS, tq), 0)
    s_meta = [lax.dot_general(km_ref[0], q_head(h), _NT, preferred_element_type=F32) for h in range(heads)]
    p_meta = []
    for h in range(heads):
        s = jnp.where(key_row < N_META, s_meta[h], NEG)
        m = jnp.max(s, axis=0, keepdims=True)
        p = jnp.exp2(s - m)
        m_ref[h] = m
        l_ref[h] = jnp.sum(p, axis=0, keepdims=True)
        p_meta.append(p.astype(MM_DTYPE))
    for h in range(heads):
        acc_ref[h] = jnp.dot(vmt_ref[0, 0], p_meta[h], preferred_element_type=F32)
    for h in range(heads):
        scores(0, 0, h)
    for h in range(heads):
        scores(1, 1, h)
        weights(0, h)

    def trip(c, parity, with_scores=True):
        for h in range(heads):
            if with_scores:
                scores(c + 1, 1 - parity, h)
            values(c - 1, 1 - parity, h)
            weights(parity, h)

    def body(j, carry):
        for u in range(LOOP_TRIPS):
            trip(LOOP_TRIPS * j + 1 + u, (1 + u) % 2)
        return carry

    looped = (n_chunks - 2) // LOOP_TRIPS * LOOP_TRIPS
    lax.fori_loop(0, looped // LOOP_TRIPS, body, 0)
    for c in range(looped + 1, n_chunks):
        trip(c, c % 2, with_scores=c + 1 < n_chunks)
    for h in range(heads):
        values(n_chunks - 1, (n_chunks - 1) % 2, h)
    if packed:
        o = (acc_ref[0] / l_ref[0]).T.astype(o_ref.dtype)
        o_ref[0] = jnp.zeros(o_ref.shape[1:], o_ref.dtype)
        for h in range(GROUP):
            o_ref[0, 0:N_META, h * HEAD_DIM:(h + 1) * HEAD_DIM] = o[h * N_META:(h + 1) * N_META, :]
        return
    for h in range(GROUP):
        o = acc_ref[h] / l_ref[h]
        o_ref[0, :, h * HEAD_DIM:(h + 1) * HEAD_DIM] = o.T.astype(o_ref.dtype)


def _attn_a(q, k, vt, km, vmt, *, tq, tk, packed=False):
    B, Tq, _ = q.shape
    S = k.shape[1]
    n_chunks = S // tk
    assert n_chunks % 2 == 0
    blocks = (_nbytes((tq, 512), MM_DTYPE) * 2 + 2 * _nbytes((S, HEAD_DIM), MM_DTYPE)
              + 2 * _nbytes((META_ROWS, HEAD_DIM), MM_DTYPE))
    heads = 1 if packed else GROUP
    stat = pltpu.VMEM((2, heads, 1, tq), F32)
    scratch = (heads * (_nbytes((HEAD_DIM, tq), F32) + 8 * _nbytes((8, tq), F32))
               + 2 * heads * (_nbytes((tk, tq), F32) + _nbytes((tk, tq), MM_DTYPE)))
    return pl.pallas_call(
        functools.partial(_attn_a_kernel, tq=tq, tk=tk, n_chunks=n_chunks, packed=packed),
        grid=(B, N_KV, Tq // tq),
        in_specs=[
            pl.BlockSpec((1, tq, GROUP * HEAD_DIM), lambda b, g, i: (b, i, g)),
            pl.BlockSpec((1, S, HEAD_DIM), lambda b, g, i: (b, 0, g)),
            pl.BlockSpec((1, n_chunks, HEAD_DIM, tk), lambda b, g, i: (b, 0, g, 0)),
            pl.BlockSpec((1, META_ROWS, HEAD_DIM), lambda b, g, i: (b, 0, g)),
            pl.BlockSpec((1, 1, HEAD_DIM, META_ROWS), lambda b, g, i: (b, 0, g, 0)),
        ],
        out_specs=pl.BlockSpec((1, tq, GROUP * HEAD_DIM), lambda b, g, i: (b, i, g)),
        out_shape=jax.ShapeDtypeStruct((B, Tq, N_HEADS * HEAD_DIM), MM_DTYPE),
        scratch_shapes=[
            pltpu.VMEM((heads, HEAD_DIM, tq), F32),
            pltpu.VMEM((heads, 1, tq), F32),
            pltpu.VMEM((heads, 1, tq), F32),
            pltpu.VMEM((2, heads, tk, tq), F32),
            pltpu.VMEM((2, heads, tk, tq), MM_DTYPE),
            stat, stat, stat,
            pltpu.VMEM((META_ROWS, HEAD_DIM), MM_DTYPE),
        ],
        compiler_params=pltpu.CompilerParams(
            dimension_semantics=("parallel", "parallel", "parallel"),
            vmem_limit_bytes=_vmem_limit(blocks, scratch, 10 * _nbytes((tk, tq), F32))),
        name="attn_global",
    )(q, k, vt, km, vmt)


def _win_attend(sink_ref, q_ref, o_ref, problems):
    g = pl.program_id(1)
    scores = [[lax.dot_general(k_all, q_ref[0, col0:col0 + width, h * HEAD_DIM:(h + 1) * HEAD_DIM], _NT,
                               preferred_element_type=F32) for h in range(GROUP)]
              for col0, width, k_all, _, _ in problems]
    for (col0, width, _, vt_all, mask), s_heads in zip(problems, scores):
        for h in range(GROUP):
            s = jnp.where(mask, s_heads[h], NEG)
            sink = sink_ref[g * GROUP + h] * LOG2E
            m = jnp.maximum(jnp.max(s, axis=0, keepdims=True), sink)
            p = jnp.exp2(s - m)
            den = jnp.sum(p, axis=0, keepdims=True) + jnp.exp2(sink - m)
            o = jnp.dot(vt_all, p.astype(MM_DTYPE), preferred_element_type=F32) / den
            o_ref[0, col0:col0 + width, h * HEAD_DIM:(h + 1) * HEAD_DIM] = o.T.astype(o_ref.dtype)


def _attn_b_kernel(sink_ref, q_ref, kp_ref, kc_ref, kn_ref, vp_ref, vc_ref, vn_ref, km_ref, vmt_ref, o_ref,
                   *, tq, n_tiles):
    i = pl.program_id(2)
    nb = tq // WIN_BLK
    n_sub = tq // WIN_SUB
    k_real = jnp.concatenate([kp_ref[0], kc_ref[0], kn_ref[0]], axis=0)
    vt_real = jnp.concatenate([vp_ref[0, 0]] + [vc_ref[0, j] for j in range(nb)] + [vn_ref[0, 0]], axis=1)
    n_keys = META_ROWS + 2 * WIN_BLK + WIN_SUB
    r = lax.broadcasted_iota(jnp.int32, (n_keys, WIN_SUB), 0)
    c = lax.broadcasted_iota(jnp.int32, (n_keys, WIN_SUB), 1)
    rel = r - c - (META_ROWS + WIN_BLK)
    in_win = (rel >= -WINDOW) & (rel <= WINDOW) & (r >= META_ROWS)
    problems = []
    for j in range(n_sub):
        span = slice(j * WIN_SUB, j * WIN_SUB + WIN_SUB + 2 * WIN_BLK)
        k_all = jnp.concatenate([km_ref[0], k_real[span, :]], axis=0)
        vt_all = jnp.concatenate([vmt_ref[0, 0], vt_real[:, span]], axis=1)
        valid = in_win
        if j == 0:
            valid = valid & (r >= META_ROWS + jnp.where(i == 0, WIN_BLK, 0))
        if j == n_sub - 1:
            valid = valid & (r < n_keys - jnp.where(i == n_tiles - 1, WIN_BLK, 0))
        problems.append((j * WIN_SUB, WIN_SUB, k_all, vt_all, (r < N_META) | valid))
    _win_attend(sink_ref, q_ref, o_ref, problems)


def _attn_b_meta_kernel(sink_ref, q_ref, k0_ref, v0_ref, km_ref, vmt_ref, o_ref):
    k_all = jnp.concatenate([km_ref[0], k0_ref[0]], axis=0)
    vt_all = jnp.concatenate([vmt_ref[0, 0], v0_ref[0, 0]], axis=1)
    r = lax.broadcasted_iota(jnp.int32, (META_ROWS + WIN_BLK, META_ROWS), 0)
    c = lax.broadcasted_iota(jnp.int32, (META_ROWS + WIN_BLK, META_ROWS), 1)
    mask = (r < N_META) | ((r >= META_ROWS) & (r - META_ROWS + N_META - c <= WINDOW))
    _win_attend(sink_ref, q_ref, o_ref, [(0, META_ROWS, k_all, vt_all, mask)])


def _attn_b(sink, q, k, vt, km, vmt, *, tq):
    B, S, _ = q.shape
    n_tiles = S // tq
    nb = tq // WIN_BLK
    last_blk = S // WIN_BLK - 1
    kblk = lambda shape, f: pl.BlockSpec(shape, f)
    n_keys = META_ROWS + 2 * WIN_BLK + tq
    blocks = 2 * _nbytes((tq, 512), MM_DTYPE) + 2 * _nbytes((n_keys, HEAD_DIM), MM_DTYPE)
    grid_spec = pltpu.PrefetchScalarGridSpec(
        num_scalar_prefetch=1,
        grid=(B, N_KV, n_tiles),
        in_specs=[
            pl.BlockSpec((1, tq, GROUP * HEAD_DIM), lambda b, g, i, s: (b, i, g)),
            kblk((1, WIN_BLK, HEAD_DIM), lambda b, g, i, s: (b, jnp.maximum(i * nb - 1, 0), g)),
            kblk((1, tq, HEAD_DIM), lambda b, g, i, s: (b, i, g)),
            kblk((1, WIN_BLK, HEAD_DIM), lambda b, g, i, s: (b, jnp.minimum((i + 1) * nb, last_blk), g)),
            kblk((1, 1, HEAD_DIM, WIN_BLK), lambda b, g, i, s: (b, jnp.maximum(i * nb - 1, 0), g, 0)),
            kblk((1, nb, HEAD_DIM, WIN_BLK), lambda b, g, i, s: (b, i, g, 0)),
            kblk((1, 1, HEAD_DIM, WIN_BLK), lambda b, g, i, s: (b, jnp.minimum((i + 1) * nb, last_blk), g, 0)),
            kblk((1, META_ROWS, HEAD_DIM), lambda b, g, i, s: (b, 0, g)),
            kblk((1, 1, HEAD_DIM, META_ROWS), lambda b, g, i, s: (b, 0, g, 0)),
        ],
        out_specs=pl.BlockSpec((1, tq, GROUP * HEAD_DIM), lambda b, g, i, s: (b, i, g)),
    )
    return pl.pallas_call(
        functools.partial(_attn_b_kernel, tq=tq, n_tiles=n_tiles),
        grid_spec=grid_spec,
        out_shape=jax.ShapeDtypeStruct((B, S, N_HEADS * HEAD_DIM), MM_DTYPE),
        compiler_params=pltpu.CompilerParams(
            dimension_semantics=("parallel", "parallel", "parallel"),
            vmem_limit_bytes=_vmem_limit(blocks, 0, 8 * _nbytes((n_keys, tq), F32))),
        name="attn_window",
    )(sink, q, k, k, k, vt, vt, vt, km, vmt)


def _attn_b_meta(sink, qm, k, vt, km, vmt):
    B = qm.shape[0]
    blocks = 2 * _nbytes((META_ROWS, 512), MM_DTYPE) + 4 * _nbytes((META_ROWS, HEAD_DIM), MM_DTYPE)
    grid_spec = pltpu.PrefetchScalarGridSpec(
        num_scalar_prefetch=1,
        grid=(B, N_KV),
        in_specs=[
            pl.BlockSpec((1, META_ROWS, GROUP * HEAD_DIM), lambda b, g, s: (b, 0, g)),
            pl.BlockSpec((1, WIN_BLK, HEAD_DIM), lambda b, g, s: (b, 0, g)),
            pl.BlockSpec((1, 1, HEAD_DIM, WIN_BLK), lambda b, g, s: (b, 0, g, 0)),
            pl.BlockSpec((1, META_ROWS, HEAD_DIM), lambda b, g, s: (b, 0, g)),
            pl.BlockSpec((1, 1, HEAD_DIM, META_ROWS), lambda b, g, s: (b, 0, g, 0)),
        ],
        out_specs=pl.BlockSpec((1, META_ROWS, GROUP * HEAD_DIM), lambda b, g, s: (b, 0, g)),
    )
    return pl.pallas_call(
        _attn_b_meta_kernel,
        grid_spec=grid_spec,
        out_shape=jax.ShapeDtypeStruct((B, META_ROWS, N_HEADS * HEAD_DIM), MM_DTYPE),
        compiler_params=pltpu.CompilerParams(
            dimension_semantics=("parallel", "parallel"),
            vmem_limit_bytes=_vmem_limit(blocks, 0, 8 * _nbytes((2 * META_ROWS, META_ROWS), F32))),
        name="attn_window_meta",
    )(sink, qm, k, vt, km, vmt)


def _merge_kernel(h_ref, oa_ref, ob_ref, d_ref, wg_ref, wa_ref, wb_ref, pw_ref, ps_ref, wo_ref, g_ref, b_ref,
                  out_ref, *, tm):
    def gate(hb, j):
        return jax.nn.sigmoid(jnp.dot(hb, wg_ref[:, j * D_MODEL:(j + 1) * D_MODEL], preferred_element_type=F32))

    sub = min(tm, SUB_ROWS)
    for r in range(0, tm, sub):
        rows = slice(r, r + sub)
        h = h_ref[0, rows, :]
        hb = h.astype(MM_DTYPE)
        ya = jnp.dot(oa_ref[0, rows, :], wa_ref[...], preferred_element_type=F32)
        yb = jnp.dot(ob_ref[0, rows, :], wb_ref[...], preferred_element_type=F32)
        yc = jnp.concatenate(
            [jnp.dot(d_ref[0, rows, gi * POOL_DIM:(gi + 1) * POOL_DIM], pw_ref[gi], preferred_element_type=F32)
             for gi in range(len(POOL_WINDOWS))], axis=1) * ps_ref[...]
        merged = gate(hb, 0) * ya + gate(hb, 1) * yb + gate(hb, 2) * yc
        y = jnp.dot(merged.astype(MM_DTYPE), wo_ref[...], preferred_element_type=F32)
        out_ref[0, rows, :] = _layer_norm(ALPHA * h + y, g_ref[...], b_ref[...])


def _merge(h, oa, ob, d, wg, wa, wb, pw, ps, wo, g, b, *, tm):
    B, T, D = h.shape
    row = lambda bb, i: (bb, i, 0)
    vec = pl.BlockSpec((1, D), lambda bb, i: (0, 0))
    wbytes = sum(_nbytes(w.shape, MM_DTYPE) for w in (wg, wa, wb, pw, wo))
    blocks = 2 * _nbytes((tm, D), F32) + 3 * _nbytes((tm, D), MM_DTYPE)
    return pl.pallas_call(
        functools.partial(_merge_kernel, tm=tm),
        grid=(B, T // tm),
        in_specs=[pl.BlockSpec((1, tm, D), row)] * 4
        + [_resident(wg.shape), _resident(wa.shape), _resident(wb.shape), _resident(pw.shape), vec,
           _resident(wo.shape), vec, vec],
        out_specs=pl.BlockSpec((1, tm, D), row),
        out_shape=jax.ShapeDtypeStruct((B, T, D), F32),
        compiler_params=pltpu.CompilerParams(
            dimension_semantics=("parallel", "parallel"),
            vmem_limit_bytes=_vmem_limit(blocks, wbytes, 12 * _nbytes((min(tm, SUB_ROWS), D), F32))),
        name="merge",
    )(h, oa, ob, d, wg, wa, wb, pw, ps, wo, g, b)


def _ffn_kernel(h_ref, wu_ref, wd_ref, g_ref, b_ref, out_ref, act_ref, *, tm):
    sub = min(tm, SUB_ROWS)
    for r in range(0, tm, sub):
        h = h_ref[0, r:r + sub, :]
        hb = h.astype(MM_DTYPE)
        for c in range(D_FF // FF_CHUNK):
            c0 = c * FF_CHUNK
            gate = jnp.dot(hb, wu_ref[:, c0:c0 + FF_CHUNK], preferred_element_type=F32)
            up = jnp.dot(hb, wu_ref[:, D_FF + c0:D_FF + c0 + FF_CHUNK], preferred_element_type=F32)
            act_ref[r:r + sub, c0:c0 + FF_CHUNK] = (jax.nn.silu(gate) * up).astype(act_ref.dtype)
        f = jnp.dot(act_ref[r:r + sub, :], wd_ref[...], preferred_element_type=F32)
        out_ref[0, r:r + sub, :] = _layer_norm(ALPHA * h + f, g_ref[...], b_ref[...])


def _ffn(h, wu, wd, g, b, *, tm):
    B, T, D = h.shape
    row = lambda bb, i: (bb, i, 0)
    vec = pl.BlockSpec((1, D), lambda bb, i: (0, 0))
    wbytes = _nbytes(wu.shape, MM_DTYPE) + _nbytes(wd.shape, MM_DTYPE)
    return pl.pallas_call(
        functools.partial(_ffn_kernel, tm=tm),
        grid=(B, T // tm),
        in_specs=[pl.BlockSpec((1, tm, D), row), _resident(wu.shape), _resident(wd.shape), vec, vec],
        out_specs=pl.BlockSpec((1, tm, D), row),
        out_shape=jax.ShapeDtypeStruct((B, T, D), F32),
        scratch_shapes=[pltpu.VMEM((tm, D_FF), MM_DTYPE)],
        compiler_params=pltpu.CompilerParams(
            dimension_semantics=("parallel", "parallel"),
            vmem_limit_bytes=_vmem_limit(2 * _nbytes((tm, D), F32), wbytes + _nbytes((tm, D_FF), MM_DTYPE),
                                         6 * _nbytes((tm, D), F32))),
        name="ffn",
    )(h, wu, wd, g, b)


def _rope_tables(row, col, pos):
    def cs(p, dim):
        inv = ROPE_THETA ** (-jnp.arange(0, dim, 2, dtype=F32) / dim)
        ang = p.astype(F32)[:, None] * inv[None, :]
        return jnp.cos(ang), jnp.sin(ang)

    cr, sr = cs(row, HEAD_DIM // 2)
    cc, sc = cs(col, HEAD_DIM // 2)
    cp, sp = cs(pos, HEAD_DIM)
    ca = jnp.concatenate([cr, cr, cc, cc], axis=1)
    sa = jnp.concatenate([-sr, sr, -sc, sc], axis=1)
    cb = jnp.concatenate([cp, cp], axis=1)
    sb = jnp.concatenate([-sp, sp], axis=1)
    return ca, sa, cb, sb


def _real_tables(S):
    t = jnp.arange(S, dtype=jnp.int32)
    return _rope_tables(t // GRID_W, t % GRID_W, t + N_META)


def _meta_tables():
    i = jnp.arange(META_ROWS, dtype=jnp.int32)
    return _rope_tables(-jnp.ones((META_ROWS,), jnp.int32), i, i)


def _layer(xr, xm, p, tabs_r, tabs_m):
    qa, ka, va, qb, kb, vb, d = _proj(xr, xm, p["w_qkv"], p["qg"], p["kg"], tabs_r,
                                      tm=ROW_TILE, cka=ATTN_TK, ckb=WIN_BLK)
    qam, kam, vam, qbm, kbm, vbm, dm = _proj_meta(xm, xr, p["w_qkv"], p["qg"], p["kg"], tabs_m)
    oa = _attn_a(qa, ka, va, kam, vam, tq=ATTN_TQ, tk=ATTN_TK)
    oam = _attn_a(qam, ka, va, kam, vam, tq=META_ROWS, tk=ATTN_TK, packed=True)
    ob = _attn_b(p["sink"], qb, kb, vb, kbm, vbm, tq=WIN_TQ)
    obm = _attn_b_meta(p["sink"], qbm, kb, vb, kbm, vbm)
    merge_w = (p["wg"], p["wa"], p["wb"], p["pw"], p["ps"], p["wo"], p["ln1_g"], p["ln1_b"])
    hr = _merge(xr, oa, ob, d, *merge_w, tm=DENSE_TILE)
    hm = _merge(xm, oam, obm, dm, *merge_w, tm=META_ROWS)
    ffn_w = (p["wu"], p["wd"], p["ln2_g"], p["ln2_b"])
    return _ffn(hr, *ffn_w, tm=DENSE_TILE), _ffn(hm, *ffn_w, tm=META_ROWS)


def _trunk(x, meta_tokens, layers, tabs_r, tabs_m):
    B, S, D = x.shape
    xm = jnp.zeros((B, META_ROWS, D), x.dtype).at[:, :N_META].set(meta_tokens.astype(x.dtype)[None])
    xr = x
    for p in layers:
        xr, xm = _layer(xr, xm, p, tabs_r, tabs_m)
    return xr


def _layer_params(w_in, q_norm_g, k_norm_g, sink_logit, pool_w, pool_scale,
                  w_branch_a, w_branch_b, w_out, ln1_g, ln1_b, w_up, w_down, ln2_g, ln2_b):
    mm = lambda w: w.astype(MM_DTYPE)
    vec = lambda v: v.astype(F32).reshape(1, -1)
    layers = []
    for l in range(w_in.shape[0]):
        layers.append(dict(
            w_qkv=mm(w_in[l, :, :QKV_WIDTH]), wg=mm(w_in[l, :, QKV_WIDTH:]),
            qg=vec(q_norm_g[l]) * QK_SCALE, kg=vec(k_norm_g[l]), sink=sink_logit[l].astype(F32),
            pw=mm(pool_w[l]), ps=vec(pool_scale[l]),
            wa=mm(w_branch_a[l]), wb=mm(w_branch_b[l]), wo=mm(w_out[l]),
            ln1_g=vec(ln1_g[l]), ln1_b=vec(ln1_b[l]),
            wu=mm(w_up[l]), wd=mm(w_down[l]),
            ln2_g=vec(ln2_g[l]), ln2_b=vec(ln2_b[l]),
        ))
    return layers


def kernel(x_prompt, x_sample, meta_tokens, w_in, q_norm_g, k_norm_g, sink_logit, pool_w, pool_scale,
           w_branch_a, w_branch_b, w_out, ln1_g, ln1_b, w_up, w_down, ln2_g, ln2_b):
    layers = _layer_params(w_in, q_norm_g, k_norm_g, sink_logit, pool_w, pool_scale,
                           w_branch_a, w_branch_b, w_out, ln1_g, ln1_b, w_up, w_down, ln2_g, ln2_b)
    tabs_m = _meta_tables()
    tabs_s = _real_tables(x_sample.shape[1])
    tabs_p = tuple(t[:x_prompt.shape[1]] for t in tabs_s)
    y_prompt = _trunk(x_prompt, meta_tokens, layers, tabs_p, tabs_m)
    y_sample = _trunk(x_sample, meta_tokens, layers, tabs_s, tabs_m)
    return (y_prompt, y_sample)
```

```python
import functools

import jax
import jax.numpy as jnp
from jax import lax
from jax.experimental import pallas as pl
from jax.experimental.pallas import tpu as pltpu

D_MODEL = 1024
HEAD_DIM = 128
N_HEADS = 8
N_KV = 2
GROUP = N_HEADS // N_KV
WINDOW = 128
N_META = 16
META_ROWS = 128
GRID_W = 64
ROPE_THETA = 10000.0
POOL_WINDOWS = (2, 4, 8, 16)
POOL_DIM = D_MODEL // len(POOL_WINDOWS)
POOL_HALO = 8
D_FF = 2816
DEPTH = 2
ALPHA = (2 * DEPTH) ** 0.25
NEG = -1e30
QKV_WIDTH = 4096
LOG2E = 1.4426950408889634
QK_SCALE = HEAD_DIM ** -0.5 * LOG2E

MM_DTYPE = jnp.bfloat16
F32 = jnp.float32

V7X_VMEM_BYTES = 64 * 1024 * 1024
VMEM_CAP = V7X_VMEM_BYTES - 6 * 1024 * 1024

ROW_TILE = 1024
DENSE_TILE = 1024
ATTN_TQ = 512
ATTN_TK = 512
WIN_TQ = 1024
WIN_SUB = 256
WIN_BLK = 128
FF_CHUNK = 256
SUB_ROWS = 256
WEIGHT_ROWS = 16
LOOP_TRIPS = 2

_NT = (((1,), (1,)), ((), ()))


def _vmem_limit(block_bytes, scratch_bytes=0, temp_bytes=0):
    est = 2 * block_bytes + scratch_bytes + temp_bytes + (8 << 20)
    return int(min(max(est, 16 << 20), VMEM_CAP))


def _nbytes(shape, dtype):
    n = 1
    for s in shape:
        n *= s
    return n * jnp.dtype(dtype).itemsize


def _layer_block(shape, layer):
    index = (layer,) + (0,) * (len(shape) - 1)
    return pl.BlockSpec((1,) + tuple(shape[1:]), lambda *_: index, pipeline_mode=pl.Buffered(1))


def _layer_norm(x, g, b):
    mu = jnp.mean(x, axis=-1, keepdims=True)
    xc = x - mu
    var = jnp.mean(xc * xc, axis=-1, keepdims=True)
    return xc * lax.rsqrt(var + 1e-5) * g + b


def _pool_group(ext_ref, d_ref, pos, seq_len, tm, gi):
    n = tm + 2 * POOL_HALO
    win = POOL_WINDOWS[gi]
    c0 = gi * POOL_DIM

    def ahead(x, k):
        return pltpu.roll(x, n - k, 0) if k % n else x

    s = ext_ref[:, c0:c0 + POOL_DIM]
    span = 1
    while span < win:
        s = s + ahead(s, span)
        span *= 2
    acc = ahead(s, POOL_HALO - win // 2)[0:tm]
    lo = jnp.maximum(pos - win // 2, 0)
    hi = jnp.minimum(pos - win // 2 + win, seq_len)
    mean = acc / (hi - lo).astype(F32)
    centre = ext_ref[POOL_HALO:POOL_HALO + tm, c0:c0 + POOL_DIM]
    d_ref[0, :, c0:c0 + POOL_DIM] = (mean - centre).astype(d_ref.dtype)


def _proj_body(x_ref, x_ext, pos, seq_len, w_ref, qg_ref, kg_ref, ca_ref, sa_ref, cb_ref, sb_ref,
               qa_ref, ka_ref, va_ref, qb_ref, kb_ref, vb_ref, d_ref, ext_ref, *, tm, cka, ckb):
    xe = x_ext.astype(MM_DTYPE)
    for blk in range(2):
        c0 = 3072 + blk * 512
        ext_ref[:, blk * 512:(blk + 1) * 512] = jnp.dot(xe, w_ref[0, :, c0:c0 + 512], preferred_element_type=F32)
    pool_steps = [functools.partial(_pool_group, ext_ref, d_ref, pos, seq_len, tm, gi)
                  for gi in range(len(POOL_WINDOWS))]

    def pool_step():
        if pool_steps:
            pool_steps.pop(0)()

    sub = min(tm, SUB_ROWS)
    qg = qg_ref[0]
    kg = kg_ref[0]
    lane = lax.broadcasted_iota(jnp.int32, (sub, HEAD_DIM), 1)
    low_quarter = (lane % 64) < 32

    def rms(x, g):
        return x * lax.rsqrt(jnp.mean(x * x, axis=-1, keepdims=True) + 1e-6) * g

    def head(u, j):
        return u[:, j * HEAD_DIM:(j + 1) * HEAD_DIM]

    for r in range(0, tm, sub):
        rows = slice(r, r + sub)
        xb = x_ref[0, rows, :].astype(MM_DTYPE)
        ca = ca_ref[rows, :]
        sa = sa_ref[rows, :]
        cb = cb_ref[rows, :]
        sb = sb_ref[rows, :]

        def rope_a(x):
            partner = jnp.where(low_quarter, pltpu.roll(x, 96, 1), pltpu.roll(x, 32, 1))
            return x * ca + partner * sa

        def rope_b(x):
            return x * cb + pltpu.roll(x, 64, 1) * sb

        def cols(c0, n):
            return jnp.dot(xb, w_ref[0, :, c0:c0 + n], preferred_element_type=F32)

        def store_vt(v_ref, v, ck):
            step = min(sub, ck)
            for o in range(0, sub, step):
                j, lane0 = (r + o) // ck, (r + o) % ck
                v_ref[0, j, :, lane0:lane0 + step] = v[o:o + step, :].T.astype(v_ref.dtype)

        for blk in range(2):
            u = cols(blk * 512, 512)
            for j in range(4):
                h = blk * 4 + j
                qa_ref[0, rows, h * HEAD_DIM:(h + 1) * HEAD_DIM] = rope_a(rms(head(u, j), qg)).astype(qa_ref.dtype)
        pool_step()
        u = cols(1024, 512)
        for j in range(2):
            ka_ref[0, rows, j * HEAD_DIM:(j + 1) * HEAD_DIM] = rope_a(rms(head(u, j), kg)).astype(ka_ref.dtype)
        store_vt(va_ref, u[:, 256:512], cka)
        for blk in range(2):
            u = cols(1536 + blk * 512, 512)
            for j in range(4):
                h = blk * 4 + j
                qb_ref[0, rows, h * HEAD_DIM:(h + 1) * HEAD_DIM] = (
                    rope_b(head(u, j)) * QK_SCALE).astype(qb_ref.dtype)
        pool_step()
        u = cols(2560, 512)
        for j in range(2):
            kb_ref[0, rows, j * HEAD_DIM:(j + 1) * HEAD_DIM] = rope_b(head(u, j)).astype(kb_ref.dtype)
        store_vt(vb_ref, u[:, 256:512], ckb)
    while pool_steps:
        pool_step()


def _proj_real_kernel(x_ref, prev_ref, meta_ref, next_ref, *rest, tm, n_tiles, seq_len, cka, ckb):
    i = pl.program_id(1)
    x_ext = jnp.concatenate([jnp.where(i == 0, meta_ref[0], prev_ref[0]), x_ref[0],
                             jnp.where(i == n_tiles - 1, 0.0, next_ref[0])], axis=0)
    pos = N_META + i * tm + lax.broadcasted_iota(jnp.int32, (tm, 1), 0)
    _proj_body(x_ref, x_ext, pos, seq_len, *rest, tm=tm, cka=cka, ckb=ckb)


def _proj_meta_kernel(x_ref, first_ref, *rest, seq_len):
    x_ext = jnp.concatenate([jnp.zeros((POOL_HALO, D_MODEL), F32), x_ref[0, 0:N_META, :], first_ref[0],
                             jnp.zeros((META_ROWS - N_META, D_MODEL), F32)], axis=0)
    pos = lax.broadcasted_iota(jnp.int32, (META_ROWS, 1), 0)
    _proj_body(x_ref, x_ext, pos, seq_len, *rest, tm=META_ROWS, cka=META_ROWS, ckb=META_ROWS)


def _proj_call(kernel, grid, x_specs, x_args, p, layer, tabs, B, T, tm, cka, ckb, row, tab_index):
    D = D_MODEL
    tab = pl.BlockSpec((tm, HEAD_DIM), tab_index)
    w_qkv, qg, kg = p["w_qkv"], p["qg"], p["kg"]
    chunked = lambda f: (lambda *idx: row(*idx)[:2] + (0, 0))
    out_shape = (
        jax.ShapeDtypeStruct((B, T, 1024), MM_DTYPE),
        jax.ShapeDtypeStruct((B, T, 256), MM_DTYPE),
        jax.ShapeDtypeStruct((B, T // cka, 256, cka), MM_DTYPE),
        jax.ShapeDtypeStruct((B, T, 1024), MM_DTYPE),
        jax.ShapeDtypeStruct((B, T, 256), MM_DTYPE),
        jax.ShapeDtypeStruct((B, T // ckb, 256, ckb), MM_DTYPE),
        jax.ShapeDtypeStruct((B, T, 1024), MM_DTYPE),
    )
    out_specs = (
        pl.BlockSpec((1, tm, 1024), row),
        pl.BlockSpec((1, tm, 256), row),
        pl.BlockSpec((1, tm // cka, 256, cka), chunked(row)),
        pl.BlockSpec((1, tm, 1024), row),
        pl.BlockSpec((1, tm, 256), row),
        pl.BlockSpec((1, tm // ckb, 256, ckb), chunked(row)),
        pl.BlockSpec((1, tm, 1024), row),
    )
    blocks = (_nbytes((tm, D), F32) + 4 * _nbytes((tm, HEAD_DIM), F32) + _nbytes((tm, 4096), MM_DTYPE))
    ext_bytes = _nbytes((tm + 2 * POOL_HALO, D), F32)
    return pl.pallas_call(
        kernel,
        grid=grid,
        in_specs=x_specs + [_layer_block(w_qkv.shape, layer), _layer_block(qg.shape, layer),
                            _layer_block(kg.shape, layer), tab, tab, tab, tab],
        out_specs=out_specs,
        out_shape=out_shape,
        scratch_shapes=[pltpu.VMEM((tm + 2 * POOL_HALO, D), F32)],
        compiler_params=pltpu.CompilerParams(
            dimension_semantics=("parallel",) * len(grid),
            vmem_limit_bytes=_vmem_limit(blocks, _nbytes(w_qkv.shape[1:], MM_DTYPE) + ext_bytes,
                                         2 * ext_bytes + 8 * _nbytes((tm, 512), F32))),
        name="proj",
    )(*x_args, w_qkv, qg, kg, *tabs)


def _proj(x, xm, p, layer, tabs, *, tm, cka, ckb):
    B, S, D = x.shape
    n_tiles = S // tm
    per = tm // POOL_HALO
    last = S // POOL_HALO - 1
    row = lambda b, i: (b, i, 0)
    halo = lambda f: pl.BlockSpec((1, POOL_HALO, D), f)
    x_specs = [
        pl.BlockSpec((1, tm, D), row),
        halo(lambda b, i: (b, jnp.maximum(i * per - 1, 0), 0)),
        halo(lambda b, i: (b, N_META // POOL_HALO - 1, 0)),
        halo(lambda b, i: (b, jnp.minimum((i + 1) * per, last), 0)),
    ]
    kernel = functools.partial(_proj_real_kernel, tm=tm, n_tiles=n_tiles, seq_len=N_META + S, cka=cka, ckb=ckb)
    return _proj_call(kernel, (B, n_tiles), x_specs, (x, x, xm, x), p, layer, tabs, B, S, tm, cka, ckb,
                      row, lambda b, i: (i, 0))


def _proj_meta(xm, x, p, layer, tabs):
    B, _, D = xm.shape
    row = lambda b: (b, 0, 0)
    x_specs = [pl.BlockSpec((1, META_ROWS, D), row), pl.BlockSpec((1, POOL_HALO, D), row)]
    kernel = functools.partial(_proj_meta_kernel, seq_len=N_META + x.shape[1])
    return _proj_call(kernel, (B,), x_specs, (xm, x), p, layer, tabs, B, META_ROWS, META_ROWS,
                      META_ROWS, META_ROWS, row, lambda b: (0, 0))


def _attn_a_kernel(q_ref, k_ref, vt_ref, km_ref, vmt_ref, o_ref,
                   acc_ref, m_ref, l_ref, s_ref, p_ref, ms_ref, as_ref, ac_ref, qp_ref, *, tq, tk, n_chunks, packed):
    heads = 1 if packed else GROUP
    if packed:
        qp_ref[...] = jnp.zeros(qp_ref.shape, qp_ref.dtype)
        for h in range(GROUP):
            qp_ref[h * N_META:(h + 1) * N_META, :] = q_ref[0, 0:N_META, h * HEAD_DIM:(h + 1) * HEAD_DIM]

    def q_head(h):
        return qp_ref[...] if packed else q_ref[0, :, h * HEAD_DIM:(h + 1) * HEAD_DIM]

    def scores(c, slot, h):
        kc = k_ref[0, pl.ds(pl.multiple_of(c * tk, tk), tk), :]
        s = lax.dot_general(kc, q_head(h), _NT, preferred_element_type=F32)
        s_ref[slot, h] = s
        m_old = m_ref[h]
        m_new = jnp.maximum(m_old, jnp.max(s, axis=0, keepdims=True))
        m_ref[h] = m_new
        ms_ref[slot, h] = m_new
        as_ref[slot, h] = jnp.exp2(m_old - m_new)

    def weights(slot, h):
        a = as_ref[slot, h]
        m = ms_ref[slot, h]
        ac_ref[slot, h] = a
        part = jnp.zeros((8, tq), F32)
        for r in range(0, tk, WEIGHT_ROWS):
            p = jnp.exp2(s_ref[slot, h, r:r + WEIGHT_ROWS, :] - m)
            p_ref[slot, h, r:r + WEIGHT_ROWS, :] = p.astype(p_ref.dtype)
            for i in range(WEIGHT_ROWS // 8):
                part = part + p[i * 8:(i + 1) * 8, :]
        l_ref[h] = a * l_ref[h] + jnp.sum(part, axis=0, keepdims=True)

    def values(c, slot, h):
        pv = jnp.dot(vt_ref[0, c], p_ref[slot, h], preferred_element_type=F32)
        acc_ref[h] = ac_ref[slot, h] * acc_ref[h] + pv

    key_row = lax.broadcasted_iota(jnp.int32, (META_ROWS, tq), 0)
    s_meta = [lax.dot_general(km_ref[0], q_head(h), _NT, preferred_element_type=F32) for h in range(heads)]
    p_meta = []
    for h in range(heads):
        s = jnp.where(key_row < N_META, s_meta[h], NEG)
        m = jnp.max(s, axis=0, keepdims=True)
        p = jnp.exp2(s - m)
        m_ref[h] = m
        l_ref[h] = jnp.sum(p, axis=0, keepdims=True)
        p_meta.append(p.astype(MM_DTYPE))
    for h in range(heads):
        acc_ref[h] = jnp.dot(vmt_ref[0, 0], p_meta[h], preferred_element_type=F32)
    for h in range(heads):
        scores(0, 0, h)
    for h in range(heads):
        scores(1, 1, h)
        weights(0, h)

    def trip(c, parity, with_scores=True):
        for h in range(heads):
            if with_scores:
                scores(c + 1, 1 - parity, h)
            values(c - 1, 1 - parity, h)
            weights(parity, h)

    def body(j, carry):
        for u in range(LOOP_TRIPS):
            trip(LOOP_TRIPS * j + 1 + u, (1 + u) % 2)
        return carry

    looped = (n_chunks - 2) // LOOP_TRIPS * LOOP_TRIPS
    lax.fori_loop(0, looped // LOOP_TRIPS, body, 0)
    for c in range(looped + 1, n_chunks):
        trip(c, c % 2, with_scores=c + 1 < n_chunks)
    for h in range(heads):
        values(n_chunks - 1, (n_chunks - 1) % 2, h)
    if packed:
        o = (acc_ref[0] / l_ref[0]).T.astype(o_ref.dtype)
        o_ref[0] = jnp.zeros(o_ref.shape[1:], o_ref.dtype)
        for h in range(GROUP):
            o_ref[0, 0:N_META, h * HEAD_DIM:(h + 1) * HEAD_DIM] = o[h * N_META:(h + 1) * N_META, :]
        return
    for h in range(GROUP):
        o = acc_ref[h] / l_ref[h]
        o_ref[0, :, h * HEAD_DIM:(h + 1) * HEAD_DIM] = o.T.astype(o_ref.dtype)


def _attn_a(q, k, vt, km, vmt, *, tq, tk, packed=False):
    B, Tq, _ = q.shape
    S = k.shape[1]
    n_chunks = S // tk
    assert n_chunks % 2 == 0
    blocks = (_nbytes((tq, 512), MM_DTYPE) * 2 + 2 * _nbytes((S, HEAD_DIM), MM_DTYPE)
              + 2 * _nbytes((META_ROWS, HEAD_DIM), MM_DTYPE))
    heads = 1 if packed else GROUP
    stat = pltpu.VMEM((2, heads, 1, tq), F32)
    scratch = (heads * (_nbytes((HEAD_DIM, tq), F32) + 8 * _nbytes((8, tq), F32))
               + 2 * heads * (_nbytes((tk, tq), F32) + _nbytes((tk, tq), MM_DTYPE)))
    return pl.pallas_call(
        functools.partial(_attn_a_kernel, tq=tq, tk=tk, n_chunks=n_chunks, packed=packed),
        grid=(B, N_KV, Tq // tq),
        in_specs=[
            pl.BlockSpec((1, tq, GROUP * HEAD_DIM), lambda b, g, i: (b, i, g)),
            pl.BlockSpec((1, S, HEAD_DIM), lambda b, g, i: (b, 0, g)),
            pl.BlockSpec((1, n_chunks, HEAD_DIM, tk), lambda b, g, i: (b, 0, g, 0)),
            pl.BlockSpec((1, META_ROWS, HEAD_DIM), lambda b, g, i: (b, 0, g)),
            pl.BlockSpec((1, 1, HEAD_DIM, META_ROWS), lambda b, g, i: (b, 0, g, 0)),
        ],
        out_specs=pl.BlockSpec((1, tq, GROUP * HEAD_DIM), lambda b, g, i: (b, i, g)),
        out_shape=jax.ShapeDtypeStruct((B, Tq, N_HEADS * HEAD_DIM), MM_DTYPE),
        scratch_shapes=[
            pltpu.VMEM((heads, HEAD_DIM, tq), F32),
            pltpu.VMEM((heads, 1, tq), F32),
            pltpu.VMEM((heads, 1, tq), F32),
            pltpu.VMEM((2, heads, tk, tq), F32),
            pltpu.VMEM((2, heads, tk, tq), MM_DTYPE),
            stat, stat, stat,
            pltpu.VMEM((META_ROWS, HEAD_DIM), MM_DTYPE),
        ],
        compiler_params=pltpu.CompilerParams(
            dimension_semantics=("parallel", "parallel", "parallel"),
            vmem_limit_bytes=_vmem_limit(blocks, scratch, 10 * _nbytes((tk, tq), F32))),
        name="attn_global",
    )(q, k, vt, km, vmt)


def _win_attend(sink_ref, layer, q_ref, o_ref, problems):
    g = pl.program_id(1)
    scores = [[lax.dot_general(k_all, q_ref[0, col0:col0 + width, h * HEAD_DIM:(h + 1) * HEAD_DIM], _NT,
                               preferred_element_type=F32) for h in range(GROUP)]
              for col0, width, k_all, _, _ in problems]
    for (col0, width, _, vt_all, mask), s_heads in zip(problems, scores):
        for h in range(GROUP):
            s = jnp.where(mask, s_heads[h], NEG)
            sink = sink_ref[layer, g * GROUP + h] * LOG2E
            m = jnp.maximum(jnp.max(s, axis=0, keepdims=True), sink)
            p = jnp.exp2(s - m)
            den = jnp.sum(p, axis=0, keepdims=True) + jnp.exp2(sink - m)
            o = jnp.dot(vt_all, p.astype(MM_DTYPE), preferred_element_type=F32) / den
            o_ref[0, col0:col0 + width, h * HEAD_DIM:(h + 1) * HEAD_DIM] = o.T.astype(o_ref.dtype)


def _attn_b_kernel(sink_ref, q_ref, kp_ref, kc_ref, kn_ref, vp_ref, vc_ref, vn_ref, km_ref, vmt_ref, o_ref,
                   *, tq, n_tiles, layer):
    i = pl.program_id(2)
    nb = tq // WIN_BLK
    n_sub = tq // WIN_SUB
    k_real = jnp.concatenate([kp_ref[0], kc_ref[0], kn_ref[0]], axis=0)
    vt_real = jnp.concatenate([vp_ref[0, 0]] + [vc_ref[0, j] for j in range(nb)] + [vn_ref[0, 0]], axis=1)
    n_keys = META_ROWS + 2 * WIN_BLK + WIN_SUB
    r = lax.broadcasted_iota(jnp.int32, (n_keys, WIN_SUB), 0)
    c = lax.broadcasted_iota(jnp.int32, (n_keys, WIN_SUB), 1)
    rel = r - c - (META_ROWS + WIN_BLK)
    in_win = (rel >= -WINDOW) & (rel <= WINDOW) & (r >= META_ROWS)
    problems = []
    for j in range(n_sub):
        span = slice(j * WIN_SUB, j * WIN_SUB + WIN_SUB + 2 * WIN_BLK)
        k_all = jnp.concatenate([km_ref[0], k_real[span, :]], axis=0)
        vt_all = jnp.concatenate([vmt_ref[0, 0], vt_real[:, span]], axis=1)
        valid = in_win
        if j == 0:
            valid = valid & (r >= META_ROWS + jnp.where(i == 0, WIN_BLK, 0))
        if j == n_sub - 1:
            valid = valid & (r < n_keys - jnp.where(i == n_tiles - 1, WIN_BLK, 0))
        problems.append((j * WIN_SUB, WIN_SUB, k_all, vt_all, (r < N_META) | valid))
    _win_attend(sink_ref, layer, q_ref, o_ref, problems)


def _attn_b_meta_kernel(sink_ref, q_ref, k0_ref, v0_ref, km_ref, vmt_ref, o_ref, *, layer):
    k_all = jnp.concatenate([km_ref[0], k0_ref[0]], axis=0)
    vt_all = jnp.concatenate([vmt_ref[0, 0], v0_ref[0, 0]], axis=1)
    r = lax.broadcasted_iota(jnp.int32, (META_ROWS + WIN_BLK, META_ROWS), 0)
    c = lax.broadcasted_iota(jnp.int32, (META_ROWS + WIN_BLK, META_ROWS), 1)
    mask = (r < N_META) | ((r >= META_ROWS) & (r - META_ROWS + N_META - c <= WINDOW))
    _win_attend(sink_ref, layer, q_ref, o_ref, [(0, META_ROWS, k_all, vt_all, mask)])


def _attn_b(sink, layer, q, k, vt, km, vmt, *, tq):
    B, S, _ = q.shape
    n_tiles = S // tq
    nb = tq // WIN_BLK
    last_blk = S // WIN_BLK - 1
    kblk = lambda shape, f: pl.BlockSpec(shape, f)
    n_keys = META_ROWS + 2 * WIN_BLK + tq
    blocks = 2 * _nbytes((tq, 512), MM_DTYPE) + 2 * _nbytes((n_keys, HEAD_DIM), MM_DTYPE)
    grid_spec = pltpu.PrefetchScalarGridSpec(
        num_scalar_prefetch=1,
        grid=(B, N_KV, n_tiles),
        in_specs=[
            pl.BlockSpec((1, tq, GROUP * HEAD_DIM), lambda b, g, i, s: (b, i, g)),
            kblk((1, WIN_BLK, HEAD_DIM), lambda b, g, i, s: (b, jnp.maximum(i * nb - 1, 0), g)),
            kblk((1, tq, HEAD_DIM), lambda b, g, i, s: (b, i, g)),
            kblk((1, WIN_BLK, HEAD_DIM), lambda b, g, i, s: (b, jnp.minimum((i + 1) * nb, last_blk), g)),
            kblk((1, 1, HEAD_DIM, WIN_BLK), lambda b, g, i, s: (b, jnp.maximum(i * nb - 1, 0), g, 0)),
            kblk((1, nb, HEAD_DIM, WIN_BLK), lambda b, g, i, s: (b, i, g, 0)),
            kblk((1, 1, HEAD_DIM, WIN_BLK), lambda b, g, i, s: (b, jnp.minimum((i + 1) * nb, last_blk), g, 0)),
            kblk((1, META_ROWS, HEAD_DIM), lambda b, g, i, s: (b, 0, g)),
            kblk((1, 1, HEAD_DIM, META_ROWS), lambda b, g, i, s: (b, 0, g, 0)),
        ],
        out_specs=pl.BlockSpec((1, tq, GROUP * HEAD_DIM), lambda b, g, i, s: (b, i, g)),
    )
    return pl.pallas_call(
        functools.partial(_attn_b_kernel, tq=tq, n_tiles=n_tiles, layer=layer),
        grid_spec=grid_spec,
        out_shape=jax.ShapeDtypeStruct((B, S, N_HEADS * HEAD_DIM), MM_DTYPE),
        compiler_params=pltpu.CompilerParams(
            dimension_semantics=("parallel", "parallel", "parallel"),
            vmem_limit_bytes=_vmem_limit(blocks, 0, 8 * _nbytes((n_keys, tq), F32))),
        name="attn_window",
    )(sink, q, k, k, k, vt, vt, vt, km, vmt)


def _attn_b_meta(sink, layer, qm, k, vt, km, vmt):
    B = qm.shape[0]
    blocks = 2 * _nbytes((META_ROWS, 512), MM_DTYPE) + 4 * _nbytes((META_ROWS, HEAD_DIM), MM_DTYPE)
    grid_spec = pltpu.PrefetchScalarGridSpec(
        num_scalar_prefetch=1,
        grid=(B, N_KV),
        in_specs=[
            pl.BlockSpec((1, META_ROWS, GROUP * HEAD_DIM), lambda b, g, s: (b, 0, g)),
            pl.BlockSpec((1, WIN_BLK, HEAD_DIM), lambda b, g, s: (b, 0, g)),
            pl.BlockSpec((1, 1, HEAD_DIM, WIN_BLK), lambda b, g, s: (b, 0, g, 0)),
            pl.BlockSpec((1, META_ROWS, HEAD_DIM), lambda b, g, s: (b, 0, g)),
            pl.BlockSpec((1, 1, HEAD_DIM, META_ROWS), lambda b, g, s: (b, 0, g, 0)),
        ],
        out_specs=pl.BlockSpec((1, META_ROWS, GROUP * HEAD_DIM), lambda b, g, s: (b, 0, g)),
    )
    return pl.pallas_call(
        functools.partial(_attn_b_meta_kernel, layer=layer),
        grid_spec=grid_spec,
        out_shape=jax.ShapeDtypeStruct((B, META_ROWS, N_HEADS * HEAD_DIM), MM_DTYPE),
        compiler_params=pltpu.CompilerParams(
            dimension_semantics=("parallel", "parallel"),
            vmem_limit_bytes=_vmem_limit(blocks, 0, 8 * _nbytes((2 * META_ROWS, META_ROWS), F32))),
        name="attn_window_meta",
    )(sink, qm, k, vt, km, vmt)


def _merge_kernel(h_ref, oa_ref, ob_ref, d_ref, wg_ref, wa_ref, wb_ref, pw_ref, ps_ref, wo_ref, g_ref, b_ref,
                  out_ref, *, tm):
    def gate(hb, j):
        return jax.nn.sigmoid(jnp.dot(hb, wg_ref[0, :, j * D_MODEL:(j + 1) * D_MODEL], preferred_element_type=F32))

    sub = min(tm, SUB_ROWS)
    for r in range(0, tm, sub):
        rows = slice(r, r + sub)
        h = h_ref[0, rows, :]
        hb = h.astype(MM_DTYPE)
        ya = jnp.dot(oa_ref[0, rows, :], wa_ref[0], preferred_element_type=F32)
        yb = jnp.dot(ob_ref[0, rows, :], wb_ref[0], preferred_element_type=F32)
        yc = jnp.concatenate(
            [jnp.dot(d_ref[0, rows, gi * POOL_DIM:(gi + 1) * POOL_DIM], pw_ref[0, gi], preferred_element_type=F32)
             for gi in range(len(POOL_WINDOWS))], axis=1) * ps_ref[0]
        merged = gate(hb, 0) * ya + gate(hb, 1) * yb + gate(hb, 2) * yc
        y = jnp.dot(merged.astype(MM_DTYPE), wo_ref[0], preferred_element_type=F32)
        out_ref[0, rows, :] = _layer_norm(ALPHA * h + y, g_ref[0], b_ref[0])


def _merge(h, oa, ob, d, p, layer, *, tm):
    B, T, D = h.shape
    row = lambda bb, i: (bb, i, 0)
    params = [p[name] for name in ("wg", "wa", "wb", "pw", "ps", "wo", "ln1_g", "ln1_b")]
    wbytes = sum(_nbytes(w.shape[1:], w.dtype) for w in params)
    blocks = 2 * _nbytes((tm, D), F32) + 3 * _nbytes((tm, D), MM_DTYPE)
    return pl.pallas_call(
        functools.partial(_merge_kernel, tm=tm),
        grid=(B, T // tm),
        in_specs=[pl.BlockSpec((1, tm, D), row)] * 4 + [_layer_block(w.shape, layer) for w in params],
        out_specs=pl.BlockSpec((1, tm, D), row),
        out_shape=jax.ShapeDtypeStruct((B, T, D), F32),
        compiler_params=pltpu.CompilerParams(
            dimension_semantics=("parallel", "parallel"),
            vmem_limit_bytes=_vmem_limit(blocks, wbytes, 12 * _nbytes((min(tm, SUB_ROWS), D), F32))),
        name="merge",
    )(h, oa, ob, d, *params)


def _ffn_kernel(h_ref, wu_ref, wd_ref, g_ref, b_ref, out_ref, act_ref, *, tm):
    sub = min(tm, SUB_ROWS)
    for r in range(0, tm, sub):
        h = h_ref[0, r:r + sub, :]
        hb = h.astype(MM_DTYPE)
        for c in range(D_FF // FF_CHUNK):
            c0 = c * FF_CHUNK
            gate = jnp.dot(hb, wu_ref[0, :, c0:c0 + FF_CHUNK], preferred_element_type=F32)
            up = jnp.dot(hb, wu_ref[0, :, D_FF + c0:D_FF + c0 + FF_CHUNK], preferred_element_type=F32)
            act_ref[r:r + sub, c0:c0 + FF_CHUNK] = (jax.nn.silu(gate) * up).astype(act_ref.dtype)
        f = jnp.dot(act_ref[r:r + sub, :], wd_ref[0], preferred_element_type=F32)
        out_ref[0, r:r + sub, :] = _layer_norm(ALPHA * h + f, g_ref[0], b_ref[0])


def _ffn(h, p, layer, *, tm):
    B, T, D = h.shape
    row = lambda bb, i: (bb, i, 0)
    params = [p[name] for name in ("wu", "wd", "ln2_g", "ln2_b")]
    wbytes = sum(_nbytes(w.shape[1:], w.dtype) for w in params)
    return pl.pallas_call(
        functools.partial(_ffn_kernel, tm=tm),
        grid=(B, T // tm),
        in_specs=[pl.BlockSpec((1, tm, D), row)] + [_layer_block(w.shape, layer) for w in params],
        out_specs=pl.BlockSpec((1, tm, D), row),
        out_shape=jax.ShapeDtypeStruct((B, T, D), F32),
        scratch_shapes=[pltpu.VMEM((tm, D_FF), MM_DTYPE)],
        compiler_params=pltpu.CompilerParams(
            dimension_semantics=("parallel", "parallel"),
            vmem_limit_bytes=_vmem_limit(2 * _nbytes((tm, D), F32), wbytes + _nbytes((tm, D_FF), MM_DTYPE),
                                         6 * _nbytes((tm, D), F32))),
        name="ffn",
    )(h, *params)


def _cos_sin(p, dim):
    inv = ROPE_THETA ** (-jnp.arange(0, dim, 2, dtype=F32) / dim)
    ang = p.astype(F32)[:, None] * inv[None, :]
    return jnp.cos(ang), jnp.sin(ang)


def _rope_tables(row_cs, col_cs, pos):
    cr, sr = row_cs
    cc, sc = col_cs
    cp, sp = _cos_sin(pos, HEAD_DIM)
    ca = jnp.concatenate([cr, cr, cc, cc], axis=1)
    sa = jnp.concatenate([-sr, sr, -sc, sc], axis=1)
    cb = jnp.concatenate([cp, cp], axis=1)
    sb = jnp.concatenate([-sp, sp], axis=1)
    return ca, sa, cb, sb


def _real_tables(S):
    half = HEAD_DIM // 2
    row_cs = [jnp.repeat(x, GRID_W, axis=0) for x in _cos_sin(jnp.arange(S // GRID_W, dtype=jnp.int32), half)]
    col_cs = [jnp.tile(x, (S // GRID_W, 1)) for x in _cos_sin(jnp.arange(GRID_W, dtype=jnp.int32), half)]
    return _rope_tables(row_cs, col_cs, jnp.arange(S, dtype=jnp.int32) + N_META)


def _meta_tables():
    i = jnp.arange(META_ROWS, dtype=jnp.int32)
    half = HEAD_DIM // 2
    return _rope_tables(_cos_sin(-jnp.ones((META_ROWS,), jnp.int32), half), _cos_sin(i, half), i)


def _layer(xr, xm, p, layer, tabs_r, tabs_m):
    qa, ka, va, qb, kb, vb, d = _proj(xr, xm, p, layer, tabs_r, tm=ROW_TILE, cka=ATTN_TK, ckb=WIN_BLK)
    qam, kam, vam, qbm, kbm, vbm, dm = _proj_meta(xm, xr, p, layer, tabs_m)
    oa = _attn_a(qa, ka, va, kam, vam, tq=ATTN_TQ, tk=ATTN_TK)
    oam = _attn_a(qam, ka, va, kam, vam, tq=META_ROWS, tk=ATTN_TK, packed=True)
    ob = _attn_b(p["sink"], layer, qb, kb, vb, kbm, vbm, tq=WIN_TQ)
    obm = _attn_b_meta(p["sink"], layer, qbm, kb, vb, kbm, vbm)
    hr = _merge(xr, oa, ob, d, p, layer, tm=DENSE_TILE)
    hm = _merge(xm, oam, obm, dm, p, layer, tm=META_ROWS)
    return _ffn(hr, p, layer, tm=DENSE_TILE), _ffn(hm, p, layer, tm=META_ROWS)


def _trunk(x, meta_tokens, p, tabs_r, tabs_m):
    B, S, D = x.shape
    xm = jnp.zeros((B, META_ROWS, D), x.dtype).at[:, :N_META].set(meta_tokens.astype(x.dtype)[None])
    xr = x
    for layer in range(p["w_qkv"].shape[0]):
        xr, xm = _layer(xr, xm, p, layer, tabs_r, tabs_m)
    return xr


def _stack_params(w_in, q_norm_g, k_norm_g, sink_logit, pool_w, pool_scale,
                  w_branch_a, w_branch_b, w_out, ln1_g, ln1_b, w_up, w_down, ln2_g, ln2_b):
    mm = lambda w: w.astype(MM_DTYPE)
    vec = lambda v: v.astype(F32)[:, None, :]
    return dict(
        w_qkv=mm(w_in[:, :, :QKV_WIDTH]), wg=mm(w_in[:, :, QKV_WIDTH:]),
        qg=vec(q_norm_g) * QK_SCALE, kg=vec(k_norm_g), sink=sink_logit.astype(F32),
        pw=mm(pool_w), ps=vec(pool_scale),
        wa=mm(w_branch_a), wb=mm(w_branch_b), wo=mm(w_out),
        ln1_g=vec(ln1_g), ln1_b=vec(ln1_b),
        wu=mm(w_up), wd=mm(w_down),
        ln2_g=vec(ln2_g), ln2_b=vec(ln2_b),
    )


def kernel(x_prompt, x_sample, meta_tokens, w_in, q_norm_g, k_norm_g, sink_logit, pool_w, pool_scale,
           w_branch_a, w_branch_b, w_out, ln1_g, ln1_b, w_up, w_down, ln2_g, ln2_b):
    params = _stack_params(w_in, q_norm_g, k_norm_g, sink_logit, pool_w, pool_scale,
                           w_branch_a, w_branch_b, w_out, ln1_g, ln1_b, w_up, w_down, ln2_g, ln2_b)
    tabs_m = _meta_tables()
    tabs_s = _real_tables(x_sample.shape[1])
    tabs_p = tuple(t[:x_prompt.shape[1]] for t in tabs_s)
    y_prompt = _trunk(x_prompt, meta_tokens, params, tabs_p, tabs_m)
    y_sample = _trunk(x_sample, meta_tokens, params, tabs_s, tabs_m)
    return (y_prompt, y_sample)
```

```python
import functools

import jax
import jax.numpy as jnp
from jax import lax
from jax.experimental import pallas as pl
from jax.experimental.pallas import tpu as pltpu

D_MODEL = 1024
HEAD_DIM = 128
N_HEADS = 8
N_KV = 2
GROUP = N_HEADS // N_KV
WINDOW = 128
N_META = 16
META_ROWS = 128
GRID_W = 64
ROPE_THETA = 10000.0
POOL_WINDOWS = (2, 4, 8, 16)
POOL_DIM = D_MODEL // len(POOL_WINDOWS)
POOL_HALO = 8
D_FF = 2816
DEPTH = 2
ALPHA = (2 * DEPTH) ** 0.25
NEG = -1e30
Q_WIDTH = N_HEADS * HEAD_DIM
KV_WIDTH = N_KV * HEAD_DIM
GROUP_WIDTH = GROUP * HEAD_DIM
COL_QA = 0
COL_KVA = COL_QA + Q_WIDTH
COL_QB = COL_KVA + 2 * KV_WIDTH
COL_KVB = COL_QB + Q_WIDTH
COL_UC = COL_KVB + 2 * KV_WIDTH
QKV_WIDTH = COL_UC + D_MODEL
LOG2E = 1.4426950408889634
QK_SCALE = HEAD_DIM ** -0.5 * LOG2E

MM_DTYPE = jnp.bfloat16
F32 = jnp.float32

V7X_VMEM_BYTES = 64 * 1024 * 1024
VMEM_CAP = V7X_VMEM_BYTES - 6 * 1024 * 1024

ROW_TILE = 1024
DENSE_TILE = 1024
ATTN_TQ = 512
ATTN_TK = 512
WIN_TQ = 1024
WIN_SUB = 256
WIN_BLK = 128
FF_CHUNK = 256
SUB_ROWS = 256
PROJ_COLS = 512
WEIGHT_ROWS = 16
LOOP_TRIPS = 2

_NT = (((1,), (1,)), ((), ()))


def _vmem_limit(block_bytes, scratch_bytes=0, temp_bytes=0):
    est = 2 * block_bytes + scratch_bytes + temp_bytes + (8 << 20)
    return int(min(max(est, 16 << 20), VMEM_CAP))


def _nbytes(shape, dtype):
    n = 1
    for s in shape:
        n *= s
    return n * jnp.dtype(dtype).itemsize


def _layer_block(shape, layer):
    index = (layer,) + (0,) * (len(shape) - 1)
    return pl.BlockSpec((1,) + tuple(shape[1:]), lambda *_: index, pipeline_mode=pl.Buffered(1))


def _layer_norm(x, g, b):
    mu = jnp.mean(x, axis=-1, keepdims=True)
    xc = x - mu
    var = jnp.mean(xc * xc, axis=-1, keepdims=True)
    return xc * lax.rsqrt(var + 1e-5) * g + b


def _pool_group(ext_ref, d_ref, pos, seq_len, tm, gi):
    n = tm + 2 * POOL_HALO
    win = POOL_WINDOWS[gi]
    c0 = gi * POOL_DIM

    def ahead(x, k):
        return pltpu.roll(x, n - k, 0) if k % n else x

    s = ext_ref[:, c0:c0 + POOL_DIM]
    span = 1
    while span < win:
        s = s + ahead(s, span)
        span *= 2
    acc = ahead(s, POOL_HALO - win // 2)[0:tm]
    lo = jnp.maximum(pos - win // 2, 0)
    hi = jnp.minimum(pos - win // 2 + win, seq_len)
    mean = acc / (hi - lo).astype(F32)
    centre = ext_ref[POOL_HALO:POOL_HALO + tm, c0:c0 + POOL_DIM]
    d_ref[0, :, c0:c0 + POOL_DIM] = (mean - centre).astype(d_ref.dtype)


def _proj_body(x_ref, x_ext, pos, seq_len, w_ref, qg_ref, kg_ref, ca_ref, sa_ref, cb_ref, sb_ref,
               qa_ref, ka_ref, va_ref, qb_ref, kb_ref, vb_ref, d_ref, ext_ref, *, tm, cka, ckb):
    xe = x_ext.astype(MM_DTYPE)
    for c0 in range(0, D_MODEL, PROJ_COLS):
        ext_ref[:, c0:c0 + PROJ_COLS] = jnp.dot(xe, w_ref[0, :, COL_UC + c0:COL_UC + c0 + PROJ_COLS],
                                                preferred_element_type=F32)
    for gi in range(len(POOL_WINDOWS)):
        _pool_group(ext_ref, d_ref, pos, seq_len, tm, gi)

    sub = min(tm, SUB_ROWS)
    qg = qg_ref[0]
    kg = kg_ref[0]
    lane = lax.broadcasted_iota(jnp.int32, (sub, HEAD_DIM), 1)
    low_quarter = (lane % (HEAD_DIM // 2)) < HEAD_DIM // 4

    def rms(x, g):
        return x * lax.rsqrt(jnp.mean(x * x, axis=-1, keepdims=True) + 1e-6) * g

    def head(u, j):
        return u[:, j * HEAD_DIM:(j + 1) * HEAD_DIM]

    for r in range(0, tm, sub):
        rows = slice(r, r + sub)
        xb = x_ref[0, rows, :].astype(MM_DTYPE)
        ca = ca_ref[rows, :]
        sa = sa_ref[rows, :]
        cb = cb_ref[rows, :]
        sb = sb_ref[rows, :]

        def rope_a(x):
            partner = jnp.where(low_quarter, pltpu.roll(x, HEAD_DIM - HEAD_DIM // 4, 1),
                                pltpu.roll(x, HEAD_DIM // 4, 1))
            return x * ca + partner * sa

        def rope_b(x):
            return x * cb + pltpu.roll(x, HEAD_DIM // 2, 1) * sb

        def cols(c0, n):
            return jnp.dot(xb, w_ref[0, :, c0:c0 + n], preferred_element_type=F32)

        def q_heads(q_ref, col0, finish):
            for c0 in range(0, Q_WIDTH, PROJ_COLS):
                u = cols(col0 + c0, PROJ_COLS)
                for j in range(PROJ_COLS // HEAD_DIM):
                    lanes = slice(c0 + j * HEAD_DIM, c0 + (j + 1) * HEAD_DIM)
                    q_ref[0, rows, lanes] = finish(head(u, j)).astype(q_ref.dtype)

        def kv_heads(k_ref, v_ref, col0, finish, ck):
            u = cols(col0, 2 * KV_WIDTH)
            for j in range(N_KV):
                k_ref[0, rows, j * HEAD_DIM:(j + 1) * HEAD_DIM] = finish(head(u, j)).astype(k_ref.dtype)
            v = u[:, KV_WIDTH:2 * KV_WIDTH]
            step = min(sub, ck)
            for o in range(0, sub, step):
                j, lane0 = (r + o) // ck, (r + o) % ck
                v_ref[0, j, :, lane0:lane0 + step] = v[o:o + step, :].T.astype(v_ref.dtype)

        q_heads(qa_ref, COL_QA, lambda x: rope_a(rms(x, qg)))
        kv_heads(ka_ref, va_ref, COL_KVA, lambda x: rope_a(rms(x, kg)), cka)
        q_heads(qb_ref, COL_QB, lambda x: rope_b(x) * QK_SCALE)
        kv_heads(kb_ref, vb_ref, COL_KVB, rope_b, ckb)


def _proj_real_kernel(x_ref, prev_ref, meta_ref, next_ref, *rest, tm, n_tiles, seq_len, cka, ckb):
    i = pl.program_id(1)
    x_ext = jnp.concatenate([jnp.where(i == 0, meta_ref[0], prev_ref[0]), x_ref[0],
                             jnp.where(i == n_tiles - 1, 0.0, next_ref[0])], axis=0)
    pos = N_META + i * tm + lax.broadcasted_iota(jnp.int32, (tm, 1), 0)
    _proj_body(x_ref, x_ext, pos, seq_len, *rest, tm=tm, cka=cka, ckb=ckb)


def _proj_meta_kernel(x_ref, first_ref, *rest, seq_len):
    x_ext = jnp.concatenate([jnp.zeros((POOL_HALO, D_MODEL), F32), x_ref[0, 0:N_META, :], first_ref[0],
                             jnp.zeros((META_ROWS - N_META, D_MODEL), F32)], axis=0)
    pos = lax.broadcasted_iota(jnp.int32, (META_ROWS, 1), 0)
    _proj_body(x_ref, x_ext, pos, seq_len, *rest, tm=META_ROWS, cka=META_ROWS, ckb=META_ROWS)


def _proj_call(kernel, grid, x_specs, x_args, p, layer, tabs, B, T, tm, cka, ckb, row, tab_index):
    D = D_MODEL
    tab = pl.BlockSpec((tm, HEAD_DIM), tab_index)
    w_qkv, qg, kg = p["w_qkv"], p["qg"], p["kg"]
    chunk_index = lambda *idx: row(*idx)[:2] + (0, 0)
    out_shape = (
        jax.ShapeDtypeStruct((B, T, Q_WIDTH), MM_DTYPE),
        jax.ShapeDtypeStruct((B, T, KV_WIDTH), MM_DTYPE),
        jax.ShapeDtypeStruct((B, T // cka, KV_WIDTH, cka), MM_DTYPE),
        jax.ShapeDtypeStruct((B, T, Q_WIDTH), MM_DTYPE),
        jax.ShapeDtypeStruct((B, T, KV_WIDTH), MM_DTYPE),
        jax.ShapeDtypeStruct((B, T // ckb, KV_WIDTH, ckb), MM_DTYPE),
        jax.ShapeDtypeStruct((B, T, D), MM_DTYPE),
    )
    out_specs = (
        pl.BlockSpec((1, tm, Q_WIDTH), row),
        pl.BlockSpec((1, tm, KV_WIDTH), row),
        pl.BlockSpec((1, tm // cka, KV_WIDTH, cka), chunk_index),
        pl.BlockSpec((1, tm, Q_WIDTH), row),
        pl.BlockSpec((1, tm, KV_WIDTH), row),
        pl.BlockSpec((1, tm // ckb, KV_WIDTH, ckb), chunk_index),
        pl.BlockSpec((1, tm, D), row),
    )
    blocks = (_nbytes((tm, D), F32) + 4 * _nbytes((tm, HEAD_DIM), F32) + _nbytes((tm, QKV_WIDTH), MM_DTYPE))
    ext_bytes = _nbytes((tm + 2 * POOL_HALO, D), F32)
    return pl.pallas_call(
        kernel,
        grid=grid,
        in_specs=x_specs + [_layer_block(w_qkv.shape, layer), _layer_block(qg.shape, layer),
                            _layer_block(kg.shape, layer), tab, tab, tab, tab],
        out_specs=out_specs,
        out_shape=out_shape,
        scratch_shapes=[pltpu.VMEM((tm + 2 * POOL_HALO, D), F32)],
        compiler_params=pltpu.CompilerParams(
            dimension_semantics=("parallel",) * len(grid),
            vmem_limit_bytes=_vmem_limit(blocks, _nbytes(w_qkv.shape[1:], MM_DTYPE) + ext_bytes,
                                         2 * ext_bytes + 8 * _nbytes((tm, PROJ_COLS), F32))),
        name="proj",
    )(*x_args, w_qkv, qg, kg, *tabs)


def _proj(x, xm, p, layer, tabs, *, tm, cka, ckb):
    B, S, D = x.shape
    n_tiles = S // tm
    per = tm // POOL_HALO
    last = S // POOL_HALO - 1
    row = lambda b, i: (b, i, 0)
    halo = lambda f: pl.BlockSpec((1, POOL_HALO, D), f)
    x_specs = [
        pl.BlockSpec((1, tm, D), row),
        halo(lambda b, i: (b, jnp.maximum(i * per - 1, 0), 0)),
        halo(lambda b, i: (b, N_META // POOL_HALO - 1, 0)),
        halo(lambda b, i: (b, jnp.minimum((i + 1) * per, last), 0)),
    ]
    kernel = functools.partial(_proj_real_kernel, tm=tm, n_tiles=n_tiles, seq_len=N_META + S, cka=cka, ckb=ckb)
    return _proj_call(kernel, (B, n_tiles), x_specs, (x, x, xm, x), p, layer, tabs, B, S, tm, cka, ckb,
                      row, lambda b, i: (i, 0))


def _proj_meta(xm, x, p, layer, tabs):
    B, _, D = xm.shape
    row = lambda b: (b, 0, 0)
    x_specs = [pl.BlockSpec((1, META_ROWS, D), row), pl.BlockSpec((1, POOL_HALO, D), row)]
    kernel = functools.partial(_proj_meta_kernel, seq_len=N_META + x.shape[1])
    return _proj_call(kernel, (B,), x_specs, (xm, x), p, layer, tabs, B, META_ROWS, META_ROWS,
                      META_ROWS, META_ROWS, row, lambda b: (0, 0))


def _attn_a_kernel(q_ref, k_ref, vt_ref, km_ref, vmt_ref, o_ref,
                   acc_ref, m_ref, l_ref, s_ref, p_ref, ms_ref, as_ref, ac_ref, qp_ref, *, tq, tk, n_chunks, packed):
    heads = 1 if packed else GROUP
    if packed:
        qp_ref[...] = jnp.zeros(qp_ref.shape, qp_ref.dtype)
        for h in range(GROUP):
            qp_ref[h * N_META:(h + 1) * N_META, :] = q_ref[0, 0:N_META, h * HEAD_DIM:(h + 1) * HEAD_DIM]

    def q_head(h):
        return qp_ref[...] if packed else q_ref[0, :, h * HEAD_DIM:(h + 1) * HEAD_DIM]

    def scores(c, slot, h):
        kc = k_ref[0, pl.ds(pl.multiple_of(c * tk, tk), tk), :]
        s = lax.dot_general(kc, q_head(h), _NT, preferred_element_type=F32)
        s_ref[slot, h] = s
        m_old = m_ref[h]
        m_new = jnp.maximum(m_old, jnp.max(s, axis=0, keepdims=True))
        m_ref[h] = m_new
        ms_ref[slot, h] = m_new
        as_ref[slot, h] = jnp.exp2(m_old - m_new)

    def weights(slot, h):
        a = as_ref[slot, h]
        m = ms_ref[slot, h]
        ac_ref[slot, h] = a
        part = jnp.zeros((8, tq), F32)
        for r in range(0, tk, WEIGHT_ROWS):
            p = jnp.exp2(s_ref[slot, h, r:r + WEIGHT_ROWS, :] - m)
            p_ref[slot, h, r:r + WEIGHT_ROWS, :] = p.astype(p_ref.dtype)
            for i in range(WEIGHT_ROWS // 8):
                part = part + p[i * 8:(i + 1) * 8, :]
        l_ref[h] = a * l_ref[h] + jnp.sum(part, axis=0, keepdims=True)

    def values(c, slot, h):
        pv = jnp.dot(vt_ref[0, c], p_ref[slot, h], preferred_element_type=F32)
        acc_ref[h] = ac_ref[slot, h] * acc_ref[h] + pv

    key_row = lax.broadcasted_iota(jnp.int32, (META_ROWS, tq), 0)
    s_meta = [lax.dot_general(km_ref[0], q_head(h), _NT, preferred_element_type=F32) for h in range(heads)]
    p_meta = []
    for h in range(heads):
        s = jnp.where(key_row < N_META, s_meta[h], NEG)
        m = jnp.max(s, axis=0, keepdims=True)
        p = jnp.exp2(s - m)
        m_ref[h] = m
        l_ref[h] = jnp.sum(p, axis=0, keepdims=True)
        p_meta.append(p.astype(MM_DTYPE))
    for h in range(heads):
        acc_ref[h] = jnp.dot(vmt_ref[0, 0], p_meta[h], preferred_element_type=F32)
    for h in range(heads):
        scores(0, 0, h)
    for h in range(heads):
        scores(1, 1, h)
        weights(0, h)

    def trip(c, parity, with_scores=True):
        for h in range(heads):
            if with_scores:
                scores(c + 1, 1 - parity, h)
            values(c - 1, 1 - parity, h)
            weights(parity, h)

    def body(j, carry):
        for u in range(LOOP_TRIPS):
            trip(LOOP_TRIPS * j + 1 + u, (1 + u) % 2)
        return carry

    looped = (n_chunks - 2) // LOOP_TRIPS * LOOP_TRIPS
    lax.fori_loop(0, looped // LOOP_TRIPS, body, 0)
    for c in range(looped + 1, n_chunks):
        trip(c, c % 2, with_scores=c + 1 < n_chunks)
    for h in range(heads):
        values(n_chunks - 1, (n_chunks - 1) % 2, h)
    if packed:
        o = (acc_ref[0] / l_ref[0]).T.astype(o_ref.dtype)
        o_ref[0] = jnp.zeros(o_ref.shape[1:], o_ref.dtype)
        for h in range(GROUP):
            o_ref[0, 0:N_META, h * HEAD_DIM:(h + 1) * HEAD_DIM] = o[h * N_META:(h + 1) * N_META, :]
        return
    for h in range(GROUP):
        o = acc_ref[h] / l_ref[h]
        o_ref[0, :, h * HEAD_DIM:(h + 1) * HEAD_DIM] = o.T.astype(o_ref.dtype)


def _attn_a(q, k, vt, km, vmt, *, tq, tk, packed=False):
    B, Tq, _ = q.shape
    S = k.shape[1]
    n_chunks = S // tk
    assert n_chunks % 2 == 0
    blocks = (_nbytes((tq, GROUP_WIDTH), MM_DTYPE) * 2 + 2 * _nbytes((S, HEAD_DIM), MM_DTYPE)
              + 2 * _nbytes((META_ROWS, HEAD_DIM), MM_DTYPE))
    heads = 1 if packed else GROUP
    stat = pltpu.VMEM((2, heads, 1, tq), F32)
    scratch = (heads * (_nbytes((HEAD_DIM, tq), F32) + 8 * _nbytes((8, tq), F32))
               + 2 * heads * (_nbytes((tk, tq), F32) + _nbytes((tk, tq), MM_DTYPE)))
    return pl.pallas_call(
        functools.partial(_attn_a_kernel, tq=tq, tk=tk, n_chunks=n_chunks, packed=packed),
        grid=(B, N_KV, Tq // tq),
        in_specs=[
            pl.BlockSpec((1, tq, GROUP * HEAD_DIM), lambda b, g, i: (b, i, g)),
            pl.BlockSpec((1, S, HEAD_DIM), lambda b, g, i: (b, 0, g)),
            pl.BlockSpec((1, n_chunks, HEAD_DIM, tk), lambda b, g, i: (b, 0, g, 0)),
            pl.BlockSpec((1, META_ROWS, HEAD_DIM), lambda b, g, i: (b, 0, g)),
            pl.BlockSpec((1, 1, HEAD_DIM, META_ROWS), lambda b, g, i: (b, 0, g, 0)),
        ],
        out_specs=pl.BlockSpec((1, tq, GROUP * HEAD_DIM), lambda b, g, i: (b, i, g)),
        out_shape=jax.ShapeDtypeStruct((B, Tq, N_HEADS * HEAD_DIM), MM_DTYPE),
        scratch_shapes=[
            pltpu.VMEM((heads, HEAD_DIM, tq), F32),
            pltpu.VMEM((heads, 1, tq), F32),
            pltpu.VMEM((heads, 1, tq), F32),
            pltpu.VMEM((2, heads, tk, tq), F32),
            pltpu.VMEM((2, heads, tk, tq), MM_DTYPE),
            stat, stat, stat,
            pltpu.VMEM((META_ROWS, HEAD_DIM), MM_DTYPE),
        ],
        compiler_params=pltpu.CompilerParams(
            dimension_semantics=("parallel", "parallel", "parallel"),
            vmem_limit_bytes=_vmem_limit(blocks, scratch, 10 * _nbytes((tk, tq), F32))),
        name="attn_global",
    )(q, k, vt, km, vmt)


def _win_attend(sink_ref, layer, q_ref, o_ref, problems):
    g = pl.program_id(1)
    scores = [[lax.dot_general(k_all, q_ref[0, col0:col0 + width, h * HEAD_DIM:(h + 1) * HEAD_DIM], _NT,
                               preferred_element_type=F32) for h in range(GROUP)]
              for col0, width, k_all, _, _ in problems]
    for (col0, width, _, vt_all, mask), s_heads in zip(problems, scores):
        for h in range(GROUP):
            s = jnp.where(mask, s_heads[h], NEG)
            sink = sink_ref[layer, g * GROUP + h] * LOG2E
            m = jnp.maximum(jnp.max(s, axis=0, keepdims=True), sink)
            p = jnp.exp2(s - m)
            den = jnp.sum(p, axis=0, keepdims=True) + jnp.exp2(sink - m)
            o = jnp.dot(vt_all, p.astype(MM_DTYPE), preferred_element_type=F32) / den
            o_ref[0, col0:col0 + width, h * HEAD_DIM:(h + 1) * HEAD_DIM] = o.T.astype(o_ref.dtype)


def _attn_b_kernel(sink_ref, q_ref, kp_ref, kc_ref, kn_ref, vp_ref, vc_ref, vn_ref, km_ref, vmt_ref, o_ref,
                   *, tq, n_tiles, layer):
    i = pl.program_id(2)
    nb = tq // WIN_BLK
    n_sub = tq // WIN_SUB
    k_real = jnp.concatenate([kp_ref[0], kc_ref[0], kn_ref[0]], axis=0)
    vt_real = jnp.concatenate([vp_ref[0, 0]] + [vc_ref[0, j] for j in range(nb)] + [vn_ref[0, 0]], axis=1)
    n_keys = META_ROWS + 2 * WIN_BLK + WIN_SUB
    r = lax.broadcasted_iota(jnp.int32, (n_keys, WIN_SUB), 0)
    c = lax.broadcasted_iota(jnp.int32, (n_keys, WIN_SUB), 1)
    rel = r - c - (META_ROWS + WIN_BLK)
    in_win = (rel >= -WINDOW) & (rel <= WINDOW) & (r >= META_ROWS)
    problems = []
    for j in range(n_sub):
        span = slice(j * WIN_SUB, j * WIN_SUB + WIN_SUB + 2 * WIN_BLK)
        k_all = jnp.concatenate([km_ref[0], k_real[span, :]], axis=0)
        vt_all = jnp.concatenate([vmt_ref[0, 0], vt_real[:, span]], axis=1)
        valid = in_win
        if j == 0:
            valid = valid & (r >= META_ROWS + jnp.where(i == 0, WIN_BLK, 0))
        if j == n_sub - 1:
            valid = valid & (r < n_keys - jnp.where(i == n_tiles - 1, WIN_BLK, 0))
        problems.append((j * WIN_SUB, WIN_SUB, k_all, vt_all, (r < N_META) | valid))
    _win_attend(sink_ref, layer, q_ref, o_ref, problems)


def _attn_b_meta_kernel(sink_ref, q_ref, k0_ref, v0_ref, km_ref, vmt_ref, o_ref, *, layer):
    k_all = jnp.concatenate([km_ref[0], k0_ref[0]], axis=0)
    vt_all = jnp.concatenate([vmt_ref[0, 0], v0_ref[0, 0]], axis=1)
    r = lax.broadcasted_iota(jnp.int32, (META_ROWS + WIN_BLK, META_ROWS), 0)
    c = lax.broadcasted_iota(jnp.int32, (META_ROWS + WIN_BLK, META_ROWS), 1)
    mask = (r < N_META) | ((r >= META_ROWS) & (r - META_ROWS + N_META - c <= WINDOW))
    _win_attend(sink_ref, layer, q_ref, o_ref, [(0, META_ROWS, k_all, vt_all, mask)])


def _attn_b(sink, layer, q, k, vt, km, vmt, *, tq):
    B, S, _ = q.shape
    n_tiles = S // tq
    nb = tq // WIN_BLK
    last_blk = S // WIN_BLK - 1
    kblk = lambda shape, f: pl.BlockSpec(shape, f)
    n_keys = META_ROWS + 2 * WIN_BLK + tq
    blocks = 2 * _nbytes((tq, GROUP_WIDTH), MM_DTYPE) + 2 * _nbytes((n_keys, HEAD_DIM), MM_DTYPE)
    grid_spec = pltpu.PrefetchScalarGridSpec(
        num_scalar_prefetch=1,
        grid=(B, N_KV, n_tiles),
        in_specs=[
            pl.BlockSpec((1, tq, GROUP * HEAD_DIM), lambda b, g, i, s: (b, i, g)),
            kblk((1, WIN_BLK, HEAD_DIM), lambda b, g, i, s: (b, jnp.maximum(i * nb - 1, 0), g)),
            kblk((1, tq, HEAD_DIM), lambda b, g, i, s: (b, i, g)),
            kblk((1, WIN_BLK, HEAD_DIM), lambda b, g, i, s: (b, jnp.minimum((i + 1) * nb, last_blk), g)),
            kblk((1, 1, HEAD_DIM, WIN_BLK), lambda b, g, i, s: (b, jnp.maximum(i * nb - 1, 0), g, 0)),
            kblk((1, nb, HEAD_DIM, WIN_BLK), lambda b, g, i, s: (b, i, g, 0)),
            kblk((1, 1, HEAD_DIM, WIN_BLK), lambda b, g, i, s: (b, jnp.minimum((i + 1) * nb, last_blk), g, 0)),
            kblk((1, META_ROWS, HEAD_DIM), lambda b, g, i, s: (b, 0, g)),
            kblk((1, 1, HEAD_DIM, META_ROWS), lambda b, g, i, s: (b, 0, g, 0)),
        ],
        out_specs=pl.BlockSpec((1, tq, GROUP * HEAD_DIM), lambda b, g, i, s: (b, i, g)),
    )
    return pl.pallas_call(
        functools.partial(_attn_b_kernel, tq=tq, n_tiles=n_tiles, layer=layer),
        grid_spec=grid_spec,
        out_shape=jax.ShapeDtypeStruct((B, S, N_HEADS * HEAD_DIM), MM_DTYPE),
        compiler_params=pltpu.CompilerParams(
            dimension_semantics=("parallel", "parallel", "parallel"),
            vmem_limit_bytes=_vmem_limit(blocks, 0, 8 * _nbytes((n_keys, tq), F32))),
        name="attn_window",
    )(sink, q, k, k, k, vt, vt, vt, km, vmt)


def _attn_b_meta(sink, layer, qm, k, vt, km, vmt):
    B = qm.shape[0]
    blocks = 2 * _nbytes((META_ROWS, GROUP_WIDTH), MM_DTYPE) + 4 * _nbytes((META_ROWS, HEAD_DIM), MM_DTYPE)
    grid_spec = pltpu.PrefetchScalarGridSpec(
        num_scalar_prefetch=1,
        grid=(B, N_KV),
        in_specs=[
            pl.BlockSpec((1, META_ROWS, GROUP * HEAD_DIM), lambda b, g, s: (b, 0, g)),
            pl.BlockSpec((1, WIN_BLK, HEAD_DIM), lambda b, g, s: (b, 0, g)),
            pl.BlockSpec((1, 1, HEAD_DIM, WIN_BLK), lambda b, g, s: (b, 0, g, 0)),
            pl.BlockSpec((1, META_ROWS, HEAD_DIM), lambda b, g, s: (b, 0, g)),
            pl.BlockSpec((1, 1, HEAD_DIM, META_ROWS), lambda b, g, s: (b, 0, g, 0)),
        ],
        out_specs=pl.BlockSpec((1, META_ROWS, GROUP * HEAD_DIM), lambda b, g, s: (b, 0, g)),
    )
    return pl.pallas_call(
        functools.partial(_attn_b_meta_kernel, layer=layer),
        grid_spec=grid_spec,
        out_shape=jax.ShapeDtypeStruct((B, META_ROWS, N_HEADS * HEAD_DIM), MM_DTYPE),
        compiler_params=pltpu.CompilerParams(
            dimension_semantics=("parallel", "parallel"),
            vmem_limit_bytes=_vmem_limit(blocks, 0, 8 * _nbytes((2 * META_ROWS, META_ROWS), F32))),
        name="attn_window_meta",
    )(sink, qm, k, vt, km, vmt)


def _merge_kernel(h_ref, oa_ref, ob_ref, d_ref, wg_ref, wa_ref, wb_ref, pw_ref, ps_ref, wo_ref, g_ref, b_ref,
                  out_ref, *, tm):
    def gate(hb, j):
        return jax.nn.sigmoid(jnp.dot(hb, wg_ref[0, :, j * D_MODEL:(j + 1) * D_MODEL], preferred_element_type=F32))

    sub = min(tm, SUB_ROWS)
    for r in range(0, tm, sub):
        rows = slice(r, r + sub)
        h = h_ref[0, rows, :]
        hb = h.astype(MM_DTYPE)
        ya = jnp.dot(oa_ref[0, rows, :], wa_ref[0], preferred_element_type=F32)
        yb = jnp.dot(ob_ref[0, rows, :], wb_ref[0], preferred_element_type=F32)
        yc = jnp.concatenate(
            [jnp.dot(d_ref[0, rows, gi * POOL_DIM:(gi + 1) * POOL_DIM], pw_ref[0, gi], preferred_element_type=F32)
             for gi in range(len(POOL_WINDOWS))], axis=1) * ps_ref[0]
        merged = gate(hb, 0) * ya + gate(hb, 1) * yb + gate(hb, 2) * yc
        y = jnp.dot(merged.astype(MM_DTYPE), wo_ref[0], preferred_element_type=F32)
        out_ref[0, rows, :] = _layer_norm(ALPHA * h + y, g_ref[0], b_ref[0])


def _merge(h, oa, ob, d, p, layer, *, tm):
    B, T, D = h.shape
    row = lambda bb, i: (bb, i, 0)
    params = [p[name] for name in ("wg", "wa", "wb", "pw", "ps", "wo", "ln1_g", "ln1_b")]
    wbytes = sum(_nbytes(w.shape[1:], w.dtype) for w in params)
    blocks = 2 * _nbytes((tm, D), F32) + 3 * _nbytes((tm, D), MM_DTYPE)
    return pl.pallas_call(
        functools.partial(_merge_kernel, tm=tm),
        grid=(B, T // tm),
        in_specs=[pl.BlockSpec((1, tm, D), row)] * 4 + [_layer_block(w.shape, layer) for w in params],
        out_specs=pl.BlockSpec((1, tm, D), row),
        out_shape=jax.ShapeDtypeStruct((B, T, D), F32),
        compiler_params=pltpu.CompilerParams(
            dimension_semantics=("parallel", "parallel"),
            vmem_limit_bytes=_vmem_limit(blocks, wbytes, 12 * _nbytes((min(tm, SUB_ROWS), D), F32))),
        name="merge",
    )(h, oa, ob, d, *params)


def _ffn_kernel(h_ref, wu_ref, wd_ref, g_ref, b_ref, out_ref, act_ref, *, tm):
    sub = min(tm, SUB_ROWS)
    for r in range(0, tm, sub):
        h = h_ref[0, r:r + sub, :]
        hb = h.astype(MM_DTYPE)
        for c in range(D_FF // FF_CHUNK):
            c0 = c * FF_CHUNK
            gate = jnp.dot(hb, wu_ref[0, :, c0:c0 + FF_CHUNK], preferred_element_type=F32)
            up = jnp.dot(hb, wu_ref[0, :, D_FF + c0:D_FF + c0 + FF_CHUNK], preferred_element_type=F32)
            act_ref[r:r + sub, c0:c0 + FF_CHUNK] = (jax.nn.silu(gate) * up).astype(act_ref.dtype)
        f = jnp.dot(act_ref[r:r + sub, :], wd_ref[0], preferred_element_type=F32)
        out_ref[0, r:r + sub, :] = _layer_norm(ALPHA * h + f, g_ref[0], b_ref[0])


def _ffn(h, p, layer, *, tm):
    B, T, D = h.shape
    row = lambda bb, i: (bb, i, 0)
    params = [p[name] for name in ("wu", "wd", "ln2_g", "ln2_b")]
    wbytes = sum(_nbytes(w.shape[1:], w.dtype) for w in params)
    return pl.pallas_call(
        functools.partial(_ffn_kernel, tm=tm),
        grid=(B, T // tm),
        in_specs=[pl.BlockSpec((1, tm, D), row)] + [_layer_block(w.shape, layer) for w in params],
        out_specs=pl.BlockSpec((1, tm, D), row),
        out_shape=jax.ShapeDtypeStruct((B, T, D), F32),
        scratch_shapes=[pltpu.VMEM((tm, D_FF), MM_DTYPE)],
        compiler_params=pltpu.CompilerParams(
            dimension_semantics=("parallel", "parallel"),
            vmem_limit_bytes=_vmem_limit(2 * _nbytes((tm, D), F32), wbytes + _nbytes((tm, D_FF), MM_DTYPE),
                                         6 * _nbytes((tm, D), F32))),
        name="ffn",
    )(h, *params)


def _cos_sin(p, dim):
    inv = ROPE_THETA ** (-jnp.arange(0, dim, 2, dtype=F32) / dim)
    ang = p.astype(F32)[:, None] * inv[None, :]
    return jnp.cos(ang), jnp.sin(ang)


def _rope_tables(row_cs, col_cs, pos):
    cr, sr = row_cs
    cc, sc = col_cs
    cp, sp = _cos_sin(pos, HEAD_DIM)
    ca = jnp.concatenate([cr, cr, cc, cc], axis=1)
    sa = jnp.concatenate([-sr, sr, -sc, sc], axis=1)
    cb = jnp.concatenate([cp, cp], axis=1)
    sb = jnp.concatenate([-sp, sp], axis=1)
    return ca, sa, cb, sb


def _real_tables(S):
    half = HEAD_DIM // 2
    row_cs = [jnp.repeat(x, GRID_W, axis=0) for x in _cos_sin(jnp.arange(S // GRID_W, dtype=jnp.int32), half)]
    col_cs = [jnp.tile(x, (S // GRID_W, 1)) for x in _cos_sin(jnp.arange(GRID_W, dtype=jnp.int32), half)]
    return _rope_tables(row_cs, col_cs, jnp.arange(S, dtype=jnp.int32) + N_META)


def _meta_tables():
    i = jnp.arange(META_ROWS, dtype=jnp.int32)
    half = HEAD_DIM // 2
    return _rope_tables(_cos_sin(-jnp.ones((META_ROWS,), jnp.int32), half), _cos_sin(i, half), i)


def _layer(xr, xm, p, layer, tabs_r, tabs_m):
    qa, ka, va, qb, kb, vb, d = _proj(xr, xm, p, layer, tabs_r, tm=ROW_TILE, cka=ATTN_TK, ckb=WIN_BLK)
    qam, kam, vam, qbm, kbm, vbm, dm = _proj_meta(xm, xr, p, layer, tabs_m)
    oa = _attn_a(qa, ka, va, kam, vam, tq=ATTN_TQ, tk=ATTN_TK)
    oam = _attn_a(qam, ka, va, kam, vam, tq=META_ROWS, tk=ATTN_TK, packed=True)
    ob = _attn_b(p["sink"], layer, qb, kb, vb, kbm, vbm, tq=WIN_TQ)
    obm = _attn_b_meta(p["sink"], layer, qbm, kb, vb, kbm, vbm)
    hr = _merge(xr, oa, ob, d, p, layer, tm=DENSE_TILE)
    hm = _merge(xm, oam, obm, dm, p, layer, tm=META_ROWS)
    return _ffn(hr, p, layer, tm=DENSE_TILE), _ffn(hm, p, layer, tm=META_ROWS)


def _trunk(x, meta_tokens, p, tabs_r, tabs_m):
    B, S, D = x.shape
    xm = jnp.zeros((B, META_ROWS, D), x.dtype).at[:, :N_META].set(meta_tokens.astype(x.dtype)[None])
    xr = x
    for layer in range(p["w_qkv"].shape[0]):
        xr, xm = _layer(xr, xm, p, layer, tabs_r, tabs_m)
    return xr


def _stack_params(w_in, q_norm_g, k_norm_g, sink_logit, pool_w, pool_scale,
                  w_branch_a, w_branch_b, w_out, ln1_g, ln1_b, w_up, w_down, ln2_g, ln2_b):
    mm = lambda w: w.astype(MM_DTYPE)
    vec = lambda v: v.astype(F32)[:, None, :]
    return dict(
        w_qkv=mm(w_in[:, :, :QKV_WIDTH]), wg=mm(w_in[:, :, QKV_WIDTH:]),
        qg=vec(q_norm_g) * QK_SCALE, kg=vec(k_norm_g), sink=sink_logit.astype(F32),
        pw=mm(pool_w), ps=vec(pool_scale),
        wa=mm(w_branch_a), wb=mm(w_branch_b), wo=mm(w_out),
        ln1_g=vec(ln1_g), ln1_b=vec(ln1_b),
        wu=mm(w_up), wd=mm(w_down),
        ln2_g=vec(ln2_g), ln2_b=vec(ln2_b),
    )


def kernel(x_prompt, x_sample, meta_tokens, w_in, q_norm_g, k_norm_g, sink_logit, pool_w, pool_scale,
           w_branch_a, w_branch_b, w_out, ln1_g, ln1_b, w_up, w_down, ln2_g, ln2_b):
    params = _stack_params(w_in, q_norm_g, k_norm_g, sink_logit, pool_w, pool_scale,
                           w_branch_a, w_branch_b, w_out, ln1_g, ln1_b, w_up, w_down, ln2_g, ln2_b)
    tabs_m = _meta_tables()
    tabs_s = _real_tables(x_sample.shape[1])
    tabs_p = tuple(t[:x_prompt.shape[1]] for t in tabs_s)
    y_prompt = _trunk(x_prompt, meta_tokens, params, tabs_p, tabs_m)
    y_sample = _trunk(x_sample, meta_tokens, params, tabs_s, tabs_m)
    return (y_prompt, y_sample)
```

```python
import functools

import jax
import jax.numpy as jnp
from jax import lax
from jax.experimental import pallas as pl
from jax.experimental.pallas import tpu as pltpu

D_MODEL = 1024
HEAD_DIM = 128
N_HEADS = 8
N_KV = 2
GROUP = N_HEADS // N_KV
WINDOW = 128
N_META = 16
META_ROWS = 128
GRID_W = 64
ROPE_THETA = 10000.0
POOL_WINDOWS = (2, 4, 8, 16)
POOL_DIM = D_MODEL // len(POOL_WINDOWS)
POOL_HALO = 8
D_FF = 2816
DEPTH = 2
ALPHA = (2 * DEPTH) ** 0.25
NEG = -1e30
Q_WIDTH = N_HEADS * HEAD_DIM
KV_WIDTH = N_KV * HEAD_DIM
GROUP_WIDTH = GROUP * HEAD_DIM
COL_QA = 0
COL_KVA = COL_QA + Q_WIDTH
COL_QB = COL_KVA + 2 * KV_WIDTH
COL_KVB = COL_QB + Q_WIDTH
COL_UC = COL_KVB + 2 * KV_WIDTH
QKV_WIDTH = COL_UC + D_MODEL
LOG2E = 1.4426950408889634
QK_SCALE = HEAD_DIM ** -0.5 * LOG2E

MM_DTYPE = jnp.bfloat16
F32 = jnp.float32

V7X_VMEM_BYTES = 64 * 1024 * 1024
VMEM_CAP = V7X_VMEM_BYTES - 6 * 1024 * 1024

ROW_TILE = 1024
DENSE_TILE = 1024
ATTN_TQ = 512
ATTN_TK = 512
WIN_TQ = 1024
WIN_SUB = 256
WIN_BLK = 128
FF_CHUNK = 256
SUB_ROWS = 256
PROJ_COLS = 512
WEIGHT_ROWS = 16
LOOP_TRIPS = 2

_NT = (((1,), (1,)), ((), ()))


def _vmem_limit(block_bytes, scratch_bytes=0, temp_bytes=0):
    est = 2 * block_bytes + scratch_bytes + temp_bytes + (8 << 20)
    return int(min(max(est, 16 << 20), VMEM_CAP))


def _nbytes(shape, dtype):
    n = 1
    for s in shape:
        n *= s
    return n * jnp.dtype(dtype).itemsize


def _layer_block(shape, layer):
    index = (layer,) + (0,) * (len(shape) - 1)
    return pl.BlockSpec((1,) + tuple(shape[1:]), lambda *_: index, pipeline_mode=pl.Buffered(1))


def _layer_norm(x, g, b):
    mu = jnp.mean(x, axis=-1, keepdims=True)
    xc = x - mu
    var = jnp.mean(xc * xc, axis=-1, keepdims=True)
    return xc * lax.rsqrt(var + 1e-5) * g + b


def _pool_group(ext_ref, d_ref, pos, seq_len, tm, gi):
    n = tm + 2 * POOL_HALO
    win = POOL_WINDOWS[gi]
    c0 = gi * POOL_DIM

    def ahead(x, k):
        return pltpu.roll(x, n - k, 0) if k % n else x

    s = ext_ref[:, c0:c0 + POOL_DIM]
    span = 1
    while span < win:
        s = s + ahead(s, span)
        span *= 2
    acc = ahead(s, POOL_HALO - win // 2)[0:tm]
    lo = jnp.maximum(pos - win // 2, 0)
    hi = jnp.minimum(pos - win // 2 + win, seq_len)
    mean = acc / (hi - lo).astype(F32)
    centre = ext_ref[POOL_HALO:POOL_HALO + tm, c0:c0 + POOL_DIM]
    d_ref[0, :, c0:c0 + POOL_DIM] = (mean - centre).astype(d_ref.dtype)


def _proj_body(x_ref, x_ext, pos, seq_len, w_ref, qg_ref, kg_ref, ca_ref, sa_ref, cb_ref, sb_ref,
               qa_ref, ka_ref, va_ref, qb_ref, kb_ref, vb_ref, d_ref, ext_ref, *, tm, cka, ckb):
    xe = x_ext.astype(MM_DTYPE)
    for c0 in range(0, D_MODEL, PROJ_COLS):
        ext_ref[:, c0:c0 + PROJ_COLS] = jnp.dot(xe, w_ref[0, :, COL_UC + c0:COL_UC + c0 + PROJ_COLS],
                                                preferred_element_type=F32)
    for gi in range(len(POOL_WINDOWS)):
        _pool_group(ext_ref, d_ref, pos, seq_len, tm, gi)

    sub = min(tm, SUB_ROWS)
    qg = qg_ref[0]
    kg = kg_ref[0]
    lane = lax.broadcasted_iota(jnp.int32, (sub, HEAD_DIM), 1)
    low_quarter = (lane % (HEAD_DIM // 2)) < HEAD_DIM // 4

    def rms(x, g):
        return x * lax.rsqrt(jnp.mean(x * x, axis=-1, keepdims=True) + 1e-6) * g

    def head(u, j):
        return u[:, j * HEAD_DIM:(j + 1) * HEAD_DIM]

    for r in range(0, tm, sub):
        rows = slice(r, r + sub)
        xb = x_ref[0, rows, :].astype(MM_DTYPE)
        ca = ca_ref[rows, :]
        sa = sa_ref[rows, :]
        cb = cb_ref[rows, :]
        sb = sb_ref[rows, :]

        def rope_a(x):
            partner = jnp.where(low_quarter, pltpu.roll(x, HEAD_DIM - HEAD_DIM // 4, 1),
                                pltpu.roll(x, HEAD_DIM // 4, 1))
            return x * ca + partner * sa

        def rope_b(x):
            return x * cb + pltpu.roll(x, HEAD_DIM // 2, 1) * sb

        def cols(c0, n):
            return jnp.dot(xb, w_ref[0, :, c0:c0 + n], preferred_element_type=F32)

        def q_heads(q_ref, col0, finish):
            for c0 in range(0, Q_WIDTH, PROJ_COLS):
                u = cols(col0 + c0, PROJ_COLS)
                for j in range(PROJ_COLS // HEAD_DIM):
                    lanes = slice(c0 + j * HEAD_DIM, c0 + (j + 1) * HEAD_DIM)
                    q_ref[0, rows, lanes] = finish(head(u, j)).astype(q_ref.dtype)

        def kv_heads(k_ref, v_ref, col0, finish, ck):
            u = cols(col0, 2 * KV_WIDTH)
            for j in range(N_KV):
                k_ref[0, rows, j * HEAD_DIM:(j + 1) * HEAD_DIM] = finish(head(u, j)).astype(k_ref.dtype)
            v = u[:, KV_WIDTH:2 * KV_WIDTH]
            step = min(sub, ck)
            for o in range(0, sub, step):
                j, lane0 = (r + o) // ck, (r + o) % ck
                v_ref[0, j, :, lane0:lane0 + step] = v[o:o + step, :].T.astype(v_ref.dtype)

        q_heads(qa_ref, COL_QA, lambda x: rope_a(rms(x, qg)))
        kv_heads(ka_ref, va_ref, COL_KVA, lambda x: rope_a(rms(x, kg)), cka)
        q_heads(qb_ref, COL_QB, lambda x: rope_b(x) * QK_SCALE)
        kv_heads(kb_ref, vb_ref, COL_KVB, rope_b, ckb)


def _proj_real_kernel(x_ref, prev_ref, meta_ref, next_ref, *rest, tm, n_tiles, seq_len, cka, ckb):
    i = pl.program_id(1)
    x_ext = jnp.concatenate([jnp.where(i == 0, meta_ref[0], prev_ref[0]), x_ref[0],
                             jnp.where(i == n_tiles - 1, 0.0, next_ref[0])], axis=0)
    pos = N_META + i * tm + lax.broadcasted_iota(jnp.int32, (tm, 1), 0)
    _proj_body(x_ref, x_ext, pos, seq_len, *rest, tm=tm, cka=cka, ckb=ckb)


def _proj_meta_kernel(x_ref, first_ref, *rest, seq_len):
    x_ext = jnp.concatenate([jnp.zeros((POOL_HALO, D_MODEL), F32), x_ref[0, 0:N_META, :], first_ref[0],
                             jnp.zeros((META_ROWS - N_META, D_MODEL), F32)], axis=0)
    pos = lax.broadcasted_iota(jnp.int32, (META_ROWS, 1), 0)
    _proj_body(x_ref, x_ext, pos, seq_len, *rest, tm=META_ROWS, cka=META_ROWS, ckb=META_ROWS)


def _proj_call(kernel, grid, x_specs, x_args, p, layer, tabs, B, T, tm, cka, ckb, row, tab_index):
    D = D_MODEL
    tab = pl.BlockSpec((tm, HEAD_DIM), tab_index)
    w_qkv, qg, kg = p["w_qkv"], p["qg"], p["kg"]
    chunk_index = lambda *idx: row(*idx)[:2] + (0, 0)
    out_shape = (
        jax.ShapeDtypeStruct((B, T, Q_WIDTH), MM_DTYPE),
        jax.ShapeDtypeStruct((B, T, KV_WIDTH), MM_DTYPE),
        jax.ShapeDtypeStruct((B, T // cka, KV_WIDTH, cka), MM_DTYPE),
        jax.ShapeDtypeStruct((B, T, Q_WIDTH), MM_DTYPE),
        jax.ShapeDtypeStruct((B, T, KV_WIDTH), MM_DTYPE),
        jax.ShapeDtypeStruct((B, T // ckb, KV_WIDTH, ckb), MM_DTYPE),
        jax.ShapeDtypeStruct((B, T, D), MM_DTYPE),
    )
    out_specs = (
        pl.BlockSpec((1, tm, Q_WIDTH), row),
        pl.BlockSpec((1, tm, KV_WIDTH), row),
        pl.BlockSpec((1, tm // cka, KV_WIDTH, cka), chunk_index),
        pl.BlockSpec((1, tm, Q_WIDTH), row),
        pl.BlockSpec((1, tm, KV_WIDTH), row),
        pl.BlockSpec((1, tm // ckb, KV_WIDTH, ckb), chunk_index),
        pl.BlockSpec((1, tm, D), row),
    )
    blocks = (_nbytes((tm, D), F32) + 4 * _nbytes((tm, HEAD_DIM), F32) + _nbytes((tm, QKV_WIDTH), MM_DTYPE))
    ext_bytes = _nbytes((tm + 2 * POOL_HALO, D), F32)
    return pl.pallas_call(
        kernel,
        grid=grid,
        in_specs=x_specs + [_layer_block(w_qkv.shape, layer), _layer_block(qg.shape, layer),
                            _layer_block(kg.shape, layer), tab, tab, tab, tab],
        out_specs=out_specs,
        out_shape=out_shape,
        scratch_shapes=[pltpu.VMEM((tm + 2 * POOL_HALO, D), F32)],
        compiler_params=pltpu.CompilerParams(
            dimension_semantics=("parallel",) * len(grid),
            vmem_limit_bytes=_vmem_limit(blocks, _nbytes(w_qkv.shape[1:], MM_DTYPE) + ext_bytes,
                                         2 * ext_bytes + 8 * _nbytes((tm, PROJ_COLS), F32))),
        name="proj",
    )(*x_args, w_qkv, qg, kg, *tabs)


def _proj(x, xm, p, layer, tabs, *, tm, cka, ckb):
    B, S, D = x.shape
    n_tiles = S // tm
    per = tm // POOL_HALO
    last = S // POOL_HALO - 1
    row = lambda b, i: (b, i, 0)
    halo = lambda f: pl.BlockSpec((1, POOL_HALO, D), f)
    x_specs = [
        pl.BlockSpec((1, tm, D), row),
        halo(lambda b, i: (b, jnp.maximum(i * per - 1, 0), 0)),
        halo(lambda b, i: (b, N_META // POOL_HALO - 1, 0)),
        halo(lambda b, i: (b, jnp.minimum((i + 1) * per, last), 0)),
    ]
    kernel = functools.partial(_proj_real_kernel, tm=tm, n_tiles=n_tiles, seq_len=N_META + S, cka=cka, ckb=ckb)
    return _proj_call(kernel, (B, n_tiles), x_specs, (x, x, xm, x), p, layer, tabs, B, S, tm, cka, ckb,
                      row, lambda b, i: (i, 0))


def _proj_meta(xm, x, p, layer, tabs):
    B, _, D = xm.shape
    row = lambda b: (b, 0, 0)
    x_specs = [pl.BlockSpec((1, META_ROWS, D), row), pl.BlockSpec((1, POOL_HALO, D), row)]
    kernel = functools.partial(_proj_meta_kernel, seq_len=N_META + x.shape[1])
    return _proj_call(kernel, (B,), x_specs, (xm, x), p, layer, tabs, B, META_ROWS, META_ROWS,
                      META_ROWS, META_ROWS, row, lambda b: (0, 0))


def _attn_a_kernel(q_ref, k_ref, vt_ref, km_ref, vmt_ref, o_ref,
                   acc_ref, m_ref, l_ref, s_ref, p_ref, ms_ref, as_ref, ac_ref, qp_ref, *, tq, tk, n_chunks, packed):
    heads = 1 if packed else GROUP
    if packed:
        qp_ref[...] = jnp.zeros(qp_ref.shape, qp_ref.dtype)
        for h in range(GROUP):
            qp_ref[h * N_META:(h + 1) * N_META, :] = q_ref[0, 0:N_META, h * HEAD_DIM:(h + 1) * HEAD_DIM]

    def q_head(h):
        return qp_ref[...] if packed else q_ref[0, :, h * HEAD_DIM:(h + 1) * HEAD_DIM]

    def scores(c, slot, h):
        kc = k_ref[0, pl.ds(pl.multiple_of(c * tk, tk), tk), :]
        s = lax.dot_general(kc, q_head(h), _NT, preferred_element_type=F32)
        s_ref[slot, h] = s
        m_old = m_ref[h]
        m_new = jnp.maximum(m_old, jnp.max(s, axis=0, keepdims=True))
        m_ref[h] = m_new
        ms_ref[slot, h] = m_new
        as_ref[slot, h] = jnp.exp2(m_old - m_new)

    def weights(slot, h):
        a = as_ref[slot, h]
        m = ms_ref[slot, h]
        ac_ref[slot, h] = a
        part = jnp.zeros((8, tq), F32)
        for r in range(0, tk, WEIGHT_ROWS):
            p = jnp.exp2(s_ref[slot, h, r:r + WEIGHT_ROWS, :] - m)
            p_ref[slot, h, r:r + WEIGHT_ROWS, :] = p.astype(p_ref.dtype)
            for i in range(WEIGHT_ROWS // 8):
                part = part + p[i * 8:(i + 1) * 8, :]
        l_ref[h] = a * l_ref[h] + jnp.sum(part, axis=0, keepdims=True)

    def values(c, slot, h):
        pv = jnp.dot(vt_ref[0, c], p_ref[slot, h], preferred_element_type=F32)
        acc_ref[h] = ac_ref[slot, h] * acc_ref[h] + pv

    key_row = lax.broadcasted_iota(jnp.int32, (META_ROWS, tq), 0)
    s_meta = [lax.dot_general(km_ref[0], q_head(h), _NT, preferred_element_type=F32) for h in range(heads)]
    p_meta = []
    for h in range(heads):
        s = jnp.where(key_row < N_META, s_meta[h], NEG)
        m = jnp.max(s, axis=0, keepdims=True)
        p = jnp.exp2(s - m)
        m_ref[h] = m
        l_ref[h] = jnp.sum(p, axis=0, keepdims=True)
        p_meta.append(p.astype(MM_DTYPE))
    for h in range(heads):
        acc_ref[h] = jnp.dot(vmt_ref[0, 0], p_meta[h], preferred_element_type=F32)
    for h in range(heads):
        scores(0, 0, h)
    for h in range(heads):
        scores(1, 1, h)
        weights(0, h)

    def trip(c, parity, with_scores=True):
        for h in range(heads):
            if with_scores:
                scores(c + 1, 1 - parity, h)
            values(c - 1, 1 - parity, h)
            weights(parity, h)

    def body(j, carry):
        for u in range(LOOP_TRIPS):
            trip(LOOP_TRIPS * j + 1 + u, (1 + u) % 2)
        return carry

    looped = (n_chunks - 2) // LOOP_TRIPS * LOOP_TRIPS
    lax.fori_loop(0, looped // LOOP_TRIPS, body, 0)
    for c in range(looped + 1, n_chunks):
        trip(c, c % 2, with_scores=c + 1 < n_chunks)
    for h in range(heads):
        values(n_chunks - 1, (n_chunks - 1) % 2, h)
    if packed:
        o = (acc_ref[0] / l_ref[0]).T.astype(o_ref.dtype)
        o_ref[0] = jnp.zeros(o_ref.shape[1:], o_ref.dtype)
        for h in range(GROUP):
            o_ref[0, 0:N_META, h * HEAD_DIM:(h + 1) * HEAD_DIM] = o[h * N_META:(h + 1) * N_META, :]
        return
    for h in range(GROUP):
        o = acc_ref[h] / l_ref[h]
        o_ref[0, :, h * HEAD_DIM:(h + 1) * HEAD_DIM] = o.T.astype(o_ref.dtype)


def _attn_a(q, k, vt, km, vmt, *, tq, tk, packed=False):
    B, Tq, _ = q.shape
    S = k.shape[1]
    n_chunks = S // tk
    assert n_chunks % 2 == 0
    blocks = (_nbytes((tq, GROUP_WIDTH), MM_DTYPE) * 2 + 2 * _nbytes((S, HEAD_DIM), MM_DTYPE)
              + 2 * _nbytes((META_ROWS, HEAD_DIM), MM_DTYPE))
    heads = 1 if packed else GROUP
    stat = pltpu.VMEM((2, heads, 1, tq), F32)
    scratch = (heads * (_nbytes((HEAD_DIM, tq), F32) + 8 * _nbytes((8, tq), F32))
               + 2 * heads * (_nbytes((tk, tq), F32) + _nbytes((tk, tq), MM_DTYPE)))
    return pl.pallas_call(
        functools.partial(_attn_a_kernel, tq=tq, tk=tk, n_chunks=n_chunks, packed=packed),
        grid=(B, N_KV, Tq // tq),
        in_specs=[
            pl.BlockSpec((1, tq, GROUP * HEAD_DIM), lambda b, g, i: (b, i, g)),
            pl.BlockSpec((1, S, HEAD_DIM), lambda b, g, i: (b, 0, g)),
            pl.BlockSpec((1, n_chunks, HEAD_DIM, tk), lambda b, g, i: (b, 0, g, 0)),
            pl.BlockSpec((1, META_ROWS, HEAD_DIM), lambda b, g, i: (b, 0, g)),
            pl.BlockSpec((1, 1, HEAD_DIM, META_ROWS), lambda b, g, i: (b, 0, g, 0)),
        ],
        out_specs=pl.BlockSpec((1, tq, GROUP * HEAD_DIM), lambda b, g, i: (b, i, g)),
        out_shape=jax.ShapeDtypeStruct((B, Tq, N_HEADS * HEAD_DIM), MM_DTYPE),
        scratch_shapes=[
            pltpu.VMEM((heads, HEAD_DIM, tq), F32),
            pltpu.VMEM((heads, 1, tq), F32),
            pltpu.VMEM((heads, 1, tq), F32),
            pltpu.VMEM((2, heads, tk, tq), F32),
            pltpu.VMEM((2, heads, tk, tq), MM_DTYPE),
            stat, stat, stat,
            pltpu.VMEM((META_ROWS, HEAD_DIM), MM_DTYPE),
        ],
        compiler_params=pltpu.CompilerParams(
            dimension_semantics=("parallel", "parallel", "parallel"),
            vmem_limit_bytes=_vmem_limit(blocks, scratch, 10 * _nbytes((tk, tq), F32))),
        name="attn_global",
    )(q, k, vt, km, vmt)


def _win_attend(sink_ref, layer, q_ref, o_ref, problems):
    g = pl.program_id(1)
    scores = [[lax.dot_general(k_all, q_ref[0, col0:col0 + width, h * HEAD_DIM:(h + 1) * HEAD_DIM], _NT,
                               preferred_element_type=F32) for h in range(GROUP)]
              for col0, width, k_all, _, _ in problems]
    for (col0, width, _, vt_all, mask), s_heads in zip(problems, scores):
        for h in range(GROUP):
            s = jnp.where(mask, s_heads[h], NEG)
            sink = sink_ref[layer, g * GROUP + h] * LOG2E
            m = jnp.maximum(jnp.max(s, axis=0, keepdims=True), sink)
            p = jnp.exp2(s - m)
            den = jnp.sum(p, axis=0, keepdims=True) + jnp.exp2(sink - m)
            pb = p.astype(MM_DTYPE)
            pb = jnp.concatenate([pb[0:N_META], jnp.zeros((META_ROWS - N_META, width), MM_DTYPE), pb[N_META:]], axis=0)
            o = jnp.dot(vt_all, pb, preferred_element_type=F32) / den
            o_ref[0, col0:col0 + width, h * HEAD_DIM:(h + 1) * HEAD_DIM] = o.T.astype(o_ref.dtype)


def _attn_b_kernel(sink_ref, q_ref, kp_ref, kc_ref, kn_ref, vp_ref, vc_ref, vn_ref, km_ref, vmt_ref, o_ref,
                   *, tq, n_tiles, layer):
    i = pl.program_id(2)
    nb = tq // WIN_BLK
    n_sub = tq // WIN_SUB
    k_real = jnp.concatenate([kp_ref[0], kc_ref[0], kn_ref[0]], axis=0)
    vt_real = jnp.concatenate([vp_ref[0, 0]] + [vc_ref[0, j] for j in range(nb)] + [vn_ref[0, 0]], axis=1)
    n_keys = N_META + 2 * WIN_BLK + WIN_SUB
    r = lax.broadcasted_iota(jnp.int32, (n_keys, WIN_SUB), 0)
    c = lax.broadcasted_iota(jnp.int32, (n_keys, WIN_SUB), 1)
    rel = r - c - (N_META + WIN_BLK)
    in_win = (rel >= -WINDOW) & (rel <= WINDOW) & (r >= N_META)
    problems = []
    for j in range(n_sub):
        span = slice(j * WIN_SUB, j * WIN_SUB + WIN_SUB + 2 * WIN_BLK)
        k_all = jnp.concatenate([km_ref[0, 0:N_META, :], k_real[span, :]], axis=0)
        vt_all = jnp.concatenate([vmt_ref[0, 0], vt_real[:, span]], axis=1)
        valid = in_win
        if j == 0:
            valid = valid & (r >= N_META + jnp.where(i == 0, WIN_BLK, 0))
        if j == n_sub - 1:
            valid = valid & (r < n_keys - jnp.where(i == n_tiles - 1, WIN_BLK, 0))
        problems.append((j * WIN_SUB, WIN_SUB, k_all, vt_all, (r < N_META) | valid))
    _win_attend(sink_ref, layer, q_ref, o_ref, problems)


def _attn_b_meta_kernel(sink_ref, q_ref, k0_ref, v0_ref, km_ref, vmt_ref, o_ref, *, layer):
    k_all = jnp.concatenate([km_ref[0, 0:N_META, :], k0_ref[0]], axis=0)
    vt_all = jnp.concatenate([vmt_ref[0, 0], v0_ref[0, 0]], axis=1)
    r = lax.broadcasted_iota(jnp.int32, (N_META + WIN_BLK, META_ROWS), 0)
    c = lax.broadcasted_iota(jnp.int32, (N_META + WIN_BLK, META_ROWS), 1)
    mask = (r < N_META) | (r - c <= WINDOW)
    _win_attend(sink_ref, layer, q_ref, o_ref, [(0, META_ROWS, k_all, vt_all, mask)])


def _attn_b(sink, layer, q, k, vt, km, vmt, *, tq):
    B, S, _ = q.shape
    n_tiles = S // tq
    nb = tq // WIN_BLK
    last_blk = S // WIN_BLK - 1
    kblk = lambda shape, f: pl.BlockSpec(shape, f)
    n_keys = META_ROWS + 2 * WIN_BLK + tq
    blocks = 2 * _nbytes((tq, GROUP_WIDTH), MM_DTYPE) + 2 * _nbytes((n_keys, HEAD_DIM), MM_DTYPE)
    grid_spec = pltpu.PrefetchScalarGridSpec(
        num_scalar_prefetch=1,
        grid=(B, N_KV, n_tiles),
        in_specs=[
            pl.BlockSpec((1, tq, GROUP * HEAD_DIM), lambda b, g, i, s: (b, i, g)),
            kblk((1, WIN_BLK, HEAD_DIM), lambda b, g, i, s: (b, jnp.maximum(i * nb - 1, 0), g)),
            kblk((1, tq, HEAD_DIM), lambda b, g, i, s: (b, i, g)),
            kblk((1, WIN_BLK, HEAD_DIM), lambda b, g, i, s: (b, jnp.minimum((i + 1) * nb, last_blk), g)),
            kblk((1, 1, HEAD_DIM, WIN_BLK), lambda b, g, i, s: (b, jnp.maximum(i * nb - 1, 0), g, 0)),
            kblk((1, nb, HEAD_DIM, WIN_BLK), lambda b, g, i, s: (b, i, g, 0)),
            kblk((1, 1, HEAD_DIM, WIN_BLK), lambda b, g, i, s: (b, jnp.minimum((i + 1) * nb, last_blk), g, 0)),
            kblk((1, META_ROWS, HEAD_DIM), lambda b, g, i, s: (b, 0, g)),
            kblk((1, 1, HEAD_DIM, META_ROWS), lambda b, g, i, s: (b, 0, g, 0)),
        ],
        out_specs=pl.BlockSpec((1, tq, GROUP * HEAD_DIM), lambda b, g, i, s: (b, i, g)),
    )
    return pl.pallas_call(
        functools.partial(_attn_b_kernel, tq=tq, n_tiles=n_tiles, layer=layer),
        grid_spec=grid_spec,
        out_shape=jax.ShapeDtypeStruct((B, S, N_HEADS * HEAD_DIM), MM_DTYPE),
        compiler_params=pltpu.CompilerParams(
            dimension_semantics=("parallel", "parallel", "parallel"),
            vmem_limit_bytes=_vmem_limit(blocks, 0, 8 * _nbytes((n_keys, tq), F32))),
        name="attn_window",
    )(sink, q, k, k, k, vt, vt, vt, km, vmt)


def _attn_b_meta(sink, layer, qm, k, vt, km, vmt):
    B = qm.shape[0]
    blocks = 2 * _nbytes((META_ROWS, GROUP_WIDTH), MM_DTYPE) + 4 * _nbytes((META_ROWS, HEAD_DIM), MM_DTYPE)
    grid_spec = pltpu.PrefetchScalarGridSpec(
        num_scalar_prefetch=1,
        grid=(B, N_KV),
        in_specs=[
            pl.BlockSpec((1, META_ROWS, GROUP * HEAD_DIM), lambda b, g, s: (b, 0, g)),
            pl.BlockSpec((1, WIN_BLK, HEAD_DIM), lambda b, g, s: (b, 0, g)),
            pl.BlockSpec((1, 1, HEAD_DIM, WIN_BLK), lambda b, g, s: (b, 0, g, 0)),
            pl.BlockSpec((1, META_ROWS, HEAD_DIM), lambda b, g, s: (b, 0, g)),
            pl.BlockSpec((1, 1, HEAD_DIM, META_ROWS), lambda b, g, s: (b, 0, g, 0)),
        ],
        out_specs=pl.BlockSpec((1, META_ROWS, GROUP * HEAD_DIM), lambda b, g, s: (b, 0, g)),
    )
    return pl.pallas_call(
        functools.partial(_attn_b_meta_kernel, layer=layer),
        grid_spec=grid_spec,
        out_shape=jax.ShapeDtypeStruct((B, META_ROWS, N_HEADS * HEAD_DIM), MM_DTYPE),
        compiler_params=pltpu.CompilerParams(
            dimension_semantics=("parallel", "parallel"),
            vmem_limit_bytes=_vmem_limit(blocks, 0, 8 * _nbytes((2 * META_ROWS, META_ROWS), F32))),
        name="attn_window_meta",
    )(sink, qm, k, vt, km, vmt)


def _merge_kernel(h_ref, oa_ref, ob_ref, d_ref, wg_ref, wa_ref, wb_ref, pw_ref, ps_ref, wo_ref, g_ref, b_ref,
                  out_ref, *, tm):
    def gate(hb, j):
        return jax.nn.sigmoid(jnp.dot(hb, wg_ref[0, :, j * D_MODEL:(j + 1) * D_MODEL], preferred_element_type=F32))

    sub = min(tm, SUB_ROWS)
    for r in range(0, tm, sub):
        rows = slice(r, r + sub)
        h = h_ref[0, rows, :]
        hb = h.astype(MM_DTYPE)
        ya = jnp.dot(oa_ref[0, rows, :], wa_ref[0], preferred_element_type=F32)
        yb = jnp.dot(ob_ref[0, rows, :], wb_ref[0], preferred_element_type=F32)
        yc = jnp.concatenate(
            [jnp.dot(d_ref[0, rows, gi * POOL_DIM:(gi + 1) * POOL_DIM], pw_ref[0, gi], preferred_element_type=F32)
             for gi in range(len(POOL_WINDOWS))], axis=1) * ps_ref[0]
        merged = gate(hb, 0) * ya + gate(hb, 1) * yb + gate(hb, 2) * yc
        y = jnp.dot(merged.astype(MM_DTYPE), wo_ref[0], preferred_element_type=F32)
        out_ref[0, rows, :] = _layer_norm(ALPHA * h + y, g_ref[0], b_ref[0])


def _merge(h, oa, ob, d, p, layer, *, tm):
    B, T, D = h.shape
    row = lambda bb, i: (bb, i, 0)
    params = [p[name] for name in ("wg", "wa", "wb", "pw", "ps", "wo", "ln1_g", "ln1_b")]
    wbytes = sum(_nbytes(w.shape[1:], w.dtype) for w in params)
    blocks = 2 * _nbytes((tm, D), F32) + 3 * _nbytes((tm, D), MM_DTYPE)
    return pl.pallas_call(
        functools.partial(_merge_kernel, tm=tm),
        grid=(B, T // tm),
        in_specs=[pl.BlockSpec((1, tm, D), row)] * 4 + [_layer_block(w.shape, layer) for w in params],
        out_specs=pl.BlockSpec((1, tm, D), row),
        out_shape=jax.ShapeDtypeStruct((B, T, D), F32),
        compiler_params=pltpu.CompilerParams(
            dimension_semantics=("parallel", "parallel"),
            vmem_limit_bytes=_vmem_limit(blocks, wbytes, 12 * _nbytes((min(tm, SUB_ROWS), D), F32))),
        name="merge",
    )(h, oa, ob, d, *params)


def _ffn_kernel(h_ref, wu_ref, wd_ref, g_ref, b_ref, out_ref, act_ref, *, tm):
    sub = min(tm, SUB_ROWS)
    for r in range(0, tm, sub):
        h = h_ref[0, r:r + sub, :]
        hb = h.astype(MM_DTYPE)
        for c in range(D_FF // FF_CHUNK):
            c0 = c * FF_CHUNK
            gate = jnp.dot(hb, wu_ref[0, :, c0:c0 + FF_CHUNK], preferred_element_type=F32)
            up = jnp.dot(hb, wu_ref[0, :, D_FF + c0:D_FF + c0 + FF_CHUNK], preferred_element_type=F32)
            act_ref[r:r + sub, c0:c0 + FF_CHUNK] = (jax.nn.silu(gate) * up).astype(act_ref.dtype)
        f = jnp.dot(act_ref[r:r + sub, :], wd_ref[0], preferred_element_type=F32)
        out_ref[0, r:r + sub, :] = _layer_norm(ALPHA * h + f, g_ref[0], b_ref[0])


def _ffn(h, p, layer, *, tm):
    B, T, D = h.shape
    row = lambda bb, i: (bb, i, 0)
    params = [p[name] for name in ("wu", "wd", "ln2_g", "ln2_b")]
    wbytes = sum(_nbytes(w.shape[1:], w.dtype) for w in params)
    return pl.pallas_call(
        functools.partial(_ffn_kernel, tm=tm),
        grid=(B, T // tm),
        in_specs=[pl.BlockSpec((1, tm, D), row)] + [_layer_block(w.shape, layer) for w in params],
        out_specs=pl.BlockSpec((1, tm, D), row),
        out_shape=jax.ShapeDtypeStruct((B, T, D), F32),
        scratch_shapes=[pltpu.VMEM((tm, D_FF), MM_DTYPE)],
        compiler_params=pltpu.CompilerParams(
            dimension_semantics=("parallel", "parallel"),
            vmem_limit_bytes=_vmem_limit(2 * _nbytes((tm, D), F32), wbytes + _nbytes((tm, D_FF), MM_DTYPE),
                                         6 * _nbytes((tm, D), F32))),
        name="ffn",
    )(h, *params)


def _cos_sin(p, dim):
    inv = ROPE_THETA ** (-jnp.arange(0, dim, 2, dtype=F32) / dim)
    ang = p.astype(F32)[:, None] * inv[None, :]
    return jnp.cos(ang), jnp.sin(ang)


def _rope_tables(row_cs, col_cs, pos):
    cr, sr = row_cs
    cc, sc = col_cs
    cp, sp = _cos_sin(pos, HEAD_DIM)
    ca = jnp.concatenate([cr, cr, cc, cc], axis=1)
    sa = jnp.concatenate([-sr, sr, -sc, sc], axis=1)
    cb = jnp.concatenate([cp, cp], axis=1)
    sb = jnp.concatenate([-sp, sp], axis=1)
    return ca, sa, cb, sb


def _real_tables(S):
    half = HEAD_DIM // 2
    row_cs = [jnp.repeat(x, GRID_W, axis=0) for x in _cos_sin(jnp.arange(S // GRID_W, dtype=jnp.int32), half)]
    col_cs = [jnp.tile(x, (S // GRID_W, 1)) for x in _cos_sin(jnp.arange(GRID_W, dtype=jnp.int32), half)]
    return _rope_tables(row_cs, col_cs, jnp.arange(S, dtype=jnp.int32) + N_META)


def _meta_tables():
    i = jnp.arange(META_ROWS, dtype=jnp.int32)
    half = HEAD_DIM // 2
    return _rope_tables(_cos_sin(-jnp.ones((META_ROWS,), jnp.int32), half), _cos_sin(i, half), i)


def _layer(xr, xm, p, layer, tabs_r, tabs_m):
    qa, ka, va, qb, kb, vb, d = _proj(xr, xm, p, layer, tabs_r, tm=ROW_TILE, cka=ATTN_TK, ckb=WIN_BLK)
    qam, kam, vam, qbm, kbm, vbm, dm = _proj_meta(xm, xr, p, layer, tabs_m)
    oa = _attn_a(qa, ka, va, kam, vam, tq=ATTN_TQ, tk=ATTN_TK)
    oam = _attn_a(qam, ka, va, kam, vam, tq=META_ROWS, tk=ATTN_TK, packed=True)
    ob = _attn_b(p["sink"], layer, qb, kb, vb, kbm, vbm, tq=WIN_TQ)
    obm = _attn_b_meta(p["sink"], layer, qbm, kb, vb, kbm, vbm)
    hr = _merge(xr, oa, ob, d, p, layer, tm=DENSE_TILE)
    hm = _merge(xm, oam, obm, dm, p, layer, tm=META_ROWS)
    return _ffn(hr, p, layer, tm=DENSE_TILE), _ffn(hm, p, layer, tm=META_ROWS)


def _trunk(x, meta_tokens, p, tabs_r, tabs_m):
    B, S, D = x.shape
    xm = jnp.zeros((B, META_ROWS, D), x.dtype).at[:, :N_META].set(meta_tokens.astype(x.dtype)[None])
    xr = x
    for layer in range(p["w_qkv"].shape[0]):
        xr, xm = _layer(xr, xm, p, layer, tabs_r, tabs_m)
    return xr


def _stack_params(w_in, q_norm_g, k_norm_g, sink_logit, pool_w, pool_scale,
                  w_branch_a, w_branch_b, w_out, ln1_g, ln1_b, w_up, w_down, ln2_g, ln2_b):
    mm = lambda w: w.astype(MM_DTYPE)
    vec = lambda v: v.astype(F32)[:, None, :]
    return dict(
        w_qkv=mm(w_in[:, :, :QKV_WIDTH]), wg=mm(w_in[:, :, QKV_WIDTH:]),
        qg=vec(q_norm_g) * QK_SCALE, kg=vec(k_norm_g), sink=sink_logit.astype(F32),
        pw=mm(pool_w), ps=vec(pool_scale),
        wa=mm(w_branch_a), wb=mm(w_branch_b), wo=mm(w_out),
        ln1_g=vec(ln1_g), ln1_b=vec(ln1_b),
        wu=mm(w_up), wd=mm(w_down),
        ln2_g=vec(ln2_g), ln2_b=vec(ln2_b),
    )


def kernel(x_prompt, x_sample, meta_tokens, w_in, q_norm_g, k_norm_g, sink_logit, pool_w, pool_scale,
           w_branch_a, w_branch_b, w_out, ln1_g, ln1_b, w_up, w_down, ln2_g, ln2_b):
    params = _stack_params(w_in, q_norm_g, k_norm_g, sink_logit, pool_w, pool_scale,
                           w_branch_a, w_branch_b, w_out, ln1_g, ln1_b, w_up, w_down, ln2_g, ln2_b)
    tabs_m = _meta_tables()
    tabs_s = _real_tables(x_sample.shape[1])
    tabs_p = tuple(t[:x_prompt.shape[1]] for t in tabs_s)
    y_prompt = _trunk(x_prompt, meta_tokens, params, tabs_p, tabs_m)
    y_sample = _trunk(x_sample, meta_tokens, params, tabs_s, tabs_m)
    return (y_prompt, y_sample)
```

```python
import functools

import jax
import jax.numpy as jnp
from jax import lax
from jax.experimental import pallas as pl
from jax.experimental.pallas import tpu as pltpu

D_MODEL = 1024
HEAD_DIM = 128
N_HEADS = 8
N_KV = 2
GROUP = N_HEADS // N_KV
WINDOW = 128
N_META = 16
META_ROWS = 128
GRID_W = 64
ROPE_THETA = 10000.0
POOL_WINDOWS = (2, 4, 8, 16)
POOL_DIM = D_MODEL // len(POOL_WINDOWS)
POOL_HALO = 8
D_FF = 2816
DEPTH = 2
ALPHA = (2 * DEPTH) ** 0.25
NEG = -1e30
Q_WIDTH = N_HEADS * HEAD_DIM
KV_WIDTH = N_KV * HEAD_DIM
GROUP_WIDTH = GROUP * HEAD_DIM
COL_QA = 0
COL_KVA = COL_QA + Q_WIDTH
COL_QB = COL_KVA + 2 * KV_WIDTH
COL_KVB = COL_QB + Q_WIDTH
COL_UC = COL_KVB + 2 * KV_WIDTH
QKV_WIDTH = COL_UC + D_MODEL
LOG2E = 1.4426950408889634
QK_SCALE = HEAD_DIM ** -0.5 * LOG2E

MM_DTYPE = jnp.bfloat16
F32 = jnp.float32

V7X_VMEM_BYTES = 64 * 1024 * 1024
VMEM_CAP = V7X_VMEM_BYTES - 6 * 1024 * 1024

ROW_TILE = 1024
DENSE_TILE = 1024
ATTN_TQ = 512
ATTN_TK = 512
WIN_TQ = 1024
WIN_SUB = 256
WIN_BLK = 128
FF_CHUNK = 256
SUB_ROWS = 256
PROJ_COLS = 512
WEIGHT_ROWS = 16
LOOP_TRIPS = 2

_NT = (((1,), (1,)), ((), ()))


def _vmem_limit(block_bytes, scratch_bytes=0, temp_bytes=0):
    est = 2 * block_bytes + scratch_bytes + temp_bytes + (8 << 20)
    return int(min(max(est, 16 << 20), VMEM_CAP))


def _nbytes(shape, dtype):
    n = 1
    for s in shape:
        n *= s
    return n * jnp.dtype(dtype).itemsize


def _layer_block(shape, layer):
    index = (layer,) + (0,) * (len(shape) - 1)
    return pl.BlockSpec((1,) + tuple(shape[1:]), lambda *_: index, pipeline_mode=pl.Buffered(1))


def _layer_norm(x, g, b):
    mu = jnp.mean(x, axis=-1, keepdims=True)
    xc = x - mu
    var = jnp.mean(xc * xc, axis=-1, keepdims=True)
    return xc * lax.rsqrt(var + 1e-5) * g + b


def _pool_group(ext_ref, d_ref, pos, seq_len, tm, gi):
    n = tm + 2 * POOL_HALO
    win = POOL_WINDOWS[gi]
    c0 = gi * POOL_DIM

    def ahead(x, k):
        return pltpu.roll(x, n - k, 0) if k % n else x

    s = ext_ref[:, c0:c0 + POOL_DIM]
    span = 1
    while span < win:
        s = s + ahead(s, span)
        span *= 2
    acc = ahead(s, POOL_HALO - win // 2)[0:tm]
    lo = jnp.maximum(pos - win // 2, 0)
    hi = jnp.minimum(pos - win // 2 + win, seq_len)
    mean = acc / (hi - lo).astype(F32)
    centre = ext_ref[POOL_HALO:POOL_HALO + tm, c0:c0 + POOL_DIM]
    d_ref[0, :, c0:c0 + POOL_DIM] = (mean - centre).astype(d_ref.dtype)


def _proj_body(x_ref, x_ext, pos, seq_len, w_ref, qg_ref, kg_ref, ca_ref, sa_ref, cb_ref, sb_ref,
               qa_ref, ka_ref, va_ref, qb_ref, kb_ref, vb_ref, d_ref, ext_ref, *, tm, cka, ckb):
    xe = x_ext.astype(MM_DTYPE)
    for c0 in range(0, D_MODEL, PROJ_COLS):
        ext_ref[:, c0:c0 + PROJ_COLS] = jnp.dot(xe, w_ref[0, :, COL_UC + c0:COL_UC + c0 + PROJ_COLS],
                                                preferred_element_type=F32)
    for gi in range(len(POOL_WINDOWS)):
        _pool_group(ext_ref, d_ref, pos, seq_len, tm, gi)

    sub = min(tm, SUB_ROWS)
    qg = qg_ref[0]
    kg = kg_ref[0]
    lane = lax.broadcasted_iota(jnp.int32, (sub, HEAD_DIM), 1)
    low_quarter = (lane % (HEAD_DIM // 2)) < HEAD_DIM // 4

    def rms(x, g):
        return x * lax.rsqrt(jnp.mean(x * x, axis=-1, keepdims=True) + 1e-6) * g

    def head(u, j):
        return u[:, j * HEAD_DIM:(j + 1) * HEAD_DIM]

    for r in range(0, tm, sub):
        rows = slice(r, r + sub)
        xb = x_ref[0, rows, :].astype(MM_DTYPE)
        ca = ca_ref[rows, :]
        sa = sa_ref[rows, :]
        cb = cb_ref[rows, :]
        sb = sb_ref[rows, :]

        def rope_a(x):
            partner = jnp.where(low_quarter, pltpu.roll(x, HEAD_DIM - HEAD_DIM // 4, 1),
                                pltpu.roll(x, HEAD_DIM // 4, 1))
            return x * ca + partner * sa

        def rope_b(x):
            return x * cb + pltpu.roll(x, HEAD_DIM // 2, 1) * sb

        def cols(c0, n):
            return jnp.dot(xb, w_ref[0, :, c0:c0 + n], preferred_element_type=F32)

        def q_heads(q_ref, col0, finish):
            for c0 in range(0, Q_WIDTH, PROJ_COLS):
                u = cols(col0 + c0, PROJ_COLS)
                for j in range(PROJ_COLS // HEAD_DIM):
                    lanes = slice(c0 + j * HEAD_DIM, c0 + (j + 1) * HEAD_DIM)
                    q_ref[0, rows, lanes] = finish(head(u, j)).astype(q_ref.dtype)

        def kv_heads(k_ref, v_ref, col0, finish, ck):
            u = cols(col0, 2 * KV_WIDTH)
            for j in range(N_KV):
                k_ref[0, rows, j * HEAD_DIM:(j + 1) * HEAD_DIM] = finish(head(u, j)).astype(k_ref.dtype)
            v = u[:, KV_WIDTH:2 * KV_WIDTH]
            step = min(sub, ck)
            for o in range(0, sub, step):
                j, lane0 = (r + o) // ck, (r + o) % ck
                v_ref[0, j, :, lane0:lane0 + step] = v[o:o + step, :].T.astype(v_ref.dtype)

        q_heads(qa_ref, COL_QA, lambda x: rope_a(rms(x, qg)))
        kv_heads(ka_ref, va_ref, COL_KVA, lambda x: rope_a(rms(x, kg)), cka)
        q_heads(qb_ref, COL_QB, lambda x: rope_b(x) * QK_SCALE)
        kv_heads(kb_ref, vb_ref, COL_KVB, rope_b, ckb)


def _proj_real_kernel(x_ref, prev_ref, meta_ref, next_ref, *rest, tm, n_tiles, seq_len, cka, ckb):
    i = pl.program_id(1)
    x_ext = jnp.concatenate([jnp.where(i == 0, meta_ref[0], prev_ref[0]), x_ref[0],
                             jnp.where(i == n_tiles - 1, 0.0, next_ref[0])], axis=0)
    pos = N_META + i * tm + lax.broadcasted_iota(jnp.int32, (tm, 1), 0)
    _proj_body(x_ref, x_ext, pos, seq_len, *rest, tm=tm, cka=cka, ckb=ckb)


def _proj_meta_kernel(x_ref, first_ref, *rest, seq_len):
    x_ext = jnp.concatenate([jnp.zeros((POOL_HALO, D_MODEL), F32), x_ref[0, 0:N_META, :], first_ref[0],
                             jnp.zeros((META_ROWS - N_META, D_MODEL), F32)], axis=0)
    pos = lax.broadcasted_iota(jnp.int32, (META_ROWS, 1), 0)
    _proj_body(x_ref, x_ext, pos, seq_len, *rest, tm=META_ROWS, cka=META_ROWS, ckb=META_ROWS)


def _proj_call(kernel, grid, x_specs, x_args, p, layer, tabs, B, T, tm, cka, ckb, row, tab_index):
    D = D_MODEL
    tab = pl.BlockSpec((tm, HEAD_DIM), tab_index)
    w_qkv, qg, kg = p["w_qkv"], p["qg"], p["kg"]
    chunk_index = lambda *idx: row(*idx)[:2] + (0, 0)
    out_shape = (
        jax.ShapeDtypeStruct((B, T, Q_WIDTH), MM_DTYPE),
        jax.ShapeDtypeStruct((B, T, KV_WIDTH), MM_DTYPE),
        jax.ShapeDtypeStruct((B, T // cka, KV_WIDTH, cka), MM_DTYPE),
        jax.ShapeDtypeStruct((B, T, Q_WIDTH), MM_DTYPE),
        jax.ShapeDtypeStruct((B, T, KV_WIDTH), MM_DTYPE),
        jax.ShapeDtypeStruct((B, T // ckb, KV_WIDTH, ckb), MM_DTYPE),
        jax.ShapeDtypeStruct((B, T, D), MM_DTYPE),
    )
    out_specs = (
        pl.BlockSpec((1, tm, Q_WIDTH), row),
        pl.BlockSpec((1, tm, KV_WIDTH), row),
        pl.BlockSpec((1, tm // cka, KV_WIDTH, cka), chunk_index),
        pl.BlockSpec((1, tm, Q_WIDTH), row),
        pl.BlockSpec((1, tm, KV_WIDTH), row),
        pl.BlockSpec((1, tm // ckb, KV_WIDTH, ckb), chunk_index),
        pl.BlockSpec((1, tm, D), row),
    )
    blocks = (_nbytes((tm, D), F32) + 4 * _nbytes((tm, HEAD_DIM), F32) + _nbytes((tm, QKV_WIDTH), MM_DTYPE))
    ext_bytes = _nbytes((tm + 2 * POOL_HALO, D), F32)
    return pl.pallas_call(
        kernel,
        grid=grid,
        in_specs=x_specs + [_layer_block(w_qkv.shape, layer), _layer_block(qg.shape, layer),
                            _layer_block(kg.shape, layer), tab, tab, tab, tab],
        out_specs=out_specs,
        out_shape=out_shape,
        scratch_shapes=[pltpu.VMEM((tm + 2 * POOL_HALO, D), F32)],
        compiler_params=pltpu.CompilerParams(
            dimension_semantics=("parallel",) * len(grid),
            vmem_limit_bytes=_vmem_limit(blocks, _nbytes(w_qkv.shape[1:], MM_DTYPE) + ext_bytes,
                                         2 * ext_bytes + 8 * _nbytes((tm, PROJ_COLS), F32))),
        name="proj",
    )(*x_args, w_qkv, qg, kg, *tabs)


def _proj(x, xm, p, layer, tabs, *, tm, cka, ckb):
    B, S, D = x.shape
    n_tiles = S // tm
    per = tm // POOL_HALO
    last = S // POOL_HALO - 1
    row = lambda b, i: (b, i, 0)
    halo = lambda f: pl.BlockSpec((1, POOL_HALO, D), f)
    x_specs = [
        pl.BlockSpec((1, tm, D), row),
        halo(lambda b, i: (b, jnp.maximum(i * per - 1, 0), 0)),
        halo(lambda b, i: (b, N_META // POOL_HALO - 1, 0)),
        halo(lambda b, i: (b, jnp.minimum((i + 1) * per, last), 0)),
    ]
    kernel = functools.partial(_proj_real_kernel, tm=tm, n_tiles=n_tiles, seq_len=N_META + S, cka=cka, ckb=ckb)
    return _proj_call(kernel, (B, n_tiles), x_specs, (x, x, xm, x), p, layer, tabs, B, S, tm, cka, ckb,
                      row, lambda b, i: (i, 0))


def _proj_meta(xm, x, p, layer, tabs):
    B, _, D = xm.shape
    row = lambda b: (b, 0, 0)
    x_specs = [pl.BlockSpec((1, META_ROWS, D), row), pl.BlockSpec((1, POOL_HALO, D), row)]
    kernel = functools.partial(_proj_meta_kernel, seq_len=N_META + x.shape[1])
    return _proj_call(kernel, (B,), x_specs, (xm, x), p, layer, tabs, B, META_ROWS, META_ROWS,
                      META_ROWS, META_ROWS, row, lambda b: (0, 0))


def _attn_a_kernel(q_ref, k_ref, vt_ref, km_ref, vmt_ref, o_ref,
                   acc_ref, m_ref, l_ref, s_ref, p_ref, ms_ref, as_ref, ac_ref, qp_ref, *, tq, tk, n_chunks, packed):
    heads = 1 if packed else GROUP
    if packed:
        qp_ref[...] = jnp.zeros(qp_ref.shape, qp_ref.dtype)
        for h in range(GROUP):
            qp_ref[h * N_META:(h + 1) * N_META, :] = q_ref[0, 0:N_META, h * HEAD_DIM:(h + 1) * HEAD_DIM]

    def q_head(h):
        return qp_ref[...] if packed else q_ref[0, :, h * HEAD_DIM:(h + 1) * HEAD_DIM]

    def scores(c, slot, h):
        kc = k_ref[0, pl.ds(pl.multiple_of(c * tk, tk), tk), :]
        s = lax.dot_general(kc, q_head(h), _NT, preferred_element_type=F32)
        s_ref[slot, h] = s
        m_old = m_ref[h]
        m_new = jnp.maximum(m_old, jnp.max(s, axis=0, keepdims=True))
        m_ref[h] = m_new
        ms_ref[slot, h] = m_new
        as_ref[slot, h] = jnp.exp2(m_old - m_new)

    def weights(slot, h):
        a = as_ref[slot, h]
        m = ms_ref[slot, h]
        ac_ref[slot, h] = a
        part = jnp.zeros((8, tq), F32)
        for r in range(0, tk, WEIGHT_ROWS):
            p = jnp.exp2(s_ref[slot, h, r:r + WEIGHT_ROWS, :] - m)
            p_ref[slot, h, r:r + WEIGHT_ROWS, :] = p.astype(p_ref.dtype)
            for i in range(WEIGHT_ROWS // 8):
                part = part + p[i * 8:(i + 1) * 8, :]
        l_ref[h] = a * l_ref[h] + jnp.sum(part, axis=0, keepdims=True)

    def values(c, slot, h):
        pv = jnp.dot(vt_ref[0, c], p_ref[slot, h], preferred_element_type=F32)
        acc_ref[h] = ac_ref[slot, h] * acc_ref[h] + pv

    key_row = lax.broadcasted_iota(jnp.int32, (META_ROWS, tq), 0)
    s_meta = [lax.dot_general(km_ref[0], q_head(h), _NT, preferred_element_type=F32) for h in range(heads)]
    p_meta = []
    for h in range(heads):
        s = jnp.where(key_row < N_META, s_meta[h], NEG)
        m = jnp.max(s, axis=0, keepdims=True)
        p = jnp.exp2(s - m)
        m_ref[h] = m
        l_ref[h] = jnp.sum(p, axis=0, keepdims=True)
        p_meta.append(p.astype(MM_DTYPE))
    for h in range(heads):
        acc_ref[h] = jnp.dot(vmt_ref[0, 0], p_meta[h], preferred_element_type=F32)
    for h in range(heads):
        scores(0, 0, h)
    for h in range(heads):
        scores(1, 1, h)
        weights(0, h)

    def trip(c, parity, with_scores=True):
        for h in range(heads):
            if with_scores:
                scores(c + 1, 1 - parity, h)
            values(c - 1, 1 - parity, h)
            weights(parity, h)

    def body(j, carry):
        for u in range(LOOP_TRIPS):
            trip(LOOP_TRIPS * j + 1 + u, (1 + u) % 2)
        return carry

    looped = (n_chunks - 2) // LOOP_TRIPS * LOOP_TRIPS
    lax.fori_loop(0, looped // LOOP_TRIPS, body, 0)
    for c in range(looped + 1, n_chunks):
        trip(c, c % 2, with_scores=c + 1 < n_chunks)
    for h in range(heads):
        values(n_chunks - 1, (n_chunks - 1) % 2, h)
    if packed:
        o = (acc_ref[0] / l_ref[0]).T.astype(o_ref.dtype)
        o_ref[0] = jnp.zeros(o_ref.shape[1:], o_ref.dtype)
        for h in range(GROUP):
            o_ref[0, 0:N_META, h * HEAD_DIM:(h + 1) * HEAD_DIM] = o[h * N_META:(h + 1) * N_META, :]
        return
    for h in range(GROUP):
        o = acc_ref[h] / l_ref[h]
        o_ref[0, :, h * HEAD_DIM:(h + 1) * HEAD_DIM] = o.T.astype(o_ref.dtype)


def _attn_a(q, k, vt, km, vmt, *, tq, tk, packed=False):
    B, Tq, _ = q.shape
    S = k.shape[1]
    n_chunks = S // tk
    assert n_chunks % 2 == 0
    blocks = (_nbytes((tq, GROUP_WIDTH), MM_DTYPE) * 2 + 2 * _nbytes((S, HEAD_DIM), MM_DTYPE)
              + 2 * _nbytes((META_ROWS, HEAD_DIM), MM_DTYPE))
    heads = 1 if packed else GROUP
    stat = pltpu.VMEM((2, heads, 1, tq), F32)
    scratch = (heads * (_nbytes((HEAD_DIM, tq), F32) + 8 * _nbytes((8, tq), F32))
               + 2 * heads * (_nbytes((tk, tq), F32) + _nbytes((tk, tq), MM_DTYPE)))
    return pl.pallas_call(
        functools.partial(_attn_a_kernel, tq=tq, tk=tk, n_chunks=n_chunks, packed=packed),
        grid=(B, N_KV, Tq // tq),
        in_specs=[
            pl.BlockSpec((1, tq, GROUP * HEAD_DIM), lambda b, g, i: (b, i, g)),
            pl.BlockSpec((1, S, HEAD_DIM), lambda b, g, i: (b, 0, g)),
            pl.BlockSpec((1, n_chunks, HEAD_DIM, tk), lambda b, g, i: (b, 0, g, 0)),
            pl.BlockSpec((1, META_ROWS, HEAD_DIM), lambda b, g, i: (b, 0, g)),
            pl.BlockSpec((1, 1, HEAD_DIM, META_ROWS), lambda b, g, i: (b, 0, g, 0)),
        ],
        out_specs=pl.BlockSpec((1, tq, GROUP * HEAD_DIM), lambda b, g, i: (b, i, g)),
        out_shape=jax.ShapeDtypeStruct((B, Tq, N_HEADS * HEAD_DIM), MM_DTYPE),
        scratch_shapes=[
            pltpu.VMEM((heads, HEAD_DIM, tq), F32),
            pltpu.VMEM((heads, 1, tq), F32),
            pltpu.VMEM((heads, 1, tq), F32),
            pltpu.VMEM((2, heads, tk, tq), F32),
            pltpu.VMEM((2, heads, tk, tq), MM_DTYPE),
            stat, stat, stat,
            pltpu.VMEM((META_ROWS, HEAD_DIM), MM_DTYPE),
        ],
        compiler_params=pltpu.CompilerParams(
            dimension_semantics=("parallel", "parallel", "parallel"),
            vmem_limit_bytes=_vmem_limit(blocks, scratch, 10 * _nbytes((tk, tq), F32))),
        name="attn_global",
    )(q, k, vt, km, vmt)


def _win_attend(sink_ref, layer, q_ref, o_ref, problems):
    g = pl.program_id(1)
    scores = [[lax.dot_general(k_all, q_ref[0, col0:col0 + width, h * HEAD_DIM:(h + 1) * HEAD_DIM], _NT,
                               preferred_element_type=F32) for h in range(GROUP)]
              for col0, width, k_all, _, _ in problems]
    for (col0, width, _, vt_all, mask), s_heads in zip(problems, scores):
        for h in range(GROUP):
            s = jnp.where(mask, s_heads[h], NEG)
            sink = sink_ref[layer, g * GROUP + h] * LOG2E
            m = jnp.maximum(jnp.max(s, axis=0, keepdims=True), sink)
            p = jnp.exp2(s - m)
            den = jnp.sum(p, axis=0, keepdims=True) + jnp.exp2(sink - m)
            pb = p.astype(MM_DTYPE)
            pb = jnp.concatenate([pb[0:N_META], jnp.zeros((META_ROWS - N_META, width), MM_DTYPE), pb[N_META:]], axis=0)
            o = jnp.dot(vt_all, pb, preferred_element_type=F32) / den
            o_ref[0, col0:col0 + width, h * HEAD_DIM:(h + 1) * HEAD_DIM] = o.T.astype(o_ref.dtype)


def _attn_b_kernel(sink_ref, q_ref, kp_ref, kc_ref, kn_ref, vp_ref, vc_ref, vn_ref, km_ref, vmt_ref, o_ref,
                   *, tq, n_tiles, layer):
    i = pl.program_id(2)
    nb = tq // WIN_BLK
    n_sub = tq // WIN_SUB
    k_real = jnp.concatenate([kp_ref[0], kc_ref[0], kn_ref[0]], axis=0)
    vt_real = jnp.concatenate([vp_ref[0, 0]] + [vc_ref[0, j] for j in range(nb)] + [vn_ref[0, 0]], axis=1)
    n_keys = N_META + 2 * WIN_BLK + WIN_SUB
    r = lax.broadcasted_iota(jnp.int32, (n_keys, WIN_SUB), 0)
    c = lax.broadcasted_iota(jnp.int32, (n_keys, WIN_SUB), 1)
    rel = r - c - (N_META + WIN_BLK)
    in_win = (rel >= -WINDOW) & (rel <= WINDOW) & (r >= N_META)
    problems = []
    for j in range(n_sub):
        span = slice(j * WIN_SUB, j * WIN_SUB + WIN_SUB + 2 * WIN_BLK)
        k_all = jnp.concatenate([km_ref[0, 0:N_META, :], k_real[span, :]], axis=0)
        vt_all = jnp.concatenate([vmt_ref[0, 0], vt_real[:, span]], axis=1)
        valid = in_win
        if j == 0:
            valid = valid & (r >= N_META + jnp.where(i == 0, WIN_BLK, 0))
        if j == n_sub - 1:
            valid = valid & (r < n_keys - jnp.where(i == n_tiles - 1, WIN_BLK, 0))
        problems.append((j * WIN_SUB, WIN_SUB, k_all, vt_all, (r < N_META) | valid))
    _win_attend(sink_ref, layer, q_ref, o_ref, problems)


def _attn_b_meta_kernel(sink_ref, q_ref, k0_ref, v0_ref, km_ref, vmt_ref, o_ref, *, layer):
    k_all = jnp.concatenate([km_ref[0, 0:N_META, :], k0_ref[0]], axis=0)
    vt_all = jnp.concatenate([vmt_ref[0, 0], v0_ref[0, 0]], axis=1)
    r = lax.broadcasted_iota(jnp.int32, (N_META + WIN_BLK, META_ROWS), 0)
    c = lax.broadcasted_iota(jnp.int32, (N_META + WIN_BLK, META_ROWS), 1)
    mask = (r < N_META) | (r - c <= WINDOW)
    _win_attend(sink_ref, layer, q_ref, o_ref, [(0, META_ROWS, k_all, vt_all, mask)])


def _attn_b(sink, layer, q, k, vt, km, vmt, *, tq):
    B, S, _ = q.shape
    n_tiles = S // tq
    nb = tq // WIN_BLK
    last_blk = S // WIN_BLK - 1
    kblk = lambda shape, f: pl.BlockSpec(shape, f)
    n_keys = META_ROWS + 2 * WIN_BLK + tq
    blocks = 2 * _nbytes((tq, GROUP_WIDTH), MM_DTYPE) + 2 * _nbytes((n_keys, HEAD_DIM), MM_DTYPE)
    grid_spec = pltpu.PrefetchScalarGridSpec(
        num_scalar_prefetch=1,
        grid=(B, N_KV, n_tiles),
        in_specs=[
            pl.BlockSpec((1, tq, GROUP * HEAD_DIM), lambda b, g, i, s: (b, i, g)),
            kblk((1, WIN_BLK, HEAD_DIM), lambda b, g, i, s: (b, jnp.maximum(i * nb - 1, 0), g)),
            kblk((1, tq, HEAD_DIM), lambda b, g, i, s: (b, i, g)),
            kblk((1, WIN_BLK, HEAD_DIM), lambda b, g, i, s: (b, jnp.minimum((i + 1) * nb, last_blk), g)),
            kblk((1, 1, HEAD_DIM, WIN_BLK), lambda b, g, i, s: (b, jnp.maximum(i * nb - 1, 0), g, 0)),
            kblk((1, nb, HEAD_DIM, WIN_BLK), lambda b, g, i, s: (b, i, g, 0)),
            kblk((1, 1, HEAD_DIM, WIN_BLK), lambda b, g, i, s: (b, jnp.minimum((i + 1) * nb, last_blk), g, 0)),
            kblk((1, META_ROWS, HEAD_DIM), lambda b, g, i, s: (b, 0, g)),
            kblk((1, 1, HEAD_DIM, META_ROWS), lambda b, g, i, s: (b, 0, g, 0)),
        ],
        out_specs=pl.BlockSpec((1, tq, GROUP * HEAD_DIM), lambda b, g, i, s: (b, i, g)),
    )
    return pl.pallas_call(
        functools.partial(_attn_b_kernel, tq=tq, n_tiles=n_tiles, layer=layer),
        grid_spec=grid_spec,
        out_shape=jax.ShapeDtypeStruct((B, S, N_HEADS * HEAD_DIM), MM_DTYPE),
        compiler_params=pltpu.CompilerParams(
            dimension_semantics=("parallel", "parallel", "parallel"),
            vmem_limit_bytes=_vmem_limit(blocks, 0, 8 * _nbytes((n_keys, tq), F32))),
        name="attn_window",
    )(sink, q, k, k, k, vt, vt, vt, km, vmt)


def _attn_b_meta(sink, layer, qm, k, vt, km, vmt):
    B = qm.shape[0]
    blocks = 2 * _nbytes((META_ROWS, GROUP_WIDTH), MM_DTYPE) + 4 * _nbytes((META_ROWS, HEAD_DIM), MM_DTYPE)
    grid_spec = pltpu.PrefetchScalarGridSpec(
        num_scalar_prefetch=1,
        grid=(B, N_KV),
        in_specs=[
            pl.BlockSpec((1, META_ROWS, GROUP * HEAD_DIM), lambda b, g, s: (b, 0, g)),
            pl.BlockSpec((1, WIN_BLK, HEAD_DIM), lambda b, g, s: (b, 0, g)),
            pl.BlockSpec((1, 1, HEAD_DIM, WIN_BLK), lambda b, g, s: (b, 0, g, 0)),
            pl.BlockSpec((1, META_ROWS, HEAD_DIM), lambda b, g, s: (b, 0, g)),
            pl.BlockSpec((1, 1, HEAD_DIM, META_ROWS), lambda b, g, s: (b, 0, g, 0)),
        ],
        out_specs=pl.BlockSpec((1, META_ROWS, GROUP * HEAD_DIM), lambda b, g, s: (b, 0, g)),
    )
    return pl.pallas_call(
        functools.partial(_attn_b_meta_kernel, layer=layer),
        grid_spec=grid_spec,
        out_shape=jax.ShapeDtypeStruct((B, META_ROWS, N_HEADS * HEAD_DIM), MM_DTYPE),
        compiler_params=pltpu.CompilerParams(
            dimension_semantics=("parallel", "parallel"),
            vmem_limit_bytes=_vmem_limit(blocks, 0, 8 * _nbytes((2 * META_ROWS, META_ROWS), F32))),
        name="attn_window_meta",
    )(sink, qm, k, vt, km, vmt)


def _merge_kernel(h_ref, oa_ref, ob_ref, d_ref, wg_ref, wa_ref, wb_ref, pw_ref, ps_ref, wo_ref, g_ref, b_ref,
                  out_ref, *, tm):
    def gate(hb, j):
        return jax.nn.sigmoid(jnp.dot(hb, wg_ref[0, :, j * D_MODEL:(j + 1) * D_MODEL], preferred_element_type=F32))

    sub = min(tm, SUB_ROWS)
    for r in range(0, tm, sub):
        rows = slice(r, r + sub)
        h = h_ref[0, rows, :]
        hb = h.astype(MM_DTYPE)
        ya = jnp.dot(oa_ref[0, rows, :], wa_ref[0], preferred_element_type=F32)
        yb = jnp.dot(ob_ref[0, rows, :], wb_ref[0], preferred_element_type=F32)
        yc = jnp.concatenate(
            [jnp.dot(d_ref[0, rows, gi * POOL_DIM:(gi + 1) * POOL_DIM], pw_ref[0, gi], preferred_element_type=F32)
             for gi in range(len(POOL_WINDOWS))], axis=1) * ps_ref[0]
        merged = gate(hb, 0) * ya + gate(hb, 1) * yb + gate(hb, 2) * yc
        y = jnp.dot(merged.astype(MM_DTYPE), wo_ref[0], preferred_element_type=F32)
        out_ref[0, rows, :] = _layer_norm(ALPHA * h + y, g_ref[0], b_ref[0])


def _merge(h, oa, ob, d, p, layer, *, tm):
    B, T, D = h.shape
    row = lambda bb, i: (bb, i, 0)
    params = [p[name] for name in ("wg", "wa", "wb", "pw", "ps", "wo", "ln1_g", "ln1_b")]
    wbytes = sum(_nbytes(w.shape[1:], w.dtype) for w in params)
    blocks = 2 * _nbytes((tm, D), F32) + 3 * _nbytes((tm, D), MM_DTYPE)
    return pl.pallas_call(
        functools.partial(_merge_kernel, tm=tm),
        grid=(B, T // tm),
        in_specs=[pl.BlockSpec((1, tm, D), row)] * 4 + [_layer_block(w.shape, layer) for w in params],
        out_specs=pl.BlockSpec((1, tm, D), row),
        out_shape=jax.ShapeDtypeStruct((B, T, D), F32),
        compiler_params=pltpu.CompilerParams(
            dimension_semantics=("parallel", "parallel"),
            vmem_limit_bytes=_vmem_limit(blocks, wbytes, 12 * _nbytes((min(tm, SUB_ROWS), D), F32))),
        name="merge",
    )(h, oa, ob, d, *params)


def _ffn_kernel(h_ref, wu_ref, wd_ref, g_ref, b_ref, out_ref, act_ref, *, tm):
    sub = min(tm, SUB_ROWS)
    for r in range(0, tm, sub):
        h = h_ref[0, r:r + sub, :]
        hb = h.astype(MM_DTYPE)
        for c in range(D_FF // FF_CHUNK):
            c0 = c * FF_CHUNK
            gate = jnp.dot(hb, wu_ref[0, :, c0:c0 + FF_CHUNK], preferred_element_type=F32)
            up = jnp.dot(hb, wu_ref[0, :, D_FF + c0:D_FF + c0 + FF_CHUNK], preferred_element_type=F32)
            act_ref[r:r + sub, c0:c0 + FF_CHUNK] = (jax.nn.silu(gate) * up).astype(act_ref.dtype)
        f = jnp.dot(act_ref[r:r + sub, :], wd_ref[0], preferred_element_type=F32)
        out_ref[0, r:r + sub, :] = _layer_norm(ALPHA * h + f, g_ref[0], b_ref[0])


def _ffn(h, p, layer, *, tm):
    B, T, D = h.shape
    row = lambda bb, i: (bb, i, 0)
    params = [p[name] for name in ("wu", "wd", "ln2_g", "ln2_b")]
    wbytes = sum(_nbytes(w.shape[1:], w.dtype) for w in params)
    return pl.pallas_call(
        functools.partial(_ffn_kernel, tm=tm),
        grid=(B, T // tm),
        in_specs=[pl.BlockSpec((1, tm, D), row)] + [_layer_block(w.shape, layer) for w in params],
        out_specs=pl.BlockSpec((1, tm, D), row),
        out_shape=jax.ShapeDtypeStruct((B, T, D), F32),
        scratch_shapes=[pltpu.VMEM((tm, D_FF), MM_DTYPE)],
        compiler_params=pltpu.CompilerParams(
            dimension_semantics=("parallel", "parallel"),
            vmem_limit_bytes=_vmem_limit(2 * _nbytes((tm, D), F32), wbytes + _nbytes((tm, D_FF), MM_DTYPE),
                                         6 * _nbytes((tm, D), F32))),
        name="ffn",
    )(h, *params)


def _cos_sin(p, dim):
    inv = ROPE_THETA ** (-jnp.arange(0, dim, 2, dtype=F32) / dim)
    ang = p.astype(F32)[:, None] * inv[None, :]
    return jnp.cos(ang), jnp.sin(ang)


def _rope_tables(row_cs, col_cs, pos):
    cr, sr = row_cs
    cc, sc = col_cs
    cp, sp = _cos_sin(pos, HEAD_DIM)
    ca = jnp.concatenate([cr, cr, cc, cc], axis=1)
    sa = jnp.concatenate([-sr, sr, -sc, sc], axis=1)
    cb = jnp.concatenate([cp, cp], axis=1)
    sb = jnp.concatenate([-sp, sp], axis=1)
    return ca, sa, cb, sb


def _real_tables(S):
    half = HEAD_DIM // 2
    n_rows = S // GRID_W
    cr, sr = _cos_sin(jnp.arange(n_rows, dtype=jnp.int32), half)
    cc, sc = _cos_sin(jnp.arange(GRID_W, dtype=jnp.int32), half)
    zr, zc = jnp.zeros_like(cr), jnp.zeros_like(cc)

    def axial(row_part, col_part):
        return (row_part[:, None, :] + col_part[None, :, :]).reshape(S, HEAD_DIM)

    ca = axial(jnp.concatenate([cr, cr, zr, zr], axis=1), jnp.concatenate([zc, zc, cc, cc], axis=1))
    sa = axial(jnp.concatenate([-sr, sr, zr, zr], axis=1), jnp.concatenate([zc, zc, -sc, sc], axis=1))
    cp, sp = _cos_sin(jnp.arange(S, dtype=jnp.int32) + N_META, HEAD_DIM)
    return ca, sa, jnp.concatenate([cp, cp], axis=1), jnp.concatenate([-sp, sp], axis=1)


def _meta_tables():
    i = jnp.arange(META_ROWS, dtype=jnp.int32)
    half = HEAD_DIM // 2
    return _rope_tables(_cos_sin(-jnp.ones((META_ROWS,), jnp.int32), half), _cos_sin(i, half), i)


def _layer(xr, xm, p, layer, tabs_r, tabs_m):
    qa, ka, va, qb, kb, vb, d = _proj(xr, xm, p, layer, tabs_r, tm=ROW_TILE, cka=ATTN_TK, ckb=WIN_BLK)
    qam, kam, vam, qbm, kbm, vbm, dm = _proj_meta(xm, xr, p, layer, tabs_m)
    oa = _attn_a(qa, ka, va, kam, vam, tq=ATTN_TQ, tk=ATTN_TK)
    oam = _attn_a(qam, ka, va, kam, vam, tq=META_ROWS, tk=ATTN_TK, packed=True)
    ob = _attn_b(p["sink"], layer, qb, kb, vb, kbm, vbm, tq=WIN_TQ)
    obm = _attn_b_meta(p["sink"], layer, qbm, kb, vb, kbm, vbm)
    hr = _merge(xr, oa, ob, d, p, layer, tm=DENSE_TILE)
    hm = _merge(xm, oam, obm, dm, p, layer, tm=META_ROWS)
    return _ffn(hr, p, layer, tm=DENSE_TILE), _ffn(hm, p, layer, tm=META_ROWS)


def _trunk(x, meta_tokens, p, tabs_r, tabs_m):
    B, S, D = x.shape
    xm = jnp.zeros((B, META_ROWS, D), x.dtype).at[:, :N_META].set(meta_tokens.astype(x.dtype)[None])
    xr = x
    for layer in range(p["w_qkv"].shape[0]):
        xr, xm = _layer(xr, xm, p, layer, tabs_r, tabs_m)
    return xr


def _stack_params(w_in, q_norm_g, k_norm_g, sink_logit, pool_w, pool_scale,
                  w_branch_a, w_branch_b, w_out, ln1_g, ln1_b, w_up, w_down, ln2_g, ln2_b):
    mm = lambda w: w.astype(MM_DTYPE)
    vec = lambda v: v.astype(F32)[:, None, :]
    return dict(
        w_qkv=mm(w_in[:, :, :QKV_WIDTH]), wg=mm(w_in[:, :, QKV_WIDTH:]),
        qg=vec(q_norm_g) * QK_SCALE, kg=vec(k_norm_g), sink=sink_logit.astype(F32),
        pw=mm(pool_w), ps=vec(pool_scale),
        wa=mm(w_branch_a), wb=mm(w_branch_b), wo=mm(w_out),
        ln1_g=vec(ln1_g), ln1_b=vec(ln1_b),
        wu=mm(w_up), wd=mm(w_down),
        ln2_g=vec(ln2_g), ln2_b=vec(ln2_b),
    )


def kernel(x_prompt, x_sample, meta_tokens, w_in, q_norm_g, k_norm_g, sink_logit, pool_w, pool_scale,
           w_branch_a, w_branch_b, w_out, ln1_g, ln1_b, w_up, w_down, ln2_g, ln2_b):
    params = _stack_params(w_in, q_norm_g, k_norm_g, sink_logit, pool_w, pool_scale,
                           w_branch_a, w_branch_b, w_out, ln1_g, ln1_b, w_up, w_down, ln2_g, ln2_b)
    tabs_m = _meta_tables()
    tabs_s = _real_tables(x_sample.shape[1])
    tabs_p = tuple(t[:x_prompt.shape[1]] for t in tabs_s)
    y_prompt = _trunk(x_prompt, meta_tokens, params, tabs_p, tabs_m)
    y_sample = _trunk(x_sample, meta_tokens, params, tabs_s, tabs_m)
    return (y_prompt, y_sample)
```
